```python
import math
import jax, jax.numpy as jnp
from jax import lax
import numpy as np

D_MODEL = 1024
BATCH = 4
SEQ = 4096
DEPTH = 1

DA_HEADS = 4
DA_HEAD_DIM = 64
DA_WIDTH = DA_HEADS * 2 * DA_HEAD_DIM
SB_HEADS = 8
SB_HEAD_DIM = 64
SB_WIDTH = SB_HEADS * SB_HEAD_DIM
ROPE_THETA = 10000.0
Q_BLOCK = 128
IN_SIZES = (DA_WIDTH, DA_WIDTH, DA_WIDTH, SB_WIDTH, SB_WIDTH, SB_WIDTH, D_MODEL, D_MODEL)
IN_WIDTH = 3 * DA_WIDTH + 3 * SB_WIDTH + 2 * D_MODEL
N_EXPERTS = 32
TOP_K = 4
D_FF = D_MODEL
SWIGLU_LIMIT = 7.0
SWIGLU_ALPHA = 1.702
MOE_BLOCK = 256
NORM_EPS = 1e-5

kernel_name = "hybrid_diffattn_stickbreak_moe"


def rms_norm(x, g):
    xf = x.astype(jnp.float32)
    y = xf * lax.rsqrt(jnp.mean(xf * xf, axis=-1, keepdims=True) + NORM_EPS)
    return (y * g.astype(jnp.float32)).astype(x.dtype)


def rope_tables(seq, dim):
    inv = 1.0 / (ROPE_THETA ** (jnp.arange(0, dim, 2, dtype=jnp.float32) / dim))
    ang = jnp.arange(seq, dtype=jnp.float32)[:, None] * inv[None, :]
    return jnp.cos(ang), jnp.sin(ang)


def apply_rope(x, cos, sin):
    xf = x.astype(jnp.float32)
    x1, x2 = jnp.split(xf, 2, axis=-1)
    c = cos[None, :, None, :]
    s = sin[None, :, None, :]
    return jnp.concatenate([x1 * c - x2 * s, x2 * c + x1 * s], axis=-1).astype(x.dtype)


def to_blocks(t):
    b, s = t.shape[:2]
    return jnp.moveaxis(t.reshape(b, s // Q_BLOCK, Q_BLOCK, *t.shape[2:]), 1, 0)


def from_blocks(t):
    t = jnp.moveaxis(t, 0, 1)
    return t.reshape(t.shape[0], t.shape[1] * t.shape[2], *t.shape[3:])


def diff_attention(q, k, v, lam, lambda_init, subln_g):
    b, s = q.shape[:2]
    nb = s // Q_BLOCK
    scale = DA_HEAD_DIM ** -0.5
    k_pos = jnp.arange(s)

    def block(args):
        qb, i = args
        q_pos = i * Q_BLOCK + jnp.arange(Q_BLOCK)
        sc = jnp.einsum('bqhd,bkhd->bhqk', qb, k, preferred_element_type=jnp.float32) * scale
        mask = k_pos[None, :] <= q_pos[:, None]
        p = jax.nn.softmax(jnp.where(mask, sc, -jnp.inf), axis=-1)
        p = p.reshape(b, DA_HEADS, 2, Q_BLOCK, s)
        w = p[:, :, 0] - lam * p[:, :, 1]
        return jnp.einsum('bhqk,bkhe->bqhe', w.astype(v.dtype), v)

    o = from_blocks(lax.map(block, (to_blocks(q), jnp.arange(nb))))
    o = rms_norm(o, subln_g) * (1.0 - lambda_init)
    return o.reshape(b, s, DA_WIDTH)


def stick_breaking_attention(q, k, v):
    b, s = q.shape[:2]
    nb = s // Q_BLOCK
    scale = SB_HEAD_DIM ** -0.5
    k_pos = jnp.arange(s)

    def block(args):
        qb, i = args
        q_pos = i * Q_BLOCK + jnp.arange(Q_BLOCK)
        z = jnp.einsum('bqhd,bkhd->bhqk', qb, k, preferred_element_type=jnp.float32) * scale
        mask = k_pos[None, :] < q_pos[:, None]
        log_1m = jnp.where(mask, jax.nn.log_sigmoid(-z), 0.0)
        suffix = lax.cumsum(log_1m, axis=3, reverse=True) - log_1m
        a = jnp.where(mask, jnp.exp(jax.nn.log_sigmoid(z) + suffix), 0.0)
        return jnp.einsum('bhqk,bkhd->bqhd', a.astype(v.dtype), v)

    o = from_blocks(lax.map(block, (to_blocks(q), jnp.arange(nb))))
    return o.reshape(b, s, SB_WIDTH)


def moe_ffn(h, w_router, b_router, w_gate, b_gate, w_up, b_up, w_down, b_down):
    n, d = h.shape
    logits = (h @ w_router + b_router).astype(jnp.float32)
    top_val, top_idx = lax.top_k(logits, TOP_K)
    gates = jax.nn.softmax(top_val, axis=-1).astype(h.dtype)
    m = n * TOP_K
    flat_e = top_idx.reshape(m)
    flat_tok = jnp.arange(m, dtype=jnp.int32) // TOP_K
    flat_g = gates.reshape(m)
    order = jnp.argsort(flat_e)
    sorted_e = flat_e[order]
    counts = jnp.bincount(flat_e, length=N_EXPERTS)
    padded = (counts + MOE_BLOCK - 1) // MOE_BLOCK * MOE_BLOCK
    starts = jnp.cumsum(counts) - counts
    pad_ends = jnp.cumsum(padded)
    pad_starts = pad_ends - padded
    dest = pad_starts[sorted_e] + jnp.arange(m, dtype=jnp.int32) - starts[sorted_e]
    n_blocks = (m + MOE_BLOCK - 1) // MOE_BLOCK + N_EXPERTS
    p_rows = n_blocks * MOE_BLOCK
    row_tok = jnp.zeros((p_rows,), jnp.int32).at[dest].set(flat_tok[order])
    row_gate = jnp.zeros((p_rows,), h.dtype).at[dest].set(flat_g[order])
    blk_e = jnp.minimum(
        jnp.searchsorted(pad_ends, jnp.arange(n_blocks, dtype=jnp.int32) * MOE_BLOCK, side='right'),
        N_EXPERTS - 1)
    xs = h[row_tok].reshape(n_blocks, MOE_BLOCK, d)

    def expert_block(args):
        xb, e = args
        g = xb @ w_gate[e] + b_gate[e]
        u = xb @ w_up[e] + b_up[e]
        g = jnp.minimum(g, SWIGLU_LIMIT)
        u = jnp.clip(u, -SWIGLU_LIMIT, SWIGLU_LIMIT)
        glu = g * jax.nn.sigmoid(SWIGLU_ALPHA * g)
        return ((u + 1.0) * glu) @ w_down[e] + b_down[e]

    out = lax.map(expert_block, (xs, blk_e)).reshape(p_rows, d)
    return jnp.zeros_like(h).at[row_tok].add(out * row_gate[:, None])


def setup_inputs(seed: int = 0) -> dict:
    key = jax.random.key(seed)
    ks = jax.random.split(key, 24)
    f32 = jnp.float32
    L, D, E, F = DEPTH, D_MODEL, N_EXPERTS, D_FF

    def nrm(k, shape, scale):
        return jax.random.normal(k, shape, f32) * scale

    return {
        "x": nrm(ks[0], (BATCH, SEQ, D), 1.0),
        "norm_mix_g": 1.0 + nrm(ks[1], (L, D), 0.01),
        "w_in": nrm(ks[2], (L, D, IN_WIDTH), D ** -0.5),
        "lambda_q1": nrm(ks[3], (L, DA_HEAD_DIM), 0.1),
        "lambda_k1": nrm(ks[4], (L, DA_HEAD_DIM), 0.1),
        "lambda_q2": nrm(ks[5], (L, DA_HEAD_DIM), 0.1),
        "lambda_k2": nrm(ks[6], (L, DA_HEAD_DIM), 0.1),
        "da_subln_g": 1.0 + nrm(ks[7], (L, 2 * DA_HEAD_DIM), 0.01),
        "w_da_out": nrm(ks[8], (L, DA_WIDTH, D), DA_WIDTH ** -0.5),
        "w_sb_out": nrm(ks[9], (L, SB_WIDTH, D), SB_WIDTH ** -0.5),
        "w_o": nrm(ks[10], (L, D, D), D ** -0.5),
        "norm_ffn_g": 1.0 + nrm(ks[11], (L, D), 0.01),
        "w_router": nrm(ks[12], (L, D, E), D ** -0.5),
        "b_router": nrm(ks[13], (L, E), 0.01),
        "w_gate": nrm(ks[14], (L, E, D, F), D ** -0.5),
        "b_gate": nrm(ks[15], (L, E, F), 0.01),
        "w_up": nrm(ks[16], (L, E, D, F), D ** -0.5),
        "b_up": nrm(ks[17], (L, E, F), 0.01),
        "w_down": nrm(ks[18], (L, E, F, D), F ** -0.5),
        "b_down": nrm(ks[19], (L, E, D), 0.01),
        "norm_final_g": 1.0 + nrm(ks[20], (D,), 0.01),
    }


def reference(x, norm_mix_g, w_in, lambda_q1, lambda_k1, lambda_q2, lambda_k2, da_subln_g,
              w_da_out, w_sb_out, w_o, norm_ffn_g, w_router, b_router, w_gate, b_gate,
              w_up, b_up, w_down, b_down, norm_final_g):
    b, s, d = x.shape
    cos, sin = rope_tables(s, DA_HEAD_DIM)
    split_points = np.cumsum(IN_SIZES)[:-1].tolist()
    for l in range(DEPTH):
        lambda_init = 0.8 - 0.6 * math.exp(-0.3 * l)
        h = rms_norm(x, norm_mix_g[l])
        proj = h @ w_in[l]
        qa, ka, va, qb, kb, vb, ga, gb = jnp.split(proj, split_points, axis=-1)
        qa = apply_rope(qa.reshape(b, s, 2 * DA_HEADS, DA_HEAD_DIM), cos, sin)
        ka = apply_rope(ka.reshape(b, s, 2 * DA_HEADS, DA_HEAD_DIM), cos, sin)
        va = va.reshape(b, s, DA_HEADS, 2 * DA_HEAD_DIM)
        lam = (jnp.exp(jnp.sum(lambda_q1[l].astype(jnp.float32) * lambda_k1[l].astype(jnp.float32)))
               - jnp.exp(jnp.sum(lambda_q2[l].astype(jnp.float32) * lambda_k2[l].astype(jnp.float32)))
               + lambda_init)
        ya = diff_attention(qa, ka, va, lam, lambda_init, da_subln_g[l]) @ w_da_out[l]
        yb = stick_breaking_attention(qb.reshape(b, s, SB_HEADS, SB_HEAD_DIM),
                                      kb.reshape(b, s, SB_HEADS, SB_HEAD_DIM),
                                      vb.reshape(b, s, SB_HEADS, SB_HEAD_DIM)) @ w_sb_out[l]
        mix = jax.nn.sigmoid(ga) * ya + jax.nn.sigmoid(gb) * yb
        x = x + mix @ w_o[l]
        h = rms_norm(x, norm_ffn_g[l]).reshape(b * s, d)
        y = moe_ffn(h, w_router[l], b_router[l], w_gate[l], b_gate[l], w_up[l], b_up[l],
                    w_down[l], b_down[l])
        x = x + y.reshape(b, s, d)
    return rms_norm(x, norm_final_g)
```

```python
import functools
import math

import jax
import jax.numpy as jnp
from jax import lax
from jax.experimental import pallas as pl
from jax.experimental.pallas import tpu as pltpu

F32 = jnp.float32
BF16 = jnp.bfloat16

DA_HEADS = 4
HEAD_DIM = 64
N_EXPERTS = 32
TOP_K = 4
ROPE_THETA = 10000.0
SWIGLU_LIMIT = 7.0
SWIGLU_ALPHA = 1.702
NORM_EPS = 1e-5
ROW_BLOCK = 256
LANES = 128
NEG_BIG = -1e30

VMEM_LIMIT = 56 * 1024 * 1024


def _params(sem, vmem=VMEM_LIMIT):
    return pltpu.CompilerParams(dimension_semantics=sem, vmem_limit_bytes=vmem)


def _in_proj_kernel(x_ref, g_ref, w_ref, cos_ref, sin_ref,
                    qa_ref, ka_ref, va_ref, qb_ref, kb_ref, vb_ref, ga_ref, gb_ref):
    x = x_ref[...]
    ms = jnp.mean(x * x, axis=-1, keepdims=True)
    h = (x * lax.rsqrt(ms + NORM_EPS) * g_ref[...]).astype(BF16)
    cos = cos_ref[...]
    sin = sin_ref[...]
    lane = lax.broadcasted_iota(jnp.int32, cos.shape, 1)
    first_half = (lane & (HEAD_DIM - 1)) < HEAD_DIM // 2

    def proj(c0, width):
        return jnp.dot(h, w_ref[:, c0:c0 + width], preferred_element_type=F32)

    def rope(r):
        outs = []
        for g in range(r.shape[1] // LANES):
            xg = r[:, g * LANES:(g + 1) * LANES]
            rot = jnp.where(first_half, pltpu.roll(xg, LANES - HEAD_DIM // 2, 1),
                            pltpu.roll(xg, HEAD_DIM // 2, 1))
            outs.append(xg * cos + rot * sin)
        return jnp.concatenate(outs, axis=1)

    scale = HEAD_DIM ** -0.5
    w = qa_ref.shape[1]
    d = ga_ref.shape[1]
    qa_ref[...] = (rope(proj(0, w)) * scale).astype(BF16)
    ka_ref[...] = rope(proj(w, w)).astype(BF16)
    va_ref[...] = proj(2 * w, w).astype(BF16)
    qb_ref[...] = (proj(3 * w, w) * scale).astype(BF16)
    kb_ref[...] = proj(4 * w, w).astype(BF16)
    vb_ref[...] = proj(5 * w, w).astype(BF16)
    ga_ref[...] = jax.nn.sigmoid(proj(6 * w, d)).astype(BF16)
    gb_ref[...] = jax.nn.sigmoid(proj(6 * w + d, d)).astype(BF16)


def _in_proj(x2, g, w_in_bf, cos_t, sin_t, seq, tm=512):
    t, d = x2.shape
    w = 512
    nseq = seq // tm
    outs = [jax.ShapeDtypeStruct((t, w), BF16)] * 6 + [jax.ShapeDtypeStruct((t, d), BF16)] * 2
    row = lambda i: (i, 0)
    return pl.pallas_call(
        _in_proj_kernel,
        out_shape=outs,
        grid=(t // tm,),
        in_specs=[
            pl.BlockSpec((tm, d), row),
            pl.BlockSpec((1, d), lambda i: (0, 0)),
            pl.BlockSpec(w_in_bf.shape, lambda i: (0, 0)),
            pl.BlockSpec((tm, LANES), lambda i: (i % nseq, 0)),
            pl.BlockSpec((tm, LANES), lambda i: (i % nseq, 0)),
        ],
        out_specs=[pl.BlockSpec((tm, w), row)] * 6 + [pl.BlockSpec((tm, d), row)] * 2,
        compiler_params=_params(("arbitrary",)),
        name="in_proj",
    )(x2, g, w_in_bf, cos_t, sin_t)


def _da_kernel(q_ref, k_ref, v_ref, lam_ref, subg_ref, o_ref, m_ref, l_ref, acc_ref,
               *, tq, tk, lambda_init):
    i = pl.program_id(2)
    q = q_ref[0]
    lane = lax.broadcasted_iota(jnp.int32, q.shape, 1)
    zero = jnp.zeros_like(q)
    qs = (jnp.where(lane < HEAD_DIM, q, zero), jnp.where(lane >= HEAD_DIM, q, zero))
    m_ref[...] = jnp.full(m_ref.shape, NEG_BIG, F32)
    l_ref[...] = jnp.zeros(l_ref.shape, F32)
    acc_ref[...] = jnp.zeros(acc_ref.shape, F32)

    def block(j, masked):
        start = pl.multiple_of(j * tk, tk)
        k = k_ref[0, pl.ds(start, tk), :]
        v = v_ref[0, pl.ds(start, tk), :]
        for h in range(2):
            s = lax.dot_general(qs[h], k, (((1,), (1,)), ((), ())), preferred_element_type=F32)
            if masked:
                rows = lax.broadcasted_iota(jnp.int32, s.shape, 0)
                cols = lax.broadcasted_iota(jnp.int32, s.shape, 1)
                s = jnp.where(rows >= cols, s, NEG_BIG)
            m_prev = m_ref[h]
            m_new = jnp.maximum(m_prev, jnp.max(s, axis=1, keepdims=True))
            alpha = jnp.exp(m_prev - m_new)
            p = jnp.exp(s - m_new)
            l_ref[h] = alpha * l_ref[h] + jnp.sum(p, axis=1, keepdims=True)
            acc_ref[h] = alpha * acc_ref[h] + jnp.dot(p.astype(BF16), v, preferred_element_type=F32)
            m_ref[h] = m_new

    def body(j, carry):
        block(j, False)
        return carry

    lax.fori_loop(0, i, body, 0)
    block(i, True)

    lam_p = lam_ref[...]
    lam = (jnp.exp(jnp.sum(lam_p[0:1] * lam_p[1:2], axis=1, keepdims=True))
           - jnp.exp(jnp.sum(lam_p[2:3] * lam_p[3:4], axis=1, keepdims=True)) + lambda_init)
    o = acc_ref[0] / l_ref[0] - lam * (acc_ref[1] / l_ref[1])
    ms = jnp.mean(o * o, axis=-1, keepdims=True)
    y = o * lax.rsqrt(ms + NORM_EPS) * subg_ref[...] * (1.0 - lambda_init)
    o_ref[0] = y.astype(o_ref.dtype)


def _diff_attn(qa, ka, va, lam_p, subg, lambda_init, tq=512):
    b, s, w = qa.shape
    nh = w // LANES
    kern = functools.partial(_da_kernel, tq=tq, tk=tq, lambda_init=lambda_init)
    qspec = pl.BlockSpec((1, tq, LANES), lambda bb, h, i: (bb, i, h))
    kvspec = pl.BlockSpec((1, s, LANES), lambda bb, h, i: (bb, 0, h))
    return pl.pallas_call(
        kern,
        out_shape=jax.ShapeDtypeStruct((b, s, w), BF16),
        grid=(b, nh, s // tq),
        in_specs=[qspec, kvspec, kvspec,
                  pl.BlockSpec(lam_p.shape, lambda bb, h, i: (0, 0)),
                  pl.BlockSpec(subg.shape, lambda bb, h, i: (0, 0))],
        out_specs=qspec,
        scratch_shapes=[pltpu.VMEM((2, tq, 1), F32), pltpu.VMEM((2, tq, 1), F32),
                        pltpu.VMEM((2, tq, LANES), F32)],
        compiler_params=_params(("arbitrary",) * 3),
        name="diff_attn",
    )(qa, ka, va, lam_p, subg)


def _sb_kernel(q_ref, k_ref, v_ref, tri_ref, o_ref, c_ref, acc_ref, *, tq, tk):
    i = pl.program_id(2)
    q = q_ref[0]
    lane = lax.broadcasted_iota(jnp.int32, q.shape, 1)
    zero = jnp.zeros_like(q)
    qs = (jnp.where(lane < HEAD_DIM, q, zero), jnp.where(lane >= HEAD_DIM, q, zero))
    c_ref[...] = jnp.zeros(c_ref.shape, F32)
    acc_ref[...] = jnp.zeros(acc_ref.shape, F32)
    tri = tri_ref[...]

    def block(j, masked):
        start = pl.multiple_of(j * tk, tk)
        k = k_ref[0, pl.ds(start, tk), :]
        v = v_ref[0, pl.ds(start, tk), :]
        for h in range(2):
            z = lax.dot_general(qs[h], k, (((1,), (1,)), ((), ())), preferred_element_type=F32)
            sp = jnp.maximum(z, 0.0) + jnp.log(1.0 + jnp.exp(-jnp.abs(z)))
            log_1m = -sp
            if masked:
                qpos = i * tq + lax.broadcasted_iota(jnp.int32, z.shape, 0)
                kpos = start + lax.broadcasted_iota(jnp.int32, z.shape, 1)
                mask = kpos < qpos
                log_1m = jnp.where(mask, log_1m, 0.0)
            suffix = jnp.dot(log_1m.astype(BF16), tri, preferred_element_type=F32) + c_ref[h]
            a = jnp.exp((z - sp) + suffix)
            if masked:
                a = jnp.where(mask, a, 0.0)
            acc_ref[h] = acc_ref[h] + jnp.dot(a.astype(BF16), v, preferred_element_type=F32)
            c_ref[h] = c_ref[h] + jnp.sum(log_1m, axis=1, keepdims=True)

    ratio = tq // tk
    last = (i + 1) * ratio - 1
    for r in range(ratio):
        block(last - r, True)

    def body(t, carry):
        block(i * ratio - 1 - t, False)
        return carry

    lax.fori_loop(0, i * ratio, body, 0)
    o_ref[0] = jnp.where(lane < HEAD_DIM, acc_ref[0], acc_ref[1]).astype(o_ref.dtype)


def _sb_attn(qb, kb, vb, tq=512, tk=256):
    b, s, w = qb.shape
    nh = w // LANES
    r = lax.broadcasted_iota(jnp.int32, (tk, tk), 0)
    c = lax.broadcasted_iota(jnp.int32, (tk, tk), 1)
    tri = (r > c).astype(BF16)
    kern = functools.partial(_sb_kernel, tq=tq, tk=tk)
    qspec = pl.BlockSpec((1, tq, LANES), lambda bb, h, i: (bb, i, h))
    kvspec = pl.BlockSpec((1, s, LANES), lambda bb, h, i: (bb, 0, h))
    return pl.pallas_call(
        kern,
        out_shape=jax.ShapeDtypeStruct((b, s, w), BF16),
        grid=(b, nh, s // tq),
        in_specs=[qspec, kvspec, kvspec, pl.BlockSpec((tk, tk), lambda bb, h, i: (0, 0))],
        out_specs=qspec,
        scratch_shapes=[pltpu.VMEM((2, tq, 1), F32), pltpu.VMEM((2, tq, LANES), F32)],
        compiler_params=_params(("arbitrary",) * 3),
        name="sb_attn",
    )(qb, kb, vb, tri)


def _post_attn_kernel(oa_ref, ob_ref, ga_ref, gb_ref, x_ref, wda_ref, wsb_ref, wo_ref, g_ref,
                      wr_hi_ref, wr_lo_ref, br_ref, tri_ref,
                      x1_ref, h_ref, idx_ref, gate_ref, rank_ref, cnt_ref, carry_ref):
    i = pl.program_id(0)

    @pl.when(i == 0)
    def _():
        carry_ref[...] = jnp.zeros(carry_ref.shape, F32)

    ya = jnp.dot(oa_ref[...], wda_ref[...], preferred_element_type=F32)
    yb = jnp.dot(ob_ref[...], wsb_ref[...], preferred_element_type=F32)
    mix = ga_ref[...].astype(F32) * ya + gb_ref[...].astype(F32) * yb
    x1 = x_ref[...] + jnp.dot(mix.astype(BF16), wo_ref[...], preferred_element_type=F32)
    x1_ref[...] = x1
    ms = jnp.mean(x1 * x1, axis=-1, keepdims=True)
    h = x1 * lax.rsqrt(ms + NORM_EPS) * g_ref[...]
    h_ref[...] = h

    h_hi = h.astype(BF16)
    h_lo = (h - h_hi.astype(F32)).astype(BF16)
    nt = (((1,), (1,)), ((), ()))
    logits = (lax.dot_general(wr_hi_ref[...], h_hi, nt, preferred_element_type=F32)
              + lax.dot_general(wr_hi_ref[...], h_lo, nt, preferred_element_type=F32)
              + lax.dot_general(wr_lo_ref[...], h_hi, nt, preferred_element_type=F32)
              + br_ref[...])
    ne, tm = logits.shape
    eid = lax.broadcasted_iota(jnp.int32, (ne, tm), 0).astype(F32)
    vals, ids = [], []
    work = logits
    for _ in range(TOP_K):
        mx = jnp.max(work, axis=0, keepdims=True)
        sel = jnp.min(jnp.where(work == mx, eid, float(ne)), axis=0, keepdims=True)
        vals.append(mx)
        ids.append(sel)
        work = jnp.where(eid == sel, -jnp.inf, work)
    exps = [jnp.exp(v - vals[0]) for v in vals]
    denom = exps[0] + exps[1] + exps[2] + exps[3]
    onehots = [(eid == sel).astype(F32) for sel in ids]
    assigned = onehots[0] + onehots[1] + onehots[2] + onehots[3]
    before = jnp.dot(assigned.astype(BF16), tri_ref[...], preferred_element_type=F32) + carry_ref[...]
    for r in range(TOP_K):
        idx_ref[r:r + 1, :] = ids[r].astype(jnp.int32)
        gate_ref[r:r + 1, :] = exps[r] / denom
        rank_ref[r:r + 1, :] = jnp.sum(onehots[r] * before, axis=0, keepdims=True).astype(jnp.int32)
    carry = carry_ref[...] + jnp.sum(assigned, axis=1, keepdims=True)
    carry_ref[...] = carry
    cnt_ref[...] = jnp.broadcast_to(carry, cnt_ref.shape)


def _post_attn(oa, ob, ga, gb, x2, wda, wsb, wo, g, wr_hi, wr_lo, br, tm=512):
    t, d = x2.shape
    w = oa.shape[1]
    r = lax.broadcasted_iota(jnp.int32, (tm, tm), 0)
    c = lax.broadcasted_iota(jnp.int32, (tm, tm), 1)
    tri = (r < c).astype(BF16)
    row = lambda i: (i, 0)
    col = lambda i: (0, i)
    const = lambda i: (0, 0)
    full = lambda a: pl.BlockSpec(a.shape, const)
    return pl.pallas_call(
        _post_attn_kernel,
        out_shape=[jax.ShapeDtypeStruct((t, d), F32), jax.ShapeDtypeStruct((t, d), F32),
                   jax.ShapeDtypeStruct((TOP_K, t), jnp.int32), jax.ShapeDtypeStruct((TOP_K, t), F32),
                   jax.ShapeDtypeStruct((TOP_K, t), jnp.int32),
                   jax.ShapeDtypeStruct((N_EXPERTS, LANES), F32)],
        grid=(t // tm,),
        in_specs=[pl.BlockSpec((tm, w), row), pl.BlockSpec((tm, w), row),
                  pl.BlockSpec((tm, d), row), pl.BlockSpec((tm, d), row), pl.BlockSpec((tm, d), row),
                  full(wda), full(wsb), full(wo), full(g), full(wr_hi), full(wr_lo), full(br), full(tri)],
        out_specs=[pl.BlockSpec((tm, d), row), pl.BlockSpec((tm, d), row),
                   pl.BlockSpec((TOP_K, tm), col), pl.BlockSpec((TOP_K, tm), col),
                   pl.BlockSpec((TOP_K, tm), col), pl.BlockSpec((N_EXPERTS, LANES), const)],
        scratch_shapes=[pltpu.VMEM((N_EXPERTS, 1), F32)],
        compiler_params=_params(("arbitrary",)),
        name="post_attn",
    )(oa, ob, ga, gb, x2, wda, wsb, wo, g, wr_hi, wr_lo, br, tri)


def _dispatch_kernel(dest_ref, h_ref, xs_in_ref, xs_ref, sem, *, tm, t_total):
    del xs_in_ref
    i = pl.program_id(0)

    def row_copy(t, k):
        dst = dest_ref[k * t_total + i * tm + t]
        return pltpu.make_async_copy(h_ref.at[pl.ds(t, 1), :], xs_ref.at[pl.ds(dst, 1), :], sem)

    def issue(t, carry):
        for k in range(TOP_K):
            row_copy(t, k).start()
        return carry

    lax.fori_loop(0, tm, issue, 0, unroll=8)
    for _ in range(TOP_K):
        pltpu.make_async_copy(h_ref, xs_ref.at[pl.ds(0, tm), :], sem).wait()


def _dispatch(dest_flat, h, xs_zero, tm=256):
    t, d = h.shape
    kern = functools.partial(_dispatch_kernel, tm=tm, t_total=t)
    return pl.pallas_call(
        kern,
        out_shape=jax.ShapeDtypeStruct(xs_zero.shape, xs_zero.dtype),
        grid_spec=pltpu.PrefetchScalarGridSpec(
            num_scalar_prefetch=1,
            grid=(t // tm,),
            in_specs=[pl.BlockSpec((tm, d), lambda i, dest: (i, 0)),
                      pl.BlockSpec(memory_space=pl.ANY)],
            out_specs=pl.BlockSpec(memory_space=pl.ANY),
            scratch_shapes=[pltpu.SemaphoreType.DMA],
        ),
        input_output_aliases={2: 0},
        compiler_params=_params(("arbitrary",)),
        name="dispatch",
    )(dest_flat, h, xs_zero)


def _expert_kernel(blk_e_ref, n_used_ref, xs_ref, wg_ref, bg_ref, wu_ref, bu_ref, wd_ref, bd_ref,
                   o_ref, wg_bf, wu_bf, wd_bf):
    i = pl.program_id(0)
    e = blk_e_ref[i]
    prev = blk_e_ref[jnp.maximum(i - 1, 0)]
    used = i < n_used_ref[0]

    @pl.when(used & ((i == 0) | (e != prev)))
    def _():
        wg_bf[...] = wg_ref[0].astype(BF16)
        wu_bf[...] = wu_ref[0].astype(BF16)
        wd_bf[...] = wd_ref[0].astype(BF16)

    @pl.when(used)
    def _():
        x = xs_ref[...].astype(BF16)
        g = jnp.dot(x, wg_bf[...], preferred_element_type=F32) + bg_ref[0]
        u = jnp.dot(x, wu_bf[...], preferred_element_type=F32) + bu_ref[0]
        g = jnp.minimum(g, SWIGLU_LIMIT)
        u = jnp.clip(u, -SWIGLU_LIMIT, SWIGLU_LIMIT)
        glu = g * jax.nn.sigmoid(SWIGLU_ALPHA * g)
        act = ((u + 1.0) * glu).astype(BF16)
        o_ref[...] = jnp.dot(act, wd_bf[...], preferred_element_type=F32) + bd_ref[0]

    @pl.when(jnp.logical_not(used))
    def _():
        o_ref[...] = jnp.zeros(o_ref.shape, o_ref.dtype)


def _experts(blk_e, n_used, xs, wg, bg, wu, bu, wd, bd):
    p, d = xs.shape
    f = wg.shape[2]
    nb = p // ROW_BLOCK
    wmap = lambda i, be, nu: (be[i], 0, 0)
    xmap = lambda i, be, nu: (jnp.minimum(i, nu[0] - 1), 0)
    return pl.pallas_call(
        _expert_kernel,
        out_shape=jax.ShapeDtypeStruct((p, d), F32),
        grid_spec=pltpu.PrefetchScalarGridSpec(
            num_scalar_prefetch=2,
            grid=(nb,),
            in_specs=[pl.BlockSpec((ROW_BLOCK, d), xmap),
                      pl.BlockSpec((1, d, f), wmap), pl.BlockSpec((1, 1, f), wmap),
                      pl.BlockSpec((1, d, f), wmap), pl.BlockSpec((1, 1, f), wmap),
                      pl.BlockSpec((1, f, d), wmap), pl.BlockSpec((1, 1, d), wmap)],
            out_specs=pl.BlockSpec((ROW_BLOCK, d), lambda i, be, nu: (i, 0)),
            scratch_shapes=[pltpu.VMEM((d, f), BF16), pltpu.VMEM((d, f), BF16), pltpu.VMEM((f, d), BF16)],
        ),
        compiler_params=_params(("arbitrary",)),
        name="experts",
    )(blk_e, n_used, xs, wg, bg, wu, bu, wd, bd)


def _combine_kernel(dest_ref, rows_ref, gate_ref, x1_ref, g_ref, o_ref, buf, sem, *, tm, t_total):
    i = pl.program_id(0)

    def row_copy(t, k):
        src = dest_ref[k * t_total + i * tm + t]
        return pltpu.make_async_copy(rows_ref.at[pl.ds(src, 1), :], buf.at[k, pl.ds(t, 1), :], sem)

    def issue(t, carry):
        for k in range(TOP_K):
            row_copy(t, k).start()
        return carry

    lax.fori_loop(0, tm, issue, 0, unroll=8)
    for k in range(TOP_K):
        pltpu.make_async_copy(rows_ref.at[pl.ds(0, tm), :], buf.at[k], sem).wait()
    gates = gate_ref[...]
    y = x1_ref[...]
    for k in range(TOP_K):
        y = y + gates[:, k:k + 1] * buf[k]
    ms = jnp.mean(y * y, axis=-1, keepdims=True)
    o_ref[...] = y * lax.rsqrt(ms + NORM_EPS) * g_ref[...]


def _combine(dest_flat, rows, gates_t, x1, g, tm=256):
    t, d = x1.shape
    kern = functools.partial(_combine_kernel, tm=tm, t_total=t)
    return pl.pallas_call(
        kern,
        out_shape=jax.ShapeDtypeStruct((t, d), F32),
        grid_spec=pltpu.PrefetchScalarGridSpec(
            num_scalar_prefetch=1,
            grid=(t // tm,),
            in_specs=[pl.BlockSpec(memory_space=pl.ANY),
                      pl.BlockSpec((tm, TOP_K), lambda i, dest: (i, 0)),
                      pl.BlockSpec((tm, d), lambda i, dest: (i, 0)),
                      pl.BlockSpec((1, d), lambda i, dest: (0, 0))],
            out_specs=pl.BlockSpec((tm, d), lambda i, dest: (i, 0)),
            scratch_shapes=[pltpu.VMEM((TOP_K, tm, d), F32), pltpu.SemaphoreType.DMA],
        ),
        compiler_params=_params(("arbitrary",)),
        name="combine",
    )(dest_flat, rows, gates_t, x1, g)


def _rope_tables(seq):
    inv = 1.0 / (ROPE_THETA ** (jnp.arange(0, HEAD_DIM, 2, dtype=F32) / HEAD_DIM))
    ang = jnp.arange(seq, dtype=F32)[:, None] * inv[None, :]
    cos, sin = jnp.cos(ang), jnp.sin(ang)
    return jnp.concatenate([cos] * 4, axis=1), jnp.concatenate([-sin, sin, -sin, sin], axis=1)


def kernel(x, norm_mix_g, w_in, lambda_q1, lambda_k1, lambda_q2, lambda_k2, da_subln_g, w_da_out, w_sb_out, w_o, norm_ffn_g, w_router, b_router, w_gate, b_gate, w_up, b_up, w_down, b_down, norm_final_g):
    b, s, d = x.shape
    depth = w_in.shape[0]
    t = b * s
    cos_t, sin_t = _rope_tables(s)
    n_blocks = (t * TOP_K + ROW_BLOCK - 1) // ROW_BLOCK + N_EXPERTS
    p_rows = n_blocks * ROW_BLOCK
    x2 = x.reshape(t, d)
    for l in range(depth):
        lambda_init = 0.8 - 0.6 * math.exp(-0.3 * l)
        qa, ka, va, qb, kb, vb, ga, gb = _in_proj(
            x2, norm_mix_g[l][None, :], w_in[l].astype(BF16), cos_t, sin_t, s)
        lam_p = jnp.stack([lambda_q1[l], lambda_k1[l], lambda_q2[l], lambda_k2[l]]).astype(F32)
        seq3 = lambda a: a.reshape(b, s, a.shape[1])
        oa = _diff_attn(seq3(qa), seq3(ka), seq3(va), lam_p, da_subln_g[l][None, :].astype(F32), lambda_init)
        ob = _sb_attn(seq3(qb), seq3(kb), seq3(vb))
        wr = w_router[l].T.astype(F32)
        wr_hi = wr.astype(BF16)
        wr_lo = (wr - wr_hi.astype(F32)).astype(BF16)
        x1, h, idx, gates, rank, cnt = _post_attn(
            oa.reshape(t, -1), ob.reshape(t, -1), ga, gb, x2,
            w_da_out[l].astype(BF16), w_sb_out[l].astype(BF16), w_o[l].astype(BF16),
            norm_ffn_g[l][None, :], wr_hi, wr_lo, b_router[l][:, None].astype(F32))
        counts = cnt[:, 0].astype(jnp.int32)
        padded = (counts + ROW_BLOCK - 1) // ROW_BLOCK * ROW_BLOCK
        pad_ends = jnp.cumsum(padded)
        pad_starts = pad_ends - padded
        experts = jnp.arange(N_EXPERTS, dtype=jnp.int32)
        base = jnp.sum(jnp.where(idx[:, :, None] == experts, pad_starts, 0), axis=-1)
        dest = (base + rank).reshape(-1)
        n_used = (pad_ends[-1] // ROW_BLOCK).astype(jnp.int32)
        blk = jnp.minimum(jnp.arange(n_blocks, dtype=jnp.int32), n_used - 1) * ROW_BLOCK
        blk_e = jnp.minimum(jnp.sum(pad_ends[None, :] <= blk[:, None], axis=1), N_EXPERTS - 1).astype(jnp.int32)
        xs = _dispatch(dest, h, jnp.zeros((p_rows, d), F32))
        rows = _experts(blk_e, n_used[None], xs, w_gate[l], b_gate[l][:, None, :], w_up[l], b_up[l][:, None, :],
                        w_down[l], b_down[l][:, None, :])
        g_next = norm_final_g[None, :] if l == depth - 1 else jnp.ones((1, d), F32)
        x2 = _combine(dest, rows, gates.T, x1, g_next)
        if l != depth - 1:
            raise NotImplementedError("only the final layer's norm is fused into the combine kernel")
    return x2.reshape(b, s, d)
```

```python
import functools
import math

import jax
import jax.numpy as jnp
from jax import lax
from jax.experimental import pallas as pl
from jax.experimental.pallas import tpu as pltpu

F32 = jnp.float32
BF16 = jnp.bfloat16

DA_HEADS = 4
HEAD_DIM = 64
N_EXPERTS = 32
TOP_K = 4
ROPE_THETA = 10000.0
SWIGLU_LIMIT = 7.0
SWIGLU_ALPHA = 1.702
NORM_EPS = 1e-5
ROW_BLOCK = 256
LANES = 128
NEG_BIG = -1e30
LOG2E = math.log2(math.e)

VMEM_LIMIT = 56 * 1024 * 1024


def _params(sem, vmem=VMEM_LIMIT):
    return pltpu.CompilerParams(dimension_semantics=sem, vmem_limit_bytes=vmem)


def _in_proj_kernel(x_ref, g_ref, w_ref, cos_ref, sin_ref,
                    qa_ref, ka_ref, va_ref, qb_ref, kb_ref, vb_ref, ga_ref, gb_ref):
    x = x_ref[...]
    ms = jnp.mean(x * x, axis=-1, keepdims=True)
    h = (x * lax.rsqrt(ms + NORM_EPS) * g_ref[...]).astype(BF16)
    cos = cos_ref[...]
    sin = sin_ref[...]
    lane = lax.broadcasted_iota(jnp.int32, cos.shape, 1)
    first_half = (lane & (HEAD_DIM - 1)) < HEAD_DIM // 2

    def proj(c0, width):
        return jnp.dot(h, w_ref[:, c0:c0 + width], preferred_element_type=F32)

    def rope(r):
        outs = []
        for g in range(r.shape[1] // LANES):
            xg = r[:, g * LANES:(g + 1) * LANES]
            rot = jnp.where(first_half, pltpu.roll(xg, LANES - HEAD_DIM // 2, 1),
                            pltpu.roll(xg, HEAD_DIM // 2, 1))
            outs.append(xg * cos + rot * sin)
        return jnp.concatenate(outs, axis=1)

    scale = HEAD_DIM ** -0.5 * LOG2E
    w = qa_ref.shape[1]
    d = ga_ref.shape[1]
    qa_ref[...] = (rope(proj(0, w)) * scale).astype(BF16)
    ka_ref[...] = rope(proj(w, w)).astype(BF16)
    va_ref[...] = proj(2 * w, w).astype(BF16)
    qb_ref[...] = (proj(3 * w, w) * scale).astype(BF16)
    kb_ref[...] = proj(4 * w, w).astype(BF16)
    vb_ref[...] = proj(5 * w, w).astype(BF16)
    ga_ref[...] = jax.nn.sigmoid(proj(6 * w, d)).astype(BF16)
    gb_ref[...] = jax.nn.sigmoid(proj(6 * w + d, d)).astype(BF16)


def _in_proj(x2, g, w_in_bf, cos_t, sin_t, seq, tm=512):
    t, d = x2.shape
    w = 512
    nseq = seq // tm
    outs = [jax.ShapeDtypeStruct((t, w), BF16)] * 6 + [jax.ShapeDtypeStruct((t, d), BF16)] * 2
    row = lambda i: (i, 0)
    return pl.pallas_call(
        _in_proj_kernel,
        out_shape=outs,
        grid=(t // tm,),
        in_specs=[
            pl.BlockSpec((tm, d), row),
            pl.BlockSpec((1, d), lambda i: (0, 0)),
            pl.BlockSpec(w_in_bf.shape, lambda i: (0, 0)),
            pl.BlockSpec((tm, LANES), lambda i: (i % nseq, 0)),
            pl.BlockSpec((tm, LANES), lambda i: (i % nseq, 0)),
        ],
        out_specs=[pl.BlockSpec((tm, w), row)] * 6 + [pl.BlockSpec((tm, d), row)] * 2,
        compiler_params=_params(("arbitrary",)),
        name="in_proj",
    )(x2, g, w_in_bf, cos_t, sin_t)


def _lane_groups(x):
    return [x[:, g * LANES:(g + 1) * LANES] for g in range(x.shape[1] // LANES)]


def _da_kernel(q_ref, k_ref, v_ref, lam_ref, subg_ref, o_ref, s_buf, mx_ref, m_ref, acc_ref,
               *, tq, tk, lambda_init):
    i = pl.program_id(2)
    q = q_ref[0]
    lane = lax.broadcasted_iota(jnp.int32, q.shape, 1)
    zero = jnp.zeros_like(q)
    qs = (jnp.where(lane < HEAD_DIM, q, zero), jnp.where(lane >= HEAD_DIM, q, zero))
    mx_ref[...] = jnp.full(mx_ref.shape, NEG_BIG, F32)
    acc_ref[...] = jnp.zeros(acc_ref.shape, F32)

    def score_block(j, masked):
        k = k_ref[0, pl.ds(pl.multiple_of(j * tk, tk), tk), :]
        maxes = [mx_ref[0], mx_ref[1]]
        for h in range(2):
            s = lax.dot_general(qs[h], k, (((1,), (1,)), ((), ())), preferred_element_type=F32)
            if masked:
                rows = lax.broadcasted_iota(jnp.int32, s.shape, 0)
                cols = lax.broadcasted_iota(jnp.int32, s.shape, 1)
                s = jnp.where(rows >= cols, s, NEG_BIG)
            s_buf[h, j] = s
            for sg in _lane_groups(s):
                maxes[h] = jnp.maximum(maxes[h], sg)
        mx_ref[0], mx_ref[1] = maxes

    def score_body(j, carry):
        score_block(j, False)
        return carry

    lax.fori_loop(0, i, score_body, 0)
    score_block(i, True)
    for h in range(2):
        m_ref[h] = jnp.broadcast_to(jnp.max(mx_ref[h], axis=1, keepdims=True), (tq, LANES))

    def pv_body(j, carry):
        v = v_ref[0, pl.ds(pl.multiple_of(j * tk, tk), tk), :]
        v_ext = jnp.concatenate([v, jnp.ones_like(v)], axis=1)
        accs = [acc_ref[0], acc_ref[1]]
        for h in range(2):
            m = m_ref[h]
            p = jnp.concatenate([jnp.exp2(sg - m) for sg in _lane_groups(s_buf[h, j])], axis=1)
            accs[h] = accs[h] + jnp.dot(p.astype(BF16), v_ext, preferred_element_type=F32)
        acc_ref[0], acc_ref[1] = accs
        return carry

    lax.fori_loop(0, i + 1, pv_body, 0)

    lam_p = lam_ref[...]
    lam = (jnp.exp(jnp.sum(lam_p[0:1] * lam_p[1:2], axis=1, keepdims=True))
           - jnp.exp(jnp.sum(lam_p[2:3] * lam_p[3:4], axis=1, keepdims=True)) + lambda_init)
    a0, a1 = acc_ref[0], acc_ref[1]
    o = a0[:, :LANES] / a0[:, LANES:] - lam * (a1[:, :LANES] / a1[:, LANES:])
    ms = jnp.mean(o * o, axis=-1, keepdims=True)
    y = o * lax.rsqrt(ms + NORM_EPS) * subg_ref[...] * (1.0 - lambda_init)
    o_ref[0] = y.astype(o_ref.dtype)


def _diff_attn(qa, ka, va, lam_p, subg, lambda_init, tq=512):
    b, s, w = qa.shape
    nh = w // LANES
    kern = functools.partial(_da_kernel, tq=tq, tk=tq, lambda_init=lambda_init)
    qspec = pl.BlockSpec((1, tq, LANES), lambda bb, h, i: (bb, i, h))
    kvspec = pl.BlockSpec((1, s, LANES), lambda bb, h, i: (bb, 0, h))
    return pl.pallas_call(
        kern,
        out_shape=jax.ShapeDtypeStruct((b, s, w), BF16),
        grid=(b, nh, s // tq),
        in_specs=[qspec, kvspec, kvspec,
                  pl.BlockSpec(lam_p.shape, lambda bb, h, i: (0, 0)),
                  pl.BlockSpec(subg.shape, lambda bb, h, i: (0, 0))],
        out_specs=qspec,
        scratch_shapes=[pltpu.VMEM((2, s // tq, tq, tq), F32),
                        pltpu.VMEM((2, tq, LANES), F32),
                        pltpu.VMEM((2, tq, LANES), F32),
                        pltpu.VMEM((2, tq, 2 * LANES), F32)],
        compiler_params=_params(("arbitrary",) * 3),
        name="diff_attn",
    )(qa, ka, va, lam_p, subg)


def _sb_kernel(q_ref, k_ref, v_ref, tri_ref, o_ref, c_ref, acc_ref, *, tq, tk):
    i = pl.program_id(2)
    q = q_ref[0]
    lane = lax.broadcasted_iota(jnp.int32, q.shape, 1)
    zero = jnp.zeros_like(q)
    qs = (jnp.where(lane < HEAD_DIM, q, zero), jnp.where(lane >= HEAD_DIM, q, zero))
    c_ref[...] = jnp.zeros(c_ref.shape, F32)
    acc_ref[...] = jnp.zeros(acc_ref.shape, F32)
    tri = tri_ref[...]

    def block(j, masked):
        start = pl.multiple_of(j * tk, tk)
        k = k_ref[0, pl.ds(start, tk), :]
        v = v_ref[0, pl.ds(start, tk), :]
        state = [(c_ref[h], acc_ref[h]) for h in range(2)]
        new_state = []
        for h in range(2):
            c_prev, acc_prev = state[h]
            z = lax.dot_general(qs[h], k, (((1,), (1,)), ((), ())), preferred_element_type=F32)
            nz = -z
            w = jnp.log2(1.0 + jnp.exp2(jnp.minimum(z, nz)))
            log_1m = jnp.minimum(nz, 0.0) - w
            log_b = jnp.minimum(z, 0.0) - w
            if masked:
                qpos = i * tq + lax.broadcasted_iota(jnp.int32, z.shape, 0)
                kpos = start + lax.broadcasted_iota(jnp.int32, z.shape, 1)
                mask = kpos < qpos
                log_1m = jnp.where(mask, log_1m, 0.0)
            r = jnp.dot(log_1m.astype(BF16), tri, preferred_element_type=F32)
            suffix, total = r[:, :tk], r[:, tk:]
            a = jnp.concatenate([jnp.exp2(lb + sf + c_prev)
                                 for lb, sf in zip(_lane_groups(log_b), _lane_groups(suffix))], axis=1)
            if masked:
                a = jnp.where(mask, a, 0.0)
            new_state.append((c_prev + total,
                              acc_prev + jnp.dot(a.astype(BF16), v, preferred_element_type=F32)))
        for h in range(2):
            c_ref[h], acc_ref[h] = new_state[h]

    ratio = tq // tk
    last = (i + 1) * ratio - 1
    for r in range(ratio):
        block(last - r, True)

    def body(t, carry):
        block(i * ratio - 1 - t, False)
        return carry

    lax.fori_loop(0, i * ratio, body, 0)
    o_ref[0] = jnp.where(lane < HEAD_DIM, acc_ref[0], acc_ref[1]).astype(o_ref.dtype)


def _sb_attn(qb, kb, vb, tq=512, tk=256):
    b, s, w = qb.shape
    nh = w // LANES
    r = lax.broadcasted_iota(jnp.int32, (tk, tk), 0)
    c = lax.broadcasted_iota(jnp.int32, (tk, tk), 1)
    tri = jnp.concatenate([(r > c).astype(BF16), jnp.ones((tk, LANES), BF16)], axis=1)
    kern = functools.partial(_sb_kernel, tq=tq, tk=tk)
    qspec = pl.BlockSpec((1, tq, LANES), lambda bb, h, i: (bb, i, h))
    kvspec = pl.BlockSpec((1, s, LANES), lambda bb, h, i: (bb, 0, h))
    return pl.pallas_call(
        kern,
        out_shape=jax.ShapeDtypeStruct((b, s, w), BF16),
        grid=(b, nh, s // tq),
        in_specs=[qspec, kvspec, kvspec, pl.BlockSpec(tri.shape, lambda bb, h, i: (0, 0))],
        out_specs=qspec,
        scratch_shapes=[pltpu.VMEM((2, tq, LANES), F32), pltpu.VMEM((2, tq, LANES), F32)],
        compiler_params=_params(("arbitrary",) * 3),
        name="sb_attn",
    )(qb, kb, vb, tri)


def _post_attn_kernel(oa_ref, ob_ref, ga_ref, gb_ref, x_ref, wda_ref, wsb_ref, wo_ref, g_ref,
                      wr_hi_ref, wr_lo_ref, br_ref, tri_ref,
                      x1_ref, h_ref, idx_ref, gate_ref, rank_ref, cnt_ref, carry_ref):
    i = pl.program_id(0)

    @pl.when(i == 0)
    def _():
        carry_ref[...] = jnp.zeros(carry_ref.shape, F32)

    ya = jnp.dot(oa_ref[...], wda_ref[...], preferred_element_type=F32)
    yb = jnp.dot(ob_ref[...], wsb_ref[...], preferred_element_type=F32)
    mix = ga_ref[...].astype(F32) * ya + gb_ref[...].astype(F32) * yb
    x1 = x_ref[...] + jnp.dot(mix.astype(BF16), wo_ref[...], preferred_element_type=F32)
    x1_ref[...] = x1
    ms = jnp.mean(x1 * x1, axis=-1, keepdims=True)
    h = x1 * lax.rsqrt(ms + NORM_EPS) * g_ref[...]
    h_ref[...] = h

    h_hi = h.astype(BF16)
    h_lo = (h - h_hi.astype(F32)).astype(BF16)
    nt = (((1,), (1,)), ((), ()))
    logits = (lax.dot_general(wr_hi_ref[...], h_hi, nt, preferred_element_type=F32)
              + lax.dot_general(wr_hi_ref[...], h_lo, nt, preferred_element_type=F32)
              + lax.dot_general(wr_lo_ref[...], h_hi, nt, preferred_element_type=F32)
              + br_ref[...])
    ne, tm = logits.shape
    eid = lax.broadcasted_iota(jnp.int32, (ne, tm), 0).astype(F32)
    vals, ids = [], []
    work = logits
    for _ in range(TOP_K):
        mx = jnp.max(work, axis=0, keepdims=True)
        sel = jnp.min(jnp.where(work == mx, eid, float(ne)), axis=0, keepdims=True)
        vals.append(mx)
        ids.append(sel)
        work = jnp.where(eid == sel, -jnp.inf, work)
    exps = [jnp.exp(v - vals[0]) for v in vals]
    denom = exps[0] + exps[1] + exps[2] + exps[3]
    onehots = [(eid == sel).astype(F32) for sel in ids]
    assigned = onehots[0] + onehots[1] + onehots[2] + onehots[3]
    before = jnp.dot(assigned.astype(BF16), tri_ref[...], preferred_element_type=F32) + carry_ref[...]
    for r in range(TOP_K):
        idx_ref[r:r + 1, :] = ids[r].astype(jnp.int32)
        gate_ref[r:r + 1, :] = exps[r] / denom
        rank_ref[r:r + 1, :] = jnp.sum(onehots[r] * before, axis=0, keepdims=True).astype(jnp.int32)
    carry = carry_ref[...] + jnp.sum(assigned, axis=1, keepdims=True)
    carry_ref[...] = carry
    cnt_ref[...] = jnp.broadcast_to(carry, cnt_ref.shape)


def _post_attn(oa, ob, ga, gb, x2, wda, wsb, wo, g, wr_hi, wr_lo, br, tm=512):
    t, d = x2.shape
    w = oa.shape[1]
    r = lax.broadcasted_iota(jnp.int32, (tm, tm), 0)
    c = lax.broadcasted_iota(jnp.int32, (tm, tm), 1)
    tri = (r < c).astype(BF16)
    row = lambda i: (i, 0)
    col = lambda i: (0, i)
    const = lambda i: (0, 0)
    full = lambda a: pl.BlockSpec(a.shape, const)
    return pl.pallas_call(
        _post_attn_kernel,
        out_shape=[jax.ShapeDtypeStruct((t, d), F32), jax.ShapeDtypeStruct((t, d), F32),
                   jax.ShapeDtypeStruct((TOP_K, t), jnp.int32), jax.ShapeDtypeStruct((TOP_K, t), F32),
                   jax.ShapeDtypeStruct((TOP_K, t), jnp.int32),
                   jax.ShapeDtypeStruct((N_EXPERTS, LANES), F32)],
        grid=(t // tm,),
        in_specs=[pl.BlockSpec((tm, w), row), pl.BlockSpec((tm, w), row),
                  pl.BlockSpec((tm, d), row), pl.BlockSpec((tm, d), row), pl.BlockSpec((tm, d), row),
                  full(wda), full(wsb), full(wo), full(g), full(wr_hi), full(wr_lo), full(br), full(tri)],
        out_specs=[pl.BlockSpec((tm, d), row), pl.BlockSpec((tm, d), row),
                   pl.BlockSpec((TOP_K, tm), col), pl.BlockSpec((TOP_K, tm), col),
                   pl.BlockSpec((TOP_K, tm), col), pl.BlockSpec((N_EXPERTS, LANES), const)],
        scratch_shapes=[pltpu.VMEM((N_EXPERTS, 1), F32)],
        compiler_params=_params(("arbitrary",)),
        name="post_attn",
    )(oa, ob, ga, gb, x2, wda, wsb, wo, g, wr_hi, wr_lo, br, tri)


def _dispatch_kernel(dest_ref, h_ref, xs_in_ref, xs_ref, sem, *, tm, t_total):
    del xs_in_ref
    i = pl.program_id(0)

    def row_copy(t, k):
        dst = dest_ref[k * t_total + i * tm + t]
        return pltpu.make_async_copy(h_ref.at[pl.ds(t, 1), :], xs_ref.at[pl.ds(dst, 1), :], sem)

    def issue(t, carry):
        for k in range(TOP_K):
            row_copy(t, k).start()
        return carry

    lax.fori_loop(0, tm, issue, 0, unroll=8)
    for _ in range(TOP_K):
        pltpu.make_async_copy(h_ref, xs_ref.at[pl.ds(0, tm), :], sem).wait()


def _dispatch(dest_flat, h, xs_zero, tm=256):
    t, d = h.shape
    kern = functools.partial(_dispatch_kernel, tm=tm, t_total=t)
    return pl.pallas_call(
        kern,
        out_shape=jax.ShapeDtypeStruct(xs_zero.shape, xs_zero.dtype),
        grid_spec=pltpu.PrefetchScalarGridSpec(
            num_scalar_prefetch=1,
            grid=(t // tm,),
            in_specs=[pl.BlockSpec((tm, d), lambda i, dest: (i, 0)),
                      pl.BlockSpec(memory_space=pl.ANY)],
            out_specs=pl.BlockSpec(memory_space=pl.ANY),
            scratch_shapes=[pltpu.SemaphoreType.DMA],
        ),
        input_output_aliases={2: 0},
        compiler_params=_params(("arbitrary",)),
        name="dispatch",
    )(dest_flat, h, xs_zero)


def _expert_kernel(blk_e_ref, n_used_ref, xs_ref, wg_ref, bg_ref, wu_ref, bu_ref, wd_ref, bd_ref,
                   o_ref, wg_bf, wu_bf, wd_bf):
    i = pl.program_id(0)
    e = blk_e_ref[i]
    prev = blk_e_ref[jnp.maximum(i - 1, 0)]
    used = i < n_used_ref[0]

    @pl.when(used & ((i == 0) | (e != prev)))
    def _():
        wg_bf[...] = wg_ref[0].astype(BF16)
        wu_bf[...] = wu_ref[0].astype(BF16)
        wd_bf[...] = wd_ref[0].astype(BF16)

    @pl.when(used)
    def _():
        x = xs_ref[...].astype(BF16)
        g = jnp.dot(x, wg_bf[...], preferred_element_type=F32) + bg_ref[0]
        u = jnp.dot(x, wu_bf[...], preferred_element_type=F32) + bu_ref[0]
        g = jnp.minimum(g, SWIGLU_LIMIT)
        u = jnp.clip(u, -SWIGLU_LIMIT, SWIGLU_LIMIT)
        glu = g * jax.nn.sigmoid(SWIGLU_ALPHA * g)
        act = ((u + 1.0) * glu).astype(BF16)
        o_ref[...] = jnp.dot(act, wd_bf[...], preferred_element_type=F32) + bd_ref[0]

    @pl.when(jnp.logical_not(used))
    def _():
        o_ref[...] = jnp.zeros(o_ref.shape, o_ref.dtype)


def _experts(blk_e, n_used, xs, wg, bg, wu, bu, wd, bd):
    p, d = xs.shape
    f = wg.shape[2]
    nb = p // ROW_BLOCK
    wmap = lambda i, be, nu: (be[i], 0, 0)
    xmap = lambda i, be, nu: (jnp.minimum(i, nu[0] - 1), 0)
    return pl.pallas_call(
        _expert_kernel,
        out_shape=jax.ShapeDtypeStruct((p, d), F32),
        grid_spec=pltpu.PrefetchScalarGridSpec(
            num_scalar_prefetch=2,
            grid=(nb,),
            in_specs=[pl.BlockSpec((ROW_BLOCK, d), xmap),
                      pl.BlockSpec((1, d, f), wmap), pl.BlockSpec((1, 1, f), wmap),
                      pl.BlockSpec((1, d, f), wmap), pl.BlockSpec((1, 1, f), wmap),
                      pl.BlockSpec((1, f, d), wmap), pl.BlockSpec((1, 1, d), wmap)],
            out_specs=pl.BlockSpec((ROW_BLOCK, d), lambda i, be, nu: (i, 0)),
            scratch_shapes=[pltpu.VMEM((d, f), BF16), pltpu.VMEM((d, f), BF16), pltpu.VMEM((f, d), BF16)],
        ),
        compiler_params=_params(("arbitrary",)),
        name="experts",
    )(blk_e, n_used, xs, wg, bg, wu, bu, wd, bd)


def _combine_kernel(dest_ref, rows_ref, gate_ref, x1_ref, g_ref, o_ref, buf, sem, *, tm, t_total):
    i = pl.program_id(0)

    def row_copy(t, k):
        src = dest_ref[k * t_total + i * tm + t]
        return pltpu.make_async_copy(rows_ref.at[pl.ds(src, 1), :], buf.at[k, pl.ds(t, 1), :], sem)

    def issue(t, carry):
        for k in range(TOP_K):
            row_copy(t, k).start()
        return carry

    lax.fori_loop(0, tm, issue, 0, unroll=8)
    for k in range(TOP_K):
        pltpu.make_async_copy(rows_ref.at[pl.ds(0, tm), :], buf.at[k], sem).wait()
    gates = gate_ref[...]
    y = x1_ref[...]
    for k in range(TOP_K):
        y = y + gates[:, k:k + 1] * buf[k]
    ms = jnp.mean(y * y, axis=-1, keepdims=True)
    o_ref[...] = y * lax.rsqrt(ms + NORM_EPS) * g_ref[...]


def _combine(dest_flat, rows, gates_t, x1, g, tm=256):
    t, d = x1.shape
    kern = functools.partial(_combine_kernel, tm=tm, t_total=t)
    return pl.pallas_call(
        kern,
        out_shape=jax.ShapeDtypeStruct((t, d), F32),
        grid_spec=pltpu.PrefetchScalarGridSpec(
            num_scalar_prefetch=1,
            grid=(t // tm,),
            in_specs=[pl.BlockSpec(memory_space=pl.ANY),
                      pl.BlockSpec((tm, TOP_K), lambda i, dest: (i, 0)),
                      pl.BlockSpec((tm, d), lambda i, dest: (i, 0)),
                      pl.BlockSpec((1, d), lambda i, dest: (0, 0))],
            out_specs=pl.BlockSpec((tm, d), lambda i, dest: (i, 0)),
            scratch_shapes=[pltpu.VMEM((TOP_K, tm, d), F32), pltpu.SemaphoreType.DMA],
        ),
        compiler_params=_params(("arbitrary",)),
        name="combine",
    )(dest_flat, rows, gates_t, x1, g)


def _rope_tables(seq):
    inv = 1.0 / (ROPE_THETA ** (jnp.arange(0, HEAD_DIM, 2, dtype=F32) / HEAD_DIM))
    ang = jnp.arange(seq, dtype=F32)[:, None] * inv[None, :]
    cos, sin = jnp.cos(ang), jnp.sin(ang)
    return jnp.concatenate([cos] * 4, axis=1), jnp.concatenate([-sin, sin, -sin, sin], axis=1)


def kernel(x, norm_mix_g, w_in, lambda_q1, lambda_k1, lambda_q2, lambda_k2, da_subln_g, w_da_out, w_sb_out, w_o, norm_ffn_g, w_router, b_router, w_gate, b_gate, w_up, b_up, w_down, b_down, norm_final_g):
    b, s, d = x.shape
    depth = w_in.shape[0]
    t = b * s
    cos_t, sin_t = _rope_tables(s)
    n_blocks = (t * TOP_K + ROW_BLOCK - 1) // ROW_BLOCK + N_EXPERTS
    p_rows = n_blocks * ROW_BLOCK
    x2 = x.reshape(t, d)
    for l in range(depth):
        lambda_init = 0.8 - 0.6 * math.exp(-0.3 * l)
        qa, ka, va, qb, kb, vb, ga, gb = _in_proj(
            x2, norm_mix_g[l][None, :], w_in[l].astype(BF16), cos_t, sin_t, s)
        lam_p = jnp.stack([lambda_q1[l], lambda_k1[l], lambda_q2[l], lambda_k2[l]]).astype(F32)
        seq3 = lambda a: a.reshape(b, s, a.shape[1])
        oa = _diff_attn(seq3(qa), seq3(ka), seq3(va), lam_p, da_subln_g[l][None, :].astype(F32), lambda_init)
        ob = _sb_attn(seq3(qb), seq3(kb), seq3(vb))
        wr = w_router[l].T.astype(F32)
        wr_hi = wr.astype(BF16)
        wr_lo = (wr - wr_hi.astype(F32)).astype(BF16)
        x1, h, idx, gates, rank, cnt = _post_attn(
            oa.reshape(t, -1), ob.reshape(t, -1), ga, gb, x2,
            w_da_out[l].astype(BF16), w_sb_out[l].astype(BF16), w_o[l].astype(BF16),
            norm_ffn_g[l][None, :], wr_hi, wr_lo, b_router[l][:, None].astype(F32))
        counts = cnt[:, 0].astype(jnp.int32)
        padded = (counts + ROW_BLOCK - 1) // ROW_BLOCK * ROW_BLOCK
        pad_ends = jnp.cumsum(padded)
        pad_starts = pad_ends - padded
        experts = jnp.arange(N_EXPERTS, dtype=jnp.int32)
        base = jnp.sum(jnp.where(idx[:, :, None] == experts, pad_starts, 0), axis=-1)
        dest = (base + rank).reshape(-1)
        n_used = (pad_ends[-1] // ROW_BLOCK).astype(jnp.int32)
        blk = jnp.minimum(jnp.arange(n_blocks, dtype=jnp.int32), n_used - 1) * ROW_BLOCK
        blk_e = jnp.minimum(jnp.sum(pad_ends[None, :] <= blk[:, None], axis=1), N_EXPERTS - 1).astype(jnp.int32)
        xs = _dispatch(dest, h, jnp.zeros((p_rows, d), F32))
        rows = _experts(blk_e, n_used[None], xs, w_gate[l], b_gate[l][:, None, :], w_up[l], b_up[l][:, None, :],
                        w_down[l], b_down[l][:, None, :])
        g_next = norm_final_g[None, :] if l == depth - 1 else jnp.ones((1, d), F32)
        x2 = _combine(dest, rows, gates.T, x1, g_next)
        if l != depth - 1:
            raise NotImplementedError("only the final layer's norm is fused into the combine kernel")
    return x2.reshape(b, s, d)
```

```python
import functools
import math

import jax
import jax.numpy as jnp
from jax import lax
from jax.experimental import pallas as pl
from jax.experimental.pallas import tpu as pltpu

F32 = jnp.float32
BF16 = jnp.bfloat16

DA_HEADS = 4
HEAD_DIM = 64
N_EXPERTS = 32
TOP_K = 4
ROPE_THETA = 10000.0
SWIGLU_LIMIT = 7.0
SWIGLU_ALPHA = 1.702
NORM_EPS = 1e-5
ROW_BLOCK = 256
LANES = 128
NEG_BIG = -1e30
LOG2E = math.log2(math.e)
SB_DEAD_LOG2 = -150.0

VMEM_LIMIT = 56 * 1024 * 1024


def _params(sem, vmem=VMEM_LIMIT):
    return pltpu.CompilerParams(dimension_semantics=sem, vmem_limit_bytes=vmem)


def _in_proj_kernel(x_ref, g_ref, w_ref, cos_ref, sin_ref,
                    qa_ref, ka_ref, va_ref, qb_ref, kb_ref, vb_ref, ga_ref, gb_ref):
    x = x_ref[...]
    ms = jnp.mean(x * x, axis=-1, keepdims=True)
    h = (x * lax.rsqrt(ms + NORM_EPS) * g_ref[...]).astype(BF16)
    cos = cos_ref[...]
    sin = sin_ref[...]
    lane = lax.broadcasted_iota(jnp.int32, cos.shape, 1)
    first_half = (lane & (HEAD_DIM - 1)) < HEAD_DIM // 2

    def proj(c0, width):
        return jnp.dot(h, w_ref[:, c0:c0 + width], preferred_element_type=F32)

    def rope(r):
        outs = []
        for g in range(r.shape[1] // LANES):
            xg = r[:, g * LANES:(g + 1) * LANES]
            rot = jnp.where(first_half, pltpu.roll(xg, LANES - HEAD_DIM // 2, 1),
                            pltpu.roll(xg, HEAD_DIM // 2, 1))
            outs.append(xg * cos + rot * sin)
        return jnp.concatenate(outs, axis=1)

    scale = HEAD_DIM ** -0.5 * LOG2E
    w = qa_ref.shape[1]
    d = ga_ref.shape[1]
    qa_ref[...] = (rope(proj(0, w)) * scale).astype(BF16)
    ka_ref[...] = rope(proj(w, w)).astype(BF16)
    va_ref[...] = proj(2 * w, w).astype(BF16)
    qb_ref[...] = (proj(3 * w, w) * scale).astype(BF16)
    kb_ref[...] = proj(4 * w, w).astype(BF16)
    vb_ref[...] = proj(5 * w, w).astype(BF16)
    ga_ref[...] = jax.nn.sigmoid(proj(6 * w, d)).astype(BF16)
    gb_ref[...] = jax.nn.sigmoid(proj(6 * w + d, d)).astype(BF16)


def _in_proj(x2, g, w_in_bf, cos_t, sin_t, seq, tm=512):
    t, d = x2.shape
    w = 512
    nseq = seq // tm
    outs = [jax.ShapeDtypeStruct((t, w), BF16)] * 6 + [jax.ShapeDtypeStruct((t, d), BF16)] * 2
    row = lambda i: (i, 0)
    return pl.pallas_call(
        _in_proj_kernel,
        out_shape=outs,
        grid=(t // tm,),
        in_specs=[
            pl.BlockSpec((tm, d), row),
            pl.BlockSpec((1, d), lambda i: (0, 0)),
            pl.BlockSpec(w_in_bf.shape, lambda i: (0, 0)),
            pl.BlockSpec((tm, LANES), lambda i: (i % nseq, 0)),
            pl.BlockSpec((tm, LANES), lambda i: (i % nseq, 0)),
        ],
        out_specs=[pl.BlockSpec((tm, w), row)] * 6 + [pl.BlockSpec((tm, d), row)] * 2,
        compiler_params=_params(("arbitrary",)),
        name="in_proj",
    )(x2, g, w_in_bf, cos_t, sin_t)


def _lane_groups(x):
    return [x[:, g * LANES:(g + 1) * LANES] for g in range(x.shape[1] // LANES)]


def _da_kernel(q_ref, k_ref, v_ref, lam_ref, subg_ref, o_ref, s_buf, mx_ref, m_ref, acc_ref,
               *, tq, tk, lambda_init):
    i = pl.program_id(2)
    q = q_ref[0]
    lane = lax.broadcasted_iota(jnp.int32, q.shape, 1)
    zero = jnp.zeros_like(q)
    qs = (jnp.where(lane < HEAD_DIM, q, zero), jnp.where(lane >= HEAD_DIM, q, zero))
    mx_ref[...] = jnp.full(mx_ref.shape, NEG_BIG, F32)
    acc_ref[...] = jnp.zeros(acc_ref.shape, F32)

    def score_block(j, masked):
        k = k_ref[0, pl.ds(pl.multiple_of(j * tk, tk), tk), :]
        maxes = [mx_ref[0], mx_ref[1]]
        for h in range(2):
            s = lax.dot_general(qs[h], k, (((1,), (1,)), ((), ())), preferred_element_type=F32)
            if masked:
                rows = lax.broadcasted_iota(jnp.int32, s.shape, 0)
                cols = lax.broadcasted_iota(jnp.int32, s.shape, 1)
                s = jnp.where(rows >= cols, s, NEG_BIG)
            s_buf[h, j] = s
            for sg in _lane_groups(s):
                maxes[h] = jnp.maximum(maxes[h], sg)
        mx_ref[0], mx_ref[1] = maxes

    def score_body(j, carry):
        score_block(j, False)
        return carry

    lax.fori_loop(0, i, score_body, 0)
    score_block(i, True)
    for h in range(2):
        m_ref[h] = jnp.broadcast_to(jnp.max(mx_ref[h], axis=1, keepdims=True), (tq, LANES))

    def pv_body(j, carry):
        v = v_ref[0, pl.ds(pl.multiple_of(j * tk, tk), tk), :]
        v_ext = jnp.concatenate([v, jnp.ones_like(v)], axis=1)
        accs = [acc_ref[0], acc_ref[1]]
        for h in range(2):
            m = m_ref[h]
            p = jnp.concatenate([jnp.exp2(sg - m) for sg in _lane_groups(s_buf[h, j])], axis=1)
            accs[h] = accs[h] + jnp.dot(p.astype(BF16), v_ext, preferred_element_type=F32)
        acc_ref[0], acc_ref[1] = accs
        return carry

    lax.fori_loop(0, i + 1, pv_body, 0)

    lam_p = lam_ref[...]
    lam = (jnp.exp(jnp.sum(lam_p[0:1] * lam_p[1:2], axis=1, keepdims=True))
           - jnp.exp(jnp.sum(lam_p[2:3] * lam_p[3:4], axis=1, keepdims=True)) + lambda_init)
    a0, a1 = acc_ref[0], acc_ref[1]
    o = a0[:, :LANES] / a0[:, LANES:] - lam * (a1[:, :LANES] / a1[:, LANES:])
    ms = jnp.mean(o * o, axis=-1, keepdims=True)
    y = o * lax.rsqrt(ms + NORM_EPS) * subg_ref[...] * (1.0 - lambda_init)
    o_ref[0] = y.astype(o_ref.dtype)


def _diff_attn(qa, ka, va, lam_p, subg, lambda_init, tq=512):
    b, s, w = qa.shape
    nh = w // LANES
    kern = functools.partial(_da_kernel, tq=tq, tk=tq, lambda_init=lambda_init)
    qspec = pl.BlockSpec((1, tq, LANES), lambda bb, h, i: (bb, i, h))
    kvspec = pl.BlockSpec((1, s, LANES), lambda bb, h, i: (bb, 0, h))
    return pl.pallas_call(
        kern,
        out_shape=jax.ShapeDtypeStruct((b, s, w), BF16),
        grid=(b, nh, s // tq),
        in_specs=[qspec, kvspec, kvspec,
                  pl.BlockSpec(lam_p.shape, lambda bb, h, i: (0, 0)),
                  pl.BlockSpec(subg.shape, lambda bb, h, i: (0, 0))],
        out_specs=qspec,
        scratch_shapes=[pltpu.VMEM((2, s // tq, tq, tq), F32),
                        pltpu.VMEM((2, tq, LANES), F32),
                        pltpu.VMEM((2, tq, LANES), F32),
                        pltpu.VMEM((2, tq, 2 * LANES), F32)],
        compiler_params=_params(("arbitrary",) * 3),
        name="diff_attn",
    )(qa, ka, va, lam_p, subg)


def _sb_kernel(q_ref, k_ref, v_ref, tri_ref, o_ref, c_ref, acc_ref, *, tq, tk):
    i = pl.program_id(2)
    tri = tri_ref[...]
    lane = lax.broadcasted_iota(jnp.int32, (tk, LANES), 1)

    def row_tile(r):
        q = q_ref[0, r * tk:(r + 1) * tk, :]
        zero = jnp.zeros_like(q)
        qs = (jnp.where(lane < HEAD_DIM, q, zero), jnp.where(lane >= HEAD_DIM, q, zero))
        c_ref[...] = jnp.zeros(c_ref.shape, F32)
        acc_ref[...] = jnp.zeros(acc_ref.shape, F32)

        def window(js, diag_first):
            ks = [k_ref[0, pl.ds(pl.multiple_of(j * tk, tk), tk), :] for j in js]
            vs = [v_ref[0, pl.ds(pl.multiple_of(j * tk, tk), tk), :] for j in js]
            v_all = jnp.concatenate(vs, axis=0)
            state = [(c_ref[h], acc_ref[h]) for h in range(2)]
            new_state = []
            for h in range(2):
                c_run, acc_prev = state[h]
                weights = []
                for n, k in enumerate(ks):
                    masked = diag_first and n == 0
                    z = lax.dot_general(qs[h], k, (((1,), (1,)), ((), ())), preferred_element_type=F32)
                    nz = -z
                    w = jnp.log2(1.0 + jnp.exp2(jnp.minimum(z, nz)))
                    log_1m = jnp.minimum(nz, 0.0) - w
                    log_b = jnp.minimum(z, 0.0) - w
                    if masked:
                        mask = (lax.broadcasted_iota(jnp.int32, z.shape, 1)
                                < lax.broadcasted_iota(jnp.int32, z.shape, 0))
                        log_1m = jnp.where(mask, log_1m, 0.0)
                    res = jnp.dot(log_1m.astype(BF16), tri, preferred_element_type=F32)
                    suffix, total = res[:, :tk], res[:, tk:]
                    a = jnp.concatenate([jnp.exp2(lb + sf + c_run)
                                         for lb, sf in zip(_lane_groups(log_b), _lane_groups(suffix))], axis=1)
                    if masked:
                        a = jnp.where(mask, a, 0.0)
                    weights.append(a.astype(BF16))
                    c_run = c_run + total
                a_all = jnp.concatenate(weights, axis=1)
                new_state.append((c_run, acc_prev + jnp.dot(a_all, v_all, preferred_element_type=F32)))
            for h in range(2):
                c_ref[h], acc_ref[h] = new_state[h]

        jd = i * (tq // tk) + r
        if r % 2 == 0:
            window([jd], True)
            top = jd - 1
        else:
            window([jd, jd - 1], True)
            top = jd - 2

        def cond(carry):
            j, live = carry
            return jnp.logical_and(j >= 1, live)

        def body(carry):
            j, _ = carry
            window([j, j - 1], False)
            return j - 2, jnp.max(jnp.maximum(c_ref[0], c_ref[1])) > SB_DEAD_LOG2

        lax.while_loop(cond, body, (top, True))
        o_ref[0, r * tk:(r + 1) * tk, :] = jnp.where(lane < HEAD_DIM, acc_ref[0], acc_ref[1]).astype(o_ref.dtype)

    for r in range(tq // tk):
        row_tile(r)


def _sb_attn(qb, kb, vb, tq=512, tk=256):
    b, s, w = qb.shape
    nh = w // LANES
    r = lax.broadcasted_iota(jnp.int32, (tk, tk), 0)
    c = lax.broadcasted_iota(jnp.int32, (tk, tk), 1)
    tri = jnp.concatenate([(r > c).astype(BF16), jnp.ones((tk, LANES), BF16)], axis=1)
    kern = functools.partial(_sb_kernel, tq=tq, tk=tk)
    qspec = pl.BlockSpec((1, tq, LANES), lambda bb, h, i: (bb, i, h))
    kvspec = pl.BlockSpec((1, s, LANES), lambda bb, h, i: (bb, 0, h))
    return pl.pallas_call(
        kern,
        out_shape=jax.ShapeDtypeStruct((b, s, w), BF16),
        grid=(b, nh, s // tq),
        in_specs=[qspec, kvspec, kvspec, pl.BlockSpec(tri.shape, lambda bb, h, i: (0, 0))],
        out_specs=qspec,
        scratch_shapes=[pltpu.VMEM((2, tk, LANES), F32), pltpu.VMEM((2, tk, LANES), F32)],
        compiler_params=_params(("arbitrary",) * 3),
        name="sb_attn",
    )(qb, kb, vb, tri)


def _post_attn_kernel(oa_ref, ob_ref, ga_ref, gb_ref, x_ref, wda_ref, wsb_ref, wo_ref, g_ref,
                      wr_hi_ref, wr_lo_ref, br_ref, tri_ref,
                      x1_ref, h_ref, idx_ref, gate_ref, rank_ref, cnt_ref, carry_ref):
    i = pl.program_id(0)

    @pl.when(i == 0)
    def _():
        carry_ref[...] = jnp.zeros(carry_ref.shape, F32)

    ya = jnp.dot(oa_ref[...], wda_ref[...], preferred_element_type=F32)
    yb = jnp.dot(ob_ref[...], wsb_ref[...], preferred_element_type=F32)
    mix = ga_ref[...].astype(F32) * ya + gb_ref[...].astype(F32) * yb
    x1 = x_ref[...] + jnp.dot(mix.astype(BF16), wo_ref[...], preferred_element_type=F32)
    x1_ref[...] = x1
    ms = jnp.mean(x1 * x1, axis=-1, keepdims=True)
    h = x1 * lax.rsqrt(ms + NORM_EPS) * g_ref[...]
    h_ref[...] = h

    h_hi = h.astype(BF16)
    h_lo = (h - h_hi.astype(F32)).astype(BF16)
    nt = (((1,), (1,)), ((), ()))
    logits = (lax.dot_general(wr_hi_ref[...], h_hi, nt, preferred_element_type=F32)
              + lax.dot_general(wr_hi_ref[...], h_lo, nt, preferred_element_type=F32)
              + lax.dot_general(wr_lo_ref[...], h_hi, nt, preferred_element_type=F32)
              + br_ref[...])
    ne, tm = logits.shape
    eid = lax.broadcasted_iota(jnp.int32, (ne, tm), 0).astype(F32)
    vals, ids = [], []
    work = logits
    for _ in range(TOP_K):
        mx = jnp.max(work, axis=0, keepdims=True)
        sel = jnp.min(jnp.where(work == mx, eid, float(ne)), axis=0, keepdims=True)
        vals.append(mx)
        ids.append(sel)
        work = jnp.where(eid == sel, -jnp.inf, work)
    exps = [jnp.exp(v - vals[0]) for v in vals]
    denom = exps[0] + exps[1] + exps[2] + exps[3]
    onehots = [(eid == sel).astype(F32) for sel in ids]
    assigned = onehots[0] + onehots[1] + onehots[2] + onehots[3]
    before = jnp.dot(assigned.astype(BF16), tri_ref[...], preferred_element_type=F32) + carry_ref[...]
    for r in range(TOP_K):
        idx_ref[r:r + 1, :] = ids[r].astype(jnp.int32)
        gate_ref[r:r + 1, :] = exps[r] / denom
        rank_ref[r:r + 1, :] = jnp.sum(onehots[r] * before, axis=0, keepdims=True).astype(jnp.int32)
    carry = carry_ref[...] + jnp.sum(assigned, axis=1, keepdims=True)
    carry_ref[...] = carry
    cnt_ref[...] = jnp.broadcast_to(carry, cnt_ref.shape)


def _post_attn(oa, ob, ga, gb, x2, wda, wsb, wo, g, wr_hi, wr_lo, br, tm=512):
    t, d = x2.shape
    w = oa.shape[1]
    r = lax.broadcasted_iota(jnp.int32, (tm, tm), 0)
    c = lax.broadcasted_iota(jnp.int32, (tm, tm), 1)
    tri = (r < c).astype(BF16)
    row = lambda i: (i, 0)
    col = lambda i: (0, i)
    const = lambda i: (0, 0)
    full = lambda a: pl.BlockSpec(a.shape, const)
    return pl.pallas_call(
        _post_attn_kernel,
        out_shape=[jax.ShapeDtypeStruct((t, d), F32), jax.ShapeDtypeStruct((t, d), F32),
                   jax.ShapeDtypeStruct((TOP_K, t), jnp.int32), jax.ShapeDtypeStruct((TOP_K, t), F32),
                   jax.ShapeDtypeStruct((TOP_K, t), jnp.int32),
                   jax.ShapeDtypeStruct((N_EXPERTS, LANES), F32)],
        grid=(t // tm,),
        in_specs=[pl.BlockSpec((tm, w), row), pl.BlockSpec((tm, w), row),
                  pl.BlockSpec((tm, d), row), pl.BlockSpec((tm, d), row), pl.BlockSpec((tm, d), row),
                  full(wda), full(wsb), full(wo), full(g), full(wr_hi), full(wr_lo), full(br), full(tri)],
        out_specs=[pl.BlockSpec((tm, d), row), pl.BlockSpec((tm, d), row),
                   pl.BlockSpec((TOP_K, tm), col), pl.BlockSpec((TOP_K, tm), col),
                   pl.BlockSpec((TOP_K, tm), col), pl.BlockSpec((N_EXPERTS, LANES), const)],
        scratch_shapes=[pltpu.VMEM((N_EXPERTS, 1), F32)],
        compiler_params=_params(("arbitrary",)),
        name="post_attn",
    )(oa, ob, ga, gb, x2, wda, wsb, wo, g, wr_hi, wr_lo, br, tri)


def _dispatch_kernel(dest_ref, h_ref, xs_in_ref, xs_ref, sem, *, tm, t_total):
    del xs_in_ref
    i = pl.program_id(0)

    def row_copy(t, k):
        dst = dest_ref[k * t_total + i * tm + t]
        return pltpu.make_async_copy(h_ref.at[pl.ds(t, 1), :], xs_ref.at[pl.ds(dst, 1), :], sem)

    def issue(t, carry):
        for k in range(TOP_K):
            row_copy(t, k).start()
        return carry

    lax.fori_loop(0, tm, issue, 0, unroll=8)
    for _ in range(TOP_K):
        pltpu.make_async_copy(h_ref, xs_ref.at[pl.ds(0, tm), :], sem).wait()


def _dispatch(dest_flat, h, xs_zero, tm=256):
    t, d = h.shape
    kern = functools.partial(_dispatch_kernel, tm=tm, t_total=t)
    return pl.pallas_call(
        kern,
        out_shape=jax.ShapeDtypeStruct(xs_zero.shape, xs_zero.dtype),
        grid_spec=pltpu.PrefetchScalarGridSpec(
            num_scalar_prefetch=1,
            grid=(t // tm,),
            in_specs=[pl.BlockSpec((tm, d), lambda i, dest: (i, 0)),
                      pl.BlockSpec(memory_space=pl.ANY)],
            out_specs=pl.BlockSpec(memory_space=pl.ANY),
            scratch_shapes=[pltpu.SemaphoreType.DMA],
        ),
        input_output_aliases={2: 0},
        compiler_params=_params(("arbitrary",)),
        name="dispatch",
    )(dest_flat, h, xs_zero)


def _expert_kernel(blk_e_ref, n_used_ref, xs_ref, wg_ref, bg_ref, wu_ref, bu_ref, wd_ref, bd_ref,
                   o_ref, wg_bf, wu_bf, wd_bf):
    i = pl.program_id(0)
    e = blk_e_ref[i]
    prev = blk_e_ref[jnp.maximum(i - 1, 0)]
    used = i < n_used_ref[0]

    @pl.when(used & ((i == 0) | (e != prev)))
    def _():
        wg_bf[...] = wg_ref[0].astype(BF16)
        wu_bf[...] = wu_ref[0].astype(BF16)
        wd_bf[...] = wd_ref[0].astype(BF16)

    @pl.when(used)
    def _():
        x = xs_ref[...].astype(BF16)
        g = jnp.dot(x, wg_bf[...], preferred_element_type=F32) + bg_ref[0]
        u = jnp.dot(x, wu_bf[...], preferred_element_type=F32) + bu_ref[0]
        g = jnp.minimum(g, SWIGLU_LIMIT)
        u = jnp.clip(u, -SWIGLU_LIMIT, SWIGLU_LIMIT)
        glu = g * jax.nn.sigmoid(SWIGLU_ALPHA * g)
        act = ((u + 1.0) * glu).astype(BF16)
        o_ref[...] = jnp.dot(act, wd_bf[...], preferred_element_type=F32) + bd_ref[0]

    @pl.when(jnp.logical_not(used))
    def _():
        o_ref[...] = jnp.zeros(o_ref.shape, o_ref.dtype)


def _experts(blk_e, n_used, xs, wg, bg, wu, bu, wd, bd):
    p, d = xs.shape
    f = wg.shape[2]
    nb = p // ROW_BLOCK
    wmap = lambda i, be, nu: (be[i], 0, 0)
    xmap = lambda i, be, nu: (jnp.minimum(i, nu[0] - 1), 0)
    return pl.pallas_call(
        _expert_kernel,
        out_shape=jax.ShapeDtypeStruct((p, d), F32),
        grid_spec=pltpu.PrefetchScalarGridSpec(
            num_scalar_prefetch=2,
            grid=(nb,),
            in_specs=[pl.BlockSpec((ROW_BLOCK, d), xmap),
                      pl.BlockSpec((1, d, f), wmap), pl.BlockSpec((1, 1, f), wmap),
                      pl.BlockSpec((1, d, f), wmap), pl.BlockSpec((1, 1, f), wmap),
                      pl.BlockSpec((1, f, d), wmap), pl.BlockSpec((1, 1, d), wmap)],
            out_specs=pl.BlockSpec((ROW_BLOCK, d), lambda i, be, nu: (i, 0)),
            scratch_shapes=[pltpu.VMEM((d, f), BF16), pltpu.VMEM((d, f), BF16), pltpu.VMEM((f, d), BF16)],
        ),
        compiler_params=_params(("arbitrary",)),
        name="experts",
    )(blk_e, n_used, xs, wg, bg, wu, bu, wd, bd)


def _combine_kernel(dest_ref, rows_ref, gate_ref, x1_ref, g_ref, o_ref, buf, sem, *, tm, t_total):
    i = pl.program_id(0)

    def row_copy(t, k):
        src = dest_ref[k * t_total + i * tm + t]
        return pltpu.make_async_copy(rows_ref.at[pl.ds(src, 1), :], buf.at[k, pl.ds(t, 1), :], sem)

    def issue(t, carry):
        for k in range(TOP_K):
            row_copy(t, k).start()
        return carry

    lax.fori_loop(0, tm, issue, 0, unroll=8)
    for k in range(TOP_K):
        pltpu.make_async_copy(rows_ref.at[pl.ds(0, tm), :], buf.at[k], sem).wait()
    gates = gate_ref[...]
    y = x1_ref[...]
    for k in range(TOP_K):
        y = y + gates[:, k:k + 1] * buf[k]
    ms = jnp.mean(y * y, axis=-1, keepdims=True)
    o_ref[...] = y * lax.rsqrt(ms + NORM_EPS) * g_ref[...]


def _combine(dest_flat, rows, gates_t, x1, g, tm=256):
    t, d = x1.shape
    kern = functools.partial(_combine_kernel, tm=tm, t_total=t)
    return pl.pallas_call(
        kern,
        out_shape=jax.ShapeDtypeStruct((t, d), F32),
        grid_spec=pltpu.PrefetchScalarGridSpec(
            num_scalar_prefetch=1,
            grid=(t // tm,),
            in_specs=[pl.BlockSpec(memory_space=pl.ANY),
                      pl.BlockSpec((tm, TOP_K), lambda i, dest: (i, 0)),
                      pl.BlockSpec((tm, d), lambda i, dest: (i, 0)),
                      pl.BlockSpec((1, d), lambda i, dest: (0, 0))],
            out_specs=pl.BlockSpec((tm, d), lambda i, dest: (i, 0)),
            scratch_shapes=[pltpu.VMEM((TOP_K, tm, d), F32), pltpu.SemaphoreType.DMA],
        ),
        compiler_params=_params(("arbitrary",)),
        name="combine",
    )(dest_flat, rows, gates_t, x1, g)


def _rope_tables(seq):
    inv = 1.0 / (ROPE_THETA ** (jnp.arange(0, HEAD_DIM, 2, dtype=F32) / HEAD_DIM))
    ang = jnp.arange(seq, dtype=F32)[:, None] * inv[None, :]
    cos, sin = jnp.cos(ang), jnp.sin(ang)
    return jnp.concatenate([cos] * 4, axis=1), jnp.concatenate([-sin, sin, -sin, sin], axis=1)


def kernel(x, norm_mix_g, w_in, lambda_q1, lambda_k1, lambda_q2, lambda_k2, da_subln_g, w_da_out, w_sb_out, w_o, norm_ffn_g, w_router, b_router, w_gate, b_gate, w_up, b_up, w_down, b_down, norm_final_g):
    b, s, d = x.shape
    depth = w_in.shape[0]
    t = b * s
    cos_t, sin_t = _rope_tables(s)
    n_blocks = (t * TOP_K + ROW_BLOCK - 1) // ROW_BLOCK + N_EXPERTS
    p_rows = n_blocks * ROW_BLOCK
    x2 = x.reshape(t, d)
    for l in range(depth):
        lambda_init = 0.8 - 0.6 * math.exp(-0.3 * l)
        qa, ka, va, qb, kb, vb, ga, gb = _in_proj(
            x2, norm_mix_g[l][None, :], w_in[l].astype(BF16), cos_t, sin_t, s)
        lam_p = jnp.stack([lambda_q1[l], lambda_k1[l], lambda_q2[l], lambda_k2[l]]).astype(F32)
        seq3 = lambda a: a.reshape(b, s, a.shape[1])
        oa = _diff_attn(seq3(qa), seq3(ka), seq3(va), lam_p, da_subln_g[l][None, :].astype(F32), lambda_init)
        ob = _sb_attn(seq3(qb), seq3(kb), seq3(vb))
        wr = w_router[l].T.astype(F32)
        wr_hi = wr.astype(BF16)
        wr_lo = (wr - wr_hi.astype(F32)).astype(BF16)
        x1, h, idx, gates, rank, cnt = _post_attn(
            oa.reshape(t, -1), ob.reshape(t, -1), ga, gb, x2,
            w_da_out[l].astype(BF16), w_sb_out[l].astype(BF16), w_o[l].astype(BF16),
            norm_ffn_g[l][None, :], wr_hi, wr_lo, b_router[l][:, None].astype(F32))
        counts = cnt[:, 0].astype(jnp.int32)
        padded = (counts + ROW_BLOCK - 1) // ROW_BLOCK * ROW_BLOCK
        pad_ends = jnp.cumsum(padded)
        pad_starts = pad_ends - padded
        experts = jnp.arange(N_EXPERTS, dtype=jnp.int32)
        base = jnp.sum(jnp.where(idx[:, :, None] == experts, pad_starts, 0), axis=-1)
        dest = (base + rank).reshape(-1)
        n_used = (pad_ends[-1] // ROW_BLOCK).astype(jnp.int32)
        blk = jnp.minimum(jnp.arange(n_blocks, dtype=jnp.int32), n_used - 1) * ROW_BLOCK
        blk_e = jnp.minimum(jnp.sum(pad_ends[None, :] <= blk[:, None], axis=1), N_EXPERTS - 1).astype(jnp.int32)
        xs = _dispatch(dest, h, jnp.zeros((p_rows, d), F32))
        rows = _experts(blk_e, n_used[None], xs, w_gate[l], b_gate[l][:, None, :], w_up[l], b_up[l][:, None, :],
                        w_down[l], b_down[l][:, None, :])
        g_next = norm_final_g[None, :] if l == depth - 1 else jnp.ones((1, d), F32)
        x2 = _combine(dest, rows, gates.T, x1, g_next)
        if l != depth - 1:
            raise NotImplementedError("only the final layer's norm is fused into the combine kernel")
    return x2.reshape(b, s, d)
```

```python
import functools
import math

import jax
import jax.numpy as jnp
from jax import lax
from jax.experimental import pallas as pl
from jax.experimental.pallas import tpu as pltpu

F32 = jnp.float32
BF16 = jnp.bfloat16

DA_HEADS = 4
HEAD_DIM = 64
N_EXPERTS = 32
TOP_K = 4
ROPE_THETA = 10000.0
SWIGLU_LIMIT = 7.0
SWIGLU_ALPHA = 1.702
NORM_EPS = 1e-5
ROW_BLOCK = 256
LANES = 128
NEG_BIG = -1e30
LOG2E = math.log2(math.e)
SB_DEAD_LOG2 = -150.0

VMEM_LIMIT = 56 * 1024 * 1024


def _params(sem, vmem=VMEM_LIMIT):
    return pltpu.CompilerParams(dimension_semantics=sem, vmem_limit_bytes=vmem)


def _in_proj_kernel(x_ref, g_ref, w_ref, cos_ref, sin_ref,
                    qa_ref, ka_ref, va_ref, qb_ref, kb_ref, vb_ref, ga_ref, gb_ref):
    x = x_ref[...]
    ms = jnp.mean(x * x, axis=-1, keepdims=True)
    h = (x * lax.rsqrt(ms + NORM_EPS) * g_ref[...]).astype(BF16)
    cos = cos_ref[...]
    sin = sin_ref[...]
    lane = lax.broadcasted_iota(jnp.int32, cos.shape, 1)
    first_half = (lane & (HEAD_DIM - 1)) < HEAD_DIM // 2

    def proj(c0, width):
        return jnp.dot(h, w_ref[:, c0:c0 + width], preferred_element_type=F32)

    def rope(r):
        outs = []
        for g in range(r.shape[1] // LANES):
            xg = r[:, g * LANES:(g + 1) * LANES]
            rot = jnp.where(first_half, pltpu.roll(xg, LANES - HEAD_DIM // 2, 1),
                            pltpu.roll(xg, HEAD_DIM // 2, 1))
            outs.append(xg * cos + rot * sin)
        return jnp.concatenate(outs, axis=1)

    scale = HEAD_DIM ** -0.5 * LOG2E
    w = qa_ref.shape[1]
    d = ga_ref.shape[1]
    qa_ref[...] = (rope(proj(0, w)) * scale).astype(BF16)
    ka_ref[...] = rope(proj(w, w)).astype(BF16)
    va_ref[...] = proj(2 * w, w).astype(BF16)
    qb_ref[...] = (proj(3 * w, w) * scale).astype(BF16)
    kb_ref[...] = proj(4 * w, w).astype(BF16)
    vb_ref[...] = proj(5 * w, w).astype(BF16)
    ga_ref[...] = jax.nn.sigmoid(proj(6 * w, d)).astype(BF16)
    gb_ref[...] = jax.nn.sigmoid(proj(6 * w + d, d)).astype(BF16)


def _in_proj(x2, g, w_in_bf, cos_t, sin_t, seq, tm=512):
    t, d = x2.shape
    w = 512
    nseq = seq // tm
    outs = [jax.ShapeDtypeStruct((t, w), BF16)] * 6 + [jax.ShapeDtypeStruct((t, d), BF16)] * 2
    row = lambda i: (i, 0)
    return pl.pallas_call(
        _in_proj_kernel,
        out_shape=outs,
        grid=(t // tm,),
        in_specs=[
            pl.BlockSpec((tm, d), row),
            pl.BlockSpec((1, d), lambda i: (0, 0)),
            pl.BlockSpec(w_in_bf.shape, lambda i: (0, 0)),
            pl.BlockSpec((tm, LANES), lambda i: (i % nseq, 0)),
            pl.BlockSpec((tm, LANES), lambda i: (i % nseq, 0)),
        ],
        out_specs=[pl.BlockSpec((tm, w), row)] * 6 + [pl.BlockSpec((tm, d), row)] * 2,
        compiler_params=_params(("arbitrary",)),
        name="in_proj",
    )(x2, g, w_in_bf, cos_t, sin_t)


def _lane_groups(x):
    return [x[:, g * LANES:(g + 1) * LANES] for g in range(x.shape[1] // LANES)]


def _da_kernel(q_ref, k_ref, v_ref, lam_ref, subg_ref, o_ref, s_buf, mx_ref, m_ref, acc_ref,
               *, tq, tk, lambda_init):
    i = pl.program_id(2)
    q = q_ref[0]
    lane = lax.broadcasted_iota(jnp.int32, q.shape, 1)
    zero = jnp.zeros_like(q)
    qs = (jnp.where(lane < HEAD_DIM, q, zero), jnp.where(lane >= HEAD_DIM, q, zero))
    mx_ref[...] = jnp.full(mx_ref.shape, NEG_BIG, F32)
    acc_ref[...] = jnp.zeros(acc_ref.shape, F32)

    def score_block(j, masked):
        k = k_ref[0, pl.ds(pl.multiple_of(j * tk, tk), tk), :]
        maxes = [mx_ref[0], mx_ref[1]]
        for h in range(2):
            s = lax.dot_general(qs[h], k, (((1,), (1,)), ((), ())), preferred_element_type=F32)
            if masked:
                rows = lax.broadcasted_iota(jnp.int32, s.shape, 0)
                cols = lax.broadcasted_iota(jnp.int32, s.shape, 1)
                s = jnp.where(rows >= cols, s, NEG_BIG)
            s_buf[h, j] = s
            for sg in _lane_groups(s):
                maxes[h] = jnp.maximum(maxes[h], sg)
        mx_ref[0], mx_ref[1] = maxes

    def score_body(j, carry):
        score_block(j, False)
        return carry

    lax.fori_loop(0, i, score_body, 0)
    score_block(i, True)
    for h in range(2):
        m_ref[h] = jnp.broadcast_to(jnp.max(mx_ref[h], axis=1, keepdims=True), (tq, LANES))

    def pv_body(j, carry):
        v = v_ref[0, pl.ds(pl.multiple_of(j * tk, tk), tk), :]
        v_ext = jnp.concatenate([v, jnp.ones_like(v)], axis=1)
        accs = [acc_ref[0], acc_ref[1]]
        for h in range(2):
            m = m_ref[h]
            p = jnp.concatenate([jnp.exp2(sg - m) for sg in _lane_groups(s_buf[h, j])], axis=1)
            accs[h] = accs[h] + jnp.dot(p.astype(BF16), v_ext, preferred_element_type=F32)
        acc_ref[0], acc_ref[1] = accs
        return carry

    lax.fori_loop(0, i + 1, pv_body, 0)

    lam_p = lam_ref[...]
    lam = (jnp.exp(jnp.sum(lam_p[0:1] * lam_p[1:2], axis=1, keepdims=True))
           - jnp.exp(jnp.sum(lam_p[2:3] * lam_p[3:4], axis=1, keepdims=True)) + lambda_init)
    a0, a1 = acc_ref[0], acc_ref[1]
    o = a0[:, :LANES] / a0[:, LANES:] - lam * (a1[:, :LANES] / a1[:, LANES:])
    ms = jnp.mean(o * o, axis=-1, keepdims=True)
    y = o * lax.rsqrt(ms + NORM_EPS) * subg_ref[...] * (1.0 - lambda_init)
    o_ref[0] = y.astype(o_ref.dtype)


def _diff_attn(qa, ka, va, lam_p, subg, lambda_init, tq=512):
    b, s, w = qa.shape
    nh = w // LANES
    kern = functools.partial(_da_kernel, tq=tq, tk=tq, lambda_init=lambda_init)
    qspec = pl.BlockSpec((1, tq, LANES), lambda bb, h, i: (bb, i, h))
    kvspec = pl.BlockSpec((1, s, LANES), lambda bb, h, i: (bb, 0, h))
    return pl.pallas_call(
        kern,
        out_shape=jax.ShapeDtypeStruct((b, s, w), BF16),
        grid=(b, nh, s // tq),
        in_specs=[qspec, kvspec, kvspec,
                  pl.BlockSpec(lam_p.shape, lambda bb, h, i: (0, 0)),
                  pl.BlockSpec(subg.shape, lambda bb, h, i: (0, 0))],
        out_specs=qspec,
        scratch_shapes=[pltpu.VMEM((2, s // tq, tq, tq), F32),
                        pltpu.VMEM((2, tq, LANES), F32),
                        pltpu.VMEM((2, tq, LANES), F32),
                        pltpu.VMEM((2, tq, 2 * LANES), F32)],
        compiler_params=_params(("arbitrary",) * 3),
        name="diff_attn",
    )(qa, ka, va, lam_p, subg)


def _sb_kernel(q_ref, k_ref, v_ref, tri_ref, o_ref, c_ref, acc_ref, *, tq, tk):
    i = pl.program_id(2)
    tri = tri_ref[...]
    lane = lax.broadcasted_iota(jnp.int32, (tk, LANES), 1)

    def row_tile(r):
        q = q_ref[0, r * tk:(r + 1) * tk, :]
        zero = jnp.zeros_like(q)
        qs = (jnp.where(lane < HEAD_DIM, q, zero), jnp.where(lane >= HEAD_DIM, q, zero))
        c_ref[...] = jnp.zeros(c_ref.shape, F32)
        acc_ref[...] = jnp.zeros(acc_ref.shape, F32)

        def window(js, diag_first):
            ks = [k_ref[0, pl.ds(pl.multiple_of(j * tk, tk), tk), :] for j in js]
            vs = [v_ref[0, pl.ds(pl.multiple_of(j * tk, tk), tk), :] for j in js]
            v_all = jnp.concatenate(vs, axis=0)
            state = [(c_ref[h], acc_ref[h]) for h in range(2)]
            new_state = []
            for h in range(2):
                c_run, acc_prev = state[h]
                weights = []
                for n, k in enumerate(ks):
                    masked = diag_first and n == 0
                    z = lax.dot_general(qs[h], k, (((1,), (1,)), ((), ())), preferred_element_type=F32)
                    nz = -z
                    w = jnp.log2(1.0 + jnp.exp2(jnp.minimum(z, nz)))
                    log_1m = jnp.minimum(nz, 0.0) - w
                    log_b = jnp.minimum(z, 0.0) - w
                    if masked:
                        mask = (lax.broadcasted_iota(jnp.int32, z.shape, 1)
                                < lax.broadcasted_iota(jnp.int32, z.shape, 0))
                        log_1m = jnp.where(mask, log_1m, 0.0)
                    res = jnp.dot(log_1m.astype(BF16), tri, preferred_element_type=F32)
                    suffix, total = res[:, :tk], res[:, tk:]
                    a = jnp.concatenate([jnp.exp2(lb + sf + c_run)
                                         for lb, sf in zip(_lane_groups(log_b), _lane_groups(suffix))], axis=1)
                    if masked:
                        a = jnp.where(mask, a, 0.0)
                    weights.append(a.astype(BF16))
                    c_run = c_run + total
                a_all = jnp.concatenate(weights, axis=1)
                new_state.append((c_run, acc_prev + jnp.dot(a_all, v_all, preferred_element_type=F32)))
            for h in range(2):
                c_ref[h], acc_ref[h] = new_state[h]

        jd = i * (tq // tk) + r
        if r % 2 == 0:
            window([jd], True)
            top = jd - 1
        else:
            window([jd, jd - 1], True)
            top = jd - 2

        def cond(carry):
            j, live = carry
            return jnp.logical_and(j >= 1, live)

        def live():
            return jnp.max(jnp.maximum(c_ref[0], c_ref[1])) > SB_DEAD_LOG2

        def body(carry):
            j, _ = carry
            window([j, j - 1], False)
            return j - 2, live()

        lax.while_loop(cond, body, (top, live()))
        o_ref[0, r * tk:(r + 1) * tk, :] = jnp.where(lane < HEAD_DIM, acc_ref[0], acc_ref[1]).astype(o_ref.dtype)

    for r in range(tq // tk):
        row_tile(r)


def _sb_attn(qb, kb, vb, tq=512, tk=256):
    b, s, w = qb.shape
    nh = w // LANES
    r = lax.broadcasted_iota(jnp.int32, (tk, tk), 0)
    c = lax.broadcasted_iota(jnp.int32, (tk, tk), 1)
    tri = jnp.concatenate([(r > c).astype(BF16), jnp.ones((tk, LANES), BF16)], axis=1)
    kern = functools.partial(_sb_kernel, tq=tq, tk=tk)
    qspec = pl.BlockSpec((1, tq, LANES), lambda bb, h, i: (bb, i, h))
    kvspec = pl.BlockSpec((1, s, LANES), lambda bb, h, i: (bb, 0, h))
    return pl.pallas_call(
        kern,
        out_shape=jax.ShapeDtypeStruct((b, s, w), BF16),
        grid=(b, nh, s // tq),
        in_specs=[qspec, kvspec, kvspec, pl.BlockSpec(tri.shape, lambda bb, h, i: (0, 0))],
        out_specs=qspec,
        scratch_shapes=[pltpu.VMEM((2, tk, LANES), F32), pltpu.VMEM((2, tk, LANES), F32)],
        compiler_params=_params(("arbitrary",) * 3),
        name="sb_attn",
    )(qb, kb, vb, tri)


def _post_attn_kernel(oa_ref, ob_ref, ga_ref, gb_ref, x_ref, wda_ref, wsb_ref, wo_ref, g_ref,
                      wr_hi_ref, wr_lo_ref, br_ref, tri_ref,
                      x1_ref, h_ref, idx_ref, gate_ref, rank_ref, cnt_ref, carry_ref):
    i = pl.program_id(0)

    @pl.when(i == 0)
    def _():
        carry_ref[...] = jnp.zeros(carry_ref.shape, F32)

    ya = jnp.dot(oa_ref[...], wda_ref[...], preferred_element_type=F32)
    yb = jnp.dot(ob_ref[...], wsb_ref[...], preferred_element_type=F32)
    mix = ga_ref[...].astype(F32) * ya + gb_ref[...].astype(F32) * yb
    x1 = x_ref[...] + jnp.dot(mix.astype(BF16), wo_ref[...], preferred_element_type=F32)
    x1_ref[...] = x1
    ms = jnp.mean(x1 * x1, axis=-1, keepdims=True)
    h = x1 * lax.rsqrt(ms + NORM_EPS) * g_ref[...]
    h_ref[...] = h

    h_hi = h.astype(BF16)
    h_lo = (h - h_hi.astype(F32)).astype(BF16)
    nt = (((1,), (1,)), ((), ()))
    logits = (lax.dot_general(wr_hi_ref[...], h_hi, nt, preferred_element_type=F32)
              + lax.dot_general(wr_hi_ref[...], h_lo, nt, preferred_element_type=F32)
              + lax.dot_general(wr_lo_ref[...], h_hi, nt, preferred_element_type=F32)
              + br_ref[...])
    ne, tm = logits.shape
    eid = lax.broadcasted_iota(jnp.int32, (ne, tm), 0).astype(F32)
    vals, ids = [], []
    work = logits
    for _ in range(TOP_K):
        mx = jnp.max(work, axis=0, keepdims=True)
        sel = jnp.min(jnp.where(work == mx, eid, float(ne)), axis=0, keepdims=True)
        vals.append(mx)
        ids.append(sel)
        work = jnp.where(eid == sel, -jnp.inf, work)
    exps = [jnp.exp(v - vals[0]) for v in vals]
    denom = exps[0] + exps[1] + exps[2] + exps[3]
    onehots = [(eid == sel).astype(F32) for sel in ids]
    assigned = onehots[0] + onehots[1] + onehots[2] + onehots[3]
    before = jnp.dot(assigned.astype(BF16), tri_ref[...], preferred_element_type=F32) + carry_ref[...]
    for r in range(TOP_K):
        idx_ref[r:r + 1, :] = ids[r].astype(jnp.int32)
        gate_ref[r:r + 1, :] = exps[r] / denom
        rank_ref[r:r + 1, :] = jnp.sum(onehots[r] * before, axis=0, keepdims=True).astype(jnp.int32)
    carry = carry_ref[...] + jnp.sum(assigned, axis=1, keepdims=True)
    carry_ref[...] = carry
    cnt_ref[...] = jnp.broadcast_to(carry, cnt_ref.shape)


def _post_attn(oa, ob, ga, gb, x2, wda, wsb, wo, g, wr_hi, wr_lo, br, tm=512):
    t, d = x2.shape
    w = oa.shape[1]
    r = lax.broadcasted_iota(jnp.int32, (tm, tm), 0)
    c = lax.broadcasted_iota(jnp.int32, (tm, tm), 1)
    tri = (r < c).astype(BF16)
    row = lambda i: (i, 0)
    col = lambda i: (0, i)
    const = lambda i: (0, 0)
    full = lambda a: pl.BlockSpec(a.shape, const)
    return pl.pallas_call(
        _post_attn_kernel,
        out_shape=[jax.ShapeDtypeStruct((t, d), F32), jax.ShapeDtypeStruct((t, d), F32),
                   jax.ShapeDtypeStruct((TOP_K, t), jnp.int32), jax.ShapeDtypeStruct((TOP_K, t), F32),
                   jax.ShapeDtypeStruct((TOP_K, t), jnp.int32),
                   jax.ShapeDtypeStruct((N_EXPERTS, LANES), F32)],
        grid=(t // tm,),
        in_specs=[pl.BlockSpec((tm, w), row), pl.BlockSpec((tm, w), row),
                  pl.BlockSpec((tm, d), row), pl.BlockSpec((tm, d), row), pl.BlockSpec((tm, d), row),
                  full(wda), full(wsb), full(wo), full(g), full(wr_hi), full(wr_lo), full(br), full(tri)],
        out_specs=[pl.BlockSpec((tm, d), row), pl.BlockSpec((tm, d), row),
                   pl.BlockSpec((TOP_K, tm), col), pl.BlockSpec((TOP_K, tm), col),
                   pl.BlockSpec((TOP_K, tm), col), pl.BlockSpec((N_EXPERTS, LANES), const)],
        scratch_shapes=[pltpu.VMEM((N_EXPERTS, 1), F32)],
        compiler_params=_params(("arbitrary",)),
        name="post_attn",
    )(oa, ob, ga, gb, x2, wda, wsb, wo, g, wr_hi, wr_lo, br, tri)


def _dispatch_kernel(dest_ref, zero_blk_ref, n_used_ref, h_ref, xs_ref, zeros, sem, zsem, *, tm, t_total):
    i = pl.program_id(0)
    n_blocks = xs_ref.shape[0] // ROW_BLOCK

    def zero_copy(blk):
        row = pl.multiple_of(blk * ROW_BLOCK, ROW_BLOCK)
        return pltpu.make_async_copy(zeros, xs_ref.at[pl.ds(row, ROW_BLOCK), :], zsem)

    @pl.when(i == 0)
    def _():
        zeros[...] = jnp.zeros(zeros.shape, zeros.dtype)
        n_tail = n_blocks - n_used_ref[0]

        def start(n, carry):
            zero_copy(jnp.where(n < N_EXPERTS, zero_blk_ref[jnp.minimum(n, N_EXPERTS - 1)],
                                n_used_ref[0] + n - N_EXPERTS)).start()
            return carry

        def wait(n, carry):
            zero_copy(0).wait()
            return carry

        lax.fori_loop(0, N_EXPERTS + n_tail, start, 0)
        lax.fori_loop(0, N_EXPERTS + n_tail, wait, 0)

    def row_copy(t, k):
        dst = dest_ref[k * t_total + i * tm + t]
        return pltpu.make_async_copy(h_ref.at[pl.ds(t, 1), :], xs_ref.at[pl.ds(dst, 1), :], sem)

    def issue(t, carry):
        for k in range(TOP_K):
            row_copy(t, k).start(priority=k % 2)
        return carry

    lax.fori_loop(0, tm, issue, 0, unroll=8)
    for _ in range(TOP_K):
        pltpu.make_async_copy(h_ref, xs_ref.at[pl.ds(0, tm), :], sem).wait()


def _dispatch(dest_flat, zero_blk, n_used, h, p_rows, tm=256):
    t, d = h.shape
    kern = functools.partial(_dispatch_kernel, tm=tm, t_total=t)
    return pl.pallas_call(
        kern,
        out_shape=jax.ShapeDtypeStruct((p_rows, d), h.dtype),
        grid_spec=pltpu.PrefetchScalarGridSpec(
            num_scalar_prefetch=3,
            grid=(t // tm,),
            in_specs=[pl.BlockSpec((tm, d), lambda i, *_: (i, 0))],
            out_specs=pl.BlockSpec(memory_space=pl.ANY),
            scratch_shapes=[pltpu.VMEM((ROW_BLOCK, d), h.dtype), pltpu.SemaphoreType.DMA,
                            pltpu.SemaphoreType.DMA],
        ),
        compiler_params=_params(("arbitrary",)),
        name="dispatch",
    )(dest_flat, zero_blk, n_used, h)


def _expert_kernel(blk_e_ref, n_used_ref, xs_ref, wg_ref, bg_ref, wu_ref, bu_ref, wd_ref, bd_ref,
                   o_ref, wg_bf, wu_bf, wd_bf):
    i = pl.program_id(0)
    e = blk_e_ref[i]
    prev = blk_e_ref[jnp.maximum(i - 1, 0)]
    used = i < n_used_ref[0]

    @pl.when(used & ((i == 0) | (e != prev)))
    def _():
        wg_bf[...] = wg_ref[0].astype(BF16)
        wu_bf[...] = wu_ref[0].astype(BF16)
        wd_bf[...] = wd_ref[0].astype(BF16)

    @pl.when(used)
    def _():
        x = xs_ref[...].astype(BF16)
        g = jnp.dot(x, wg_bf[...], preferred_element_type=F32) + bg_ref[0]
        u = jnp.dot(x, wu_bf[...], preferred_element_type=F32) + bu_ref[0]
        g = jnp.minimum(g, SWIGLU_LIMIT)
        u = jnp.clip(u, -SWIGLU_LIMIT, SWIGLU_LIMIT)
        glu = g * jax.nn.sigmoid(SWIGLU_ALPHA * g)
        act = ((u + 1.0) * glu).astype(BF16)
        o_ref[...] = jnp.dot(act, wd_bf[...], preferred_element_type=F32) + bd_ref[0]

    @pl.when(jnp.logical_not(used))
    def _():
        o_ref[...] = jnp.zeros(o_ref.shape, o_ref.dtype)


def _experts(blk_e, n_used, xs, wg, bg, wu, bu, wd, bd):
    p, d = xs.shape
    f = wg.shape[2]
    nb = p // ROW_BLOCK
    wmap = lambda i, be, nu: (be[i], 0, 0)
    xmap = lambda i, be, nu: (jnp.minimum(i, nu[0] - 1), 0)
    return pl.pallas_call(
        _expert_kernel,
        out_shape=jax.ShapeDtypeStruct((p, d), F32),
        grid_spec=pltpu.PrefetchScalarGridSpec(
            num_scalar_prefetch=2,
            grid=(nb,),
            in_specs=[pl.BlockSpec((ROW_BLOCK, d), xmap),
                      pl.BlockSpec((1, d, f), wmap), pl.BlockSpec((1, 1, f), wmap),
                      pl.BlockSpec((1, d, f), wmap), pl.BlockSpec((1, 1, f), wmap),
                      pl.BlockSpec((1, f, d), wmap), pl.BlockSpec((1, 1, d), wmap)],
            out_specs=pl.BlockSpec((ROW_BLOCK, d), lambda i, be, nu: (i, 0)),
            scratch_shapes=[pltpu.VMEM((d, f), BF16), pltpu.VMEM((d, f), BF16), pltpu.VMEM((f, d), BF16)],
        ),
        compiler_params=_params(("arbitrary",)),
        name="experts",
    )(blk_e, n_used, xs, wg, bg, wu, bu, wd, bd)


def _combine_kernel(dest_ref, rows_ref, gate_ref, x1_ref, g_ref, o_ref, buf, sem, *, tm, t_total):
    i = pl.program_id(0)

    def row_copy(t, k):
        src = dest_ref[k * t_total + i * tm + t]
        return pltpu.make_async_copy(rows_ref.at[pl.ds(src, 1), :], buf.at[k, pl.ds(t, 1), :], sem)

    def issue(t, carry):
        for k in range(TOP_K):
            row_copy(t, k).start(priority=k % 2)
        return carry

    lax.fori_loop(0, tm, issue, 0, unroll=8)
    for k in range(TOP_K):
        pltpu.make_async_copy(rows_ref.at[pl.ds(0, tm), :], buf.at[k], sem).wait()
    gates = gate_ref[...]
    y = x1_ref[...]
    for k in range(TOP_K):
        y = y + gates[:, k:k + 1] * buf[k]
    ms = jnp.mean(y * y, axis=-1, keepdims=True)
    o_ref[...] = y * lax.rsqrt(ms + NORM_EPS) * g_ref[...]


def _combine(dest_flat, rows, gates_t, x1, g, tm=256):
    t, d = x1.shape
    kern = functools.partial(_combine_kernel, tm=tm, t_total=t)
    return pl.pallas_call(
        kern,
        out_shape=jax.ShapeDtypeStruct((t, d), F32),
        grid_spec=pltpu.PrefetchScalarGridSpec(
            num_scalar_prefetch=1,
            grid=(t // tm,),
            in_specs=[pl.BlockSpec(memory_space=pl.ANY),
                      pl.BlockSpec((tm, TOP_K), lambda i, dest: (i, 0)),
                      pl.BlockSpec((tm, d), lambda i, dest: (i, 0)),
                      pl.BlockSpec((1, d), lambda i, dest: (0, 0))],
            out_specs=pl.BlockSpec((tm, d), lambda i, dest: (i, 0)),
            scratch_shapes=[pltpu.VMEM((TOP_K, tm, d), F32), pltpu.SemaphoreType.DMA],
        ),
        compiler_params=_params(("arbitrary",)),
        name="combine",
    )(dest_flat, rows, gates_t, x1, g)


def _rope_tables(seq):
    inv = 1.0 / (ROPE_THETA ** (jnp.arange(0, HEAD_DIM, 2, dtype=F32) / HEAD_DIM))
    ang = jnp.arange(seq, dtype=F32)[:, None] * inv[None, :]
    cos, sin = jnp.cos(ang), jnp.sin(ang)
    return jnp.concatenate([cos] * 4, axis=1), jnp.concatenate([-sin, sin, -sin, sin], axis=1)


def kernel(x, norm_mix_g, w_in, lambda_q1, lambda_k1, lambda_q2, lambda_k2, da_subln_g, w_da_out, w_sb_out, w_o, norm_ffn_g, w_router, b_router, w_gate, b_gate, w_up, b_up, w_down, b_down, norm_final_g):
    b, s, d = x.shape
    depth = w_in.shape[0]
    t = b * s
    cos_t, sin_t = _rope_tables(s)
    n_blocks = (t * TOP_K + ROW_BLOCK - 1) // ROW_BLOCK + N_EXPERTS
    p_rows = n_blocks * ROW_BLOCK
    x2 = x.reshape(t, d)
    for l in range(depth):
        lambda_init = 0.8 - 0.6 * math.exp(-0.3 * l)
        qa, ka, va, qb, kb, vb, ga, gb = _in_proj(
            x2, norm_mix_g[l][None, :], w_in[l].astype(BF16), cos_t, sin_t, s)
        lam_p = jnp.stack([lambda_q1[l], lambda_k1[l], lambda_q2[l], lambda_k2[l]]).astype(F32)
        seq3 = lambda a: a.reshape(b, s, a.shape[1])
        oa = _diff_attn(seq3(qa), seq3(ka), seq3(va), lam_p, da_subln_g[l][None, :].astype(F32), lambda_init)
        ob = _sb_attn(seq3(qb), seq3(kb), seq3(vb))
        wr = w_router[l].T.astype(F32)
        wr_hi = wr.astype(BF16)
        wr_lo = (wr - wr_hi.astype(F32)).astype(BF16)
        x1, h, idx, gates, rank, cnt = _post_attn(
            oa.reshape(t, -1), ob.reshape(t, -1), ga, gb, x2,
            w_da_out[l].astype(BF16), w_sb_out[l].astype(BF16), w_o[l].astype(BF16),
            norm_ffn_g[l][None, :], wr_hi, wr_lo, b_router[l][:, None].astype(F32))
        counts = cnt[:, 0].astype(jnp.int32)
        padded = (counts + ROW_BLOCK - 1) // ROW_BLOCK * ROW_BLOCK
        pad_ends = jnp.cumsum(padded)
        pad_starts = pad_ends - padded
        experts = jnp.arange(N_EXPERTS, dtype=jnp.int32)
        base = jnp.sum(jnp.where(idx[:, :, None] == experts, pad_starts, 0), axis=-1)
        dest = (base + rank).reshape(-1)
        n_used = (pad_ends[-1] // ROW_BLOCK).astype(jnp.int32)
        blk = jnp.minimum(jnp.arange(n_blocks, dtype=jnp.int32), n_used - 1) * ROW_BLOCK
        blk_e = jnp.minimum(jnp.sum(pad_ends[None, :] <= blk[:, None], axis=1), N_EXPERTS - 1).astype(jnp.int32)
        zero_blk = jnp.maximum(pad_ends // ROW_BLOCK - 1, 0).astype(jnp.int32)
        xs = _dispatch(dest, zero_blk, n_used[None], h, p_rows)
        rows = _experts(blk_e, n_used[None], xs, w_gate[l], b_gate[l][:, None, :], w_up[l], b_up[l][:, None, :],
                        w_down[l], b_down[l][:, None, :])
        g_next = norm_final_g[None, :] if l == depth - 1 else jnp.ones((1, d), F32)
        x2 = _combine(dest, rows, gates.T, x1, g_next)
        if l != depth - 1:
            raise NotImplementedError("only the final layer's norm is fused into the combine kernel")
    return x2.reshape(b, s, d)
```

```python
import functools
import math

import jax
import jax.numpy as jnp
from jax import lax
from jax.experimental import pallas as pl
from jax.experimental.pallas import tpu as pltpu

F32 = jnp.float32
BF16 = jnp.bfloat16

DA_HEADS = 4
HEAD_DIM = 64
N_EXPERTS = 32
TOP_K = 4
ROPE_THETA = 10000.0
SWIGLU_LIMIT = 7.0
SWIGLU_ALPHA = 1.702
NORM_EPS = 1e-5
ROW_BLOCK = 256
COMBINE_TILE = 256
SUBLANES = 8
COMBINE_WINDOW = 72
LANES = 128
NEG_BIG = -1e30
LOG2E = math.log2(math.e)
SB_DEAD_LOG2 = -150.0

VMEM_LIMIT = 56 * 1024 * 1024


def _params(sem, vmem=VMEM_LIMIT):
    return pltpu.CompilerParams(dimension_semantics=sem, vmem_limit_bytes=vmem)


def _in_proj_kernel(x_ref, g_ref, w_ref, cos_ref, sin_ref,
                    qa_ref, ka_ref, va_ref, qb_ref, kb_ref, vb_ref, ga_ref, gb_ref):
    x = x_ref[...]
    ms = jnp.mean(x * x, axis=-1, keepdims=True)
    h = (x * lax.rsqrt(ms + NORM_EPS) * g_ref[...]).astype(BF16)
    cos = cos_ref[...]
    sin = sin_ref[...]
    lane = lax.broadcasted_iota(jnp.int32, cos.shape, 1)
    first_half = (lane & (HEAD_DIM - 1)) < HEAD_DIM // 2

    def proj(c0, width):
        return jnp.dot(h, w_ref[:, c0:c0 + width], preferred_element_type=F32)

    def rope(r):
        outs = []
        for g in range(r.shape[1] // LANES):
            xg = r[:, g * LANES:(g + 1) * LANES]
            rot = jnp.where(first_half, pltpu.roll(xg, LANES - HEAD_DIM // 2, 1),
                            pltpu.roll(xg, HEAD_DIM // 2, 1))
            outs.append(xg * cos + rot * sin)
        return jnp.concatenate(outs, axis=1)

    scale = HEAD_DIM ** -0.5 * LOG2E
    w = qa_ref.shape[1]
    d = ga_ref.shape[1]
    qa_ref[...] = (rope(proj(0, w)) * scale).astype(BF16)
    ka_ref[...] = rope(proj(w, w)).astype(BF16)
    va_ref[...] = proj(2 * w, w).astype(BF16)
    qb_ref[...] = (proj(3 * w, w) * scale).astype(BF16)
    kb_ref[...] = proj(4 * w, w).astype(BF16)
    vb_ref[...] = proj(5 * w, w).astype(BF16)
    ga_ref[...] = jax.nn.sigmoid(proj(6 * w, d)).astype(BF16)
    gb_ref[...] = jax.nn.sigmoid(proj(6 * w + d, d)).astype(BF16)


def _in_proj(x2, g, w_in_bf, cos_t, sin_t, seq, tm=512):
    t, d = x2.shape
    w = 512
    nseq = seq // tm
    outs = [jax.ShapeDtypeStruct((t, w), BF16)] * 6 + [jax.ShapeDtypeStruct((t, d), BF16)] * 2
    row = lambda i: (i, 0)
    return pl.pallas_call(
        _in_proj_kernel,
        out_shape=outs,
        grid=(t // tm,),
        in_specs=[
            pl.BlockSpec((tm, d), row),
            pl.BlockSpec((1, d), lambda i: (0, 0)),
            pl.BlockSpec(w_in_bf.shape, lambda i: (0, 0)),
            pl.BlockSpec((tm, LANES), lambda i: (i % nseq, 0)),
            pl.BlockSpec((tm, LANES), lambda i: (i % nseq, 0)),
        ],
        out_specs=[pl.BlockSpec((tm, w), row)] * 6 + [pl.BlockSpec((tm, d), row)] * 2,
        compiler_params=_params(("arbitrary",)),
        name="in_proj",
    )(x2, g, w_in_bf, cos_t, sin_t)


def _lane_groups(x):
    return [x[:, g * LANES:(g + 1) * LANES] for g in range(x.shape[1] // LANES)]


def _da_kernel(q_ref, k_ref, v_ref, lam_ref, subg_ref, o_ref, s_buf, mx_ref, m_ref, acc_ref,
               *, tq, tk, lambda_init):
    i = pl.program_id(2)
    q = q_ref[0]
    lane = lax.broadcasted_iota(jnp.int32, q.shape, 1)
    zero = jnp.zeros_like(q)
    qs = (jnp.where(lane < HEAD_DIM, q, zero), jnp.where(lane >= HEAD_DIM, q, zero))
    mx_ref[...] = jnp.full(mx_ref.shape, NEG_BIG, F32)
    acc_ref[...] = jnp.zeros(acc_ref.shape, F32)

    def score_blocks(js, diag_last):
        maxes = [mx_ref[0], mx_ref[1]]
        for n, j in enumerate(js):
            k = k_ref[0, pl.ds(pl.multiple_of(j * tk, tk), tk), :]
            for h in range(2):
                s = lax.dot_general(qs[h], k, (((1,), (1,)), ((), ())), preferred_element_type=F32)
                if diag_last and n == len(js) - 1:
                    rows = lax.broadcasted_iota(jnp.int32, s.shape, 0)
                    cols = lax.broadcasted_iota(jnp.int32, s.shape, 1)
                    s = jnp.where(rows >= cols, s, NEG_BIG)
                s_buf[h, j] = s
                for sg in _lane_groups(s):
                    maxes[h] = jnp.maximum(maxes[h], sg)
        mx_ref[0], mx_ref[1] = maxes

    def score_body(jj, carry):
        score_blocks([2 * jj, 2 * jj + 1], False)
        return carry

    lax.fori_loop(0, i // 2, score_body, 0)

    @pl.when(i % 2 == 0)
    def _():
        score_blocks([i], True)

    @pl.when(i % 2 == 1)
    def _():
        score_blocks([i - 1, i], True)

    for h in range(2):
        m_ref[h] = jnp.broadcast_to(jnp.max(mx_ref[h], axis=1, keepdims=True), (tq, LANES))

    def pv_blocks(js):
        vs = [v_ref[0, pl.ds(pl.multiple_of(j * tk, tk), tk), :] for j in js]
        v_all = jnp.concatenate(vs, axis=0)
        v_ext = jnp.concatenate([v_all, jnp.ones_like(v_all)], axis=1)
        accs = [acc_ref[0], acc_ref[1]]
        for h in range(2):
            m = m_ref[h]
            p = jnp.concatenate([jnp.exp2(sg - m).astype(BF16)
                                 for j in js for sg in _lane_groups(s_buf[h, j])], axis=1)
            accs[h] = accs[h] + jnp.dot(p, v_ext, preferred_element_type=F32)
        acc_ref[0], acc_ref[1] = accs

    def pv_body(jj, carry):
        pv_blocks([2 * jj, 2 * jj + 1])
        return carry

    lax.fori_loop(0, (i + 1) // 2, pv_body, 0)

    @pl.when(i % 2 == 0)
    def _():
        pv_blocks([i])

    lam_p = lam_ref[...]
    lam = (jnp.exp(jnp.sum(lam_p[0:1] * lam_p[1:2], axis=1, keepdims=True))
           - jnp.exp(jnp.sum(lam_p[2:3] * lam_p[3:4], axis=1, keepdims=True)) + lambda_init)
    a0, a1 = acc_ref[0], acc_ref[1]
    o = a0[:, :LANES] / a0[:, LANES:] - lam * (a1[:, :LANES] / a1[:, LANES:])
    ms = jnp.mean(o * o, axis=-1, keepdims=True)
    y = o * lax.rsqrt(ms + NORM_EPS) * subg_ref[...] * (1.0 - lambda_init)
    o_ref[0] = y.astype(o_ref.dtype)


def _diff_attn(qa, ka, va, lam_p, subg, lambda_init, tq=512):
    b, s, w = qa.shape
    nh = w // LANES
    kern = functools.partial(_da_kernel, tq=tq, tk=tq, lambda_init=lambda_init)
    qspec = pl.BlockSpec((1, tq, LANES), lambda bb, h, i: (bb, i, h))
    kvspec = pl.BlockSpec((1, s, LANES), lambda bb, h, i: (bb, 0, h))
    return pl.pallas_call(
        kern,
        out_shape=jax.ShapeDtypeStruct((b, s, w), BF16),
        grid=(b, nh, s // tq),
        in_specs=[qspec, kvspec, kvspec,
                  pl.BlockSpec(lam_p.shape, lambda bb, h, i: (0, 0)),
                  pl.BlockSpec(subg.shape, lambda bb, h, i: (0, 0))],
        out_specs=qspec,
        scratch_shapes=[pltpu.VMEM((2, s // tq, tq, tq), F32),
                        pltpu.VMEM((2, tq, LANES), F32),
                        pltpu.VMEM((2, tq, LANES), F32),
                        pltpu.VMEM((2, tq, 2 * LANES), F32)],
        compiler_params=_params(("arbitrary",) * 3),
        name="diff_attn",
    )(qa, ka, va, lam_p, subg)


def _sb_kernel(q_ref, k_ref, v_ref, tri_ref, o_ref, c_ref, acc_ref, *, tq, tk):
    i = pl.program_id(2)
    tri = tri_ref[...]
    lane = lax.broadcasted_iota(jnp.int32, (tk, LANES), 1)

    def row_tile(r):
        q = q_ref[0, r * tk:(r + 1) * tk, :]
        zero = jnp.zeros_like(q)
        qs = (jnp.where(lane < HEAD_DIM, q, zero), jnp.where(lane >= HEAD_DIM, q, zero))
        c_ref[...] = jnp.zeros(c_ref.shape, F32)
        acc_ref[...] = jnp.zeros(acc_ref.shape, F32)

        def window(js, diag_first):
            ks = [k_ref[0, pl.ds(pl.multiple_of(j * tk, tk), tk), :] for j in js]
            vs = [v_ref[0, pl.ds(pl.multiple_of(j * tk, tk), tk), :] for j in js]
            v_all = jnp.concatenate(vs, axis=0)
            state = [(c_ref[h], acc_ref[h]) for h in range(2)]
            new_state = []
            for h in range(2):
                c_run, acc_prev = state[h]
                weights = []
                for n, k in enumerate(ks):
                    masked = diag_first and n == 0
                    z = lax.dot_general(qs[h], k, (((1,), (1,)), ((), ())), preferred_element_type=F32)
                    nz = -z
                    w = jnp.log2(1.0 + jnp.exp2(jnp.minimum(z, nz)))
                    log_1m = jnp.minimum(nz, 0.0) - w
                    log_b = jnp.minimum(z, 0.0) - w
                    if masked:
                        mask = (lax.broadcasted_iota(jnp.int32, z.shape, 1)
                                < lax.broadcasted_iota(jnp.int32, z.shape, 0))
                        log_1m = jnp.where(mask, log_1m, 0.0)
                    res = jnp.dot(log_1m.astype(BF16), tri, preferred_element_type=F32)
                    suffix, total = res[:, :tk], res[:, tk:]
                    a = jnp.concatenate([jnp.exp2(lb + sf + c_run)
                                         for lb, sf in zip(_lane_groups(log_b), _lane_groups(suffix))], axis=1)
                    if masked:
                        a = jnp.where(mask, a, 0.0)
                    weights.append(a.astype(BF16))
                    c_run = c_run + total
                a_all = jnp.concatenate(weights, axis=1)
                new_state.append((c_run, acc_prev + jnp.dot(a_all, v_all, preferred_element_type=F32)))
            for h in range(2):
                c_ref[h], acc_ref[h] = new_state[h]

        jd = i * (tq // tk) + r
        if r % 2 == 0:
            window([jd], True)
            top = jd - 1
        else:
            window([jd, jd - 1], True)
            top = jd - 2

        def cond(carry):
            j, live = carry
            return jnp.logical_and(j >= 1, live)

        def live():
            return jnp.max(jnp.maximum(c_ref[0], c_ref[1])) > SB_DEAD_LOG2

        def body(carry):
            j, _ = carry
            window([j, j - 1], False)
            return j - 2, live()

        lax.while_loop(cond, body, (top, live()))
        o_ref[0, r * tk:(r + 1) * tk, :] = jnp.where(lane < HEAD_DIM, acc_ref[0], acc_ref[1]).astype(o_ref.dtype)

    for r in range(tq // tk):
        row_tile(r)


def _sb_attn(qb, kb, vb, tq=512, tk=256):
    b, s, w = qb.shape
    nh = w // LANES
    r = lax.broadcasted_iota(jnp.int32, (tk, tk), 0)
    c = lax.broadcasted_iota(jnp.int32, (tk, tk), 1)
    tri = jnp.concatenate([(r > c).astype(BF16), jnp.ones((tk, LANES), BF16)], axis=1)
    kern = functools.partial(_sb_kernel, tq=tq, tk=tk)
    qspec = pl.BlockSpec((1, tq, LANES), lambda bb, h, i: (bb, i, h))
    kvspec = pl.BlockSpec((1, s, LANES), lambda bb, h, i: (bb, 0, h))
    return pl.pallas_call(
        kern,
        out_shape=jax.ShapeDtypeStruct((b, s, w), BF16),
        grid=(b, nh, s // tq),
        in_specs=[qspec, kvspec, kvspec, pl.BlockSpec(tri.shape, lambda bb, h, i: (0, 0))],
        out_specs=qspec,
        scratch_shapes=[pltpu.VMEM((2, tk, LANES), F32), pltpu.VMEM((2, tk, LANES), F32)],
        compiler_params=_params(("arbitrary",) * 3),
        name="sb_attn",
    )(qb, kb, vb, tri)


def _post_attn_kernel(oa_ref, ob_ref, ga_ref, gb_ref, x_ref, wda_ref, wsb_ref, wo_ref, g_ref,
                      wr_hi_ref, wr_lo_ref, br_ref, tri_ref,
                      x1_ref, h_ref, idx_ref, gate_ref, rank_ref, cnt_ref, carry_ref):
    i = pl.program_id(0)

    @pl.when(i == 0)
    def _():
        carry_ref[...] = jnp.zeros(carry_ref.shape, F32)

    ya = jnp.dot(oa_ref[...], wda_ref[...], preferred_element_type=F32)
    yb = jnp.dot(ob_ref[...], wsb_ref[...], preferred_element_type=F32)
    mix = ga_ref[...].astype(F32) * ya + gb_ref[...].astype(F32) * yb
    x1 = x_ref[...] + jnp.dot(mix.astype(BF16), wo_ref[...], preferred_element_type=F32)
    x1_ref[...] = x1
    ms = jnp.mean(x1 * x1, axis=-1, keepdims=True)
    h = x1 * lax.rsqrt(ms + NORM_EPS) * g_ref[...]
    h_ref[...] = h

    h_hi = h.astype(BF16)
    h_lo = (h - h_hi.astype(F32)).astype(BF16)
    nt = (((1,), (1,)), ((), ()))
    logits = (lax.dot_general(wr_hi_ref[...], h_hi, nt, preferred_element_type=F32)
              + lax.dot_general(wr_hi_ref[...], h_lo, nt, preferred_element_type=F32)
              + lax.dot_general(wr_lo_ref[...], h_hi, nt, preferred_element_type=F32)
              + br_ref[...])
    ne, tm = logits.shape
    eid = lax.broadcasted_iota(jnp.int32, (ne, tm), 0).astype(F32)
    vals, ids = [], []
    work = logits
    for _ in range(TOP_K):
        mx = jnp.max(work, axis=0, keepdims=True)
        sel = jnp.min(jnp.where(work == mx, eid, float(ne)), axis=0, keepdims=True)
        vals.append(mx)
        ids.append(sel)
        work = jnp.where(eid == sel, -jnp.inf, work)
    exps = [jnp.exp(v - vals[0]) for v in vals]
    denom = exps[0] + exps[1] + exps[2] + exps[3]
    onehots = [(eid == sel).astype(F32) for sel in ids]
    assigned = onehots[0] + onehots[1] + onehots[2] + onehots[3]
    before = jnp.dot(assigned.astype(BF16), tri_ref[...], preferred_element_type=F32) + carry_ref[...]
    for r in range(TOP_K):
        idx_ref[r:r + 1, :] = ids[r].astype(jnp.int32)
        gate_ref[r:r + 1, :] = exps[r] / denom
        rank_ref[r:r + 1, :] = jnp.sum(onehots[r] * before, axis=0, keepdims=True).astype(jnp.int32)
    carry = carry_ref[...] + jnp.sum(assigned, axis=1, keepdims=True)
    carry_ref[...] = carry
    cnt_ref[...] = jnp.broadcast_to(carry, cnt_ref.shape)


def _post_attn(oa, ob, ga, gb, x2, wda, wsb, wo, g, wr_hi, wr_lo, br, tm=512):
    t, d = x2.shape
    w = oa.shape[1]
    r = lax.broadcasted_iota(jnp.int32, (tm, tm), 0)
    c = lax.broadcasted_iota(jnp.int32, (tm, tm), 1)
    tri = (r < c).astype(BF16)
    row = lambda i: (i, 0)
    col = lambda i: (0, i)
    const = lambda i: (0, 0)
    full = lambda a: pl.BlockSpec(a.shape, const)
    return pl.pallas_call(
        _post_attn_kernel,
        out_shape=[jax.ShapeDtypeStruct((t, d), F32), jax.ShapeDtypeStruct((t, d), F32),
                   jax.ShapeDtypeStruct((TOP_K, t), jnp.int32), jax.ShapeDtypeStruct((TOP_K, t), F32),
                   jax.ShapeDtypeStruct((TOP_K, t), jnp.int32),
                   jax.ShapeDtypeStruct((N_EXPERTS, LANES), F32)],
        grid=(t // tm,),
        in_specs=[pl.BlockSpec((tm, w), row), pl.BlockSpec((tm, w), row),
                  pl.BlockSpec((tm, d), row), pl.BlockSpec((tm, d), row), pl.BlockSpec((tm, d), row),
                  full(wda), full(wsb), full(wo), full(g), full(wr_hi), full(wr_lo), full(br), full(tri)],
        out_specs=[pl.BlockSpec((tm, d), row), pl.BlockSpec((tm, d), row),
                   pl.BlockSpec((TOP_K, tm), col), pl.BlockSpec((TOP_K, tm), col),
                   pl.BlockSpec((TOP_K, tm), col), pl.BlockSpec((N_EXPERTS, LANES), const)],
        scratch_shapes=[pltpu.VMEM((N_EXPERTS, 1), F32)],
        compiler_params=_params(("arbitrary",)),
        name="post_attn",
    )(oa, ob, ga, gb, x2, wda, wsb, wo, g, wr_hi, wr_lo, br, tri)


def _dispatch_kernel(dest_ref, zero_blk_ref, n_used_ref, h_ref, xs_ref, zeros, sem, zsem, *, tm, t_total):
    i = pl.program_id(0)
    n_blocks = xs_ref.shape[0] // ROW_BLOCK

    def zero_copy(blk):
        row = pl.multiple_of(blk * ROW_BLOCK, ROW_BLOCK)
        return pltpu.make_async_copy(zeros, xs_ref.at[pl.ds(row, ROW_BLOCK), :], zsem)

    @pl.when(i == 0)
    def _():
        zeros[...] = jnp.zeros(zeros.shape, zeros.dtype)
        n_tail = n_blocks - n_used_ref[0]

        def start(n, carry):
            zero_copy(jnp.where(n < N_EXPERTS, zero_blk_ref[jnp.minimum(n, N_EXPERTS - 1)],
                                n_used_ref[0] + n - N_EXPERTS)).start()
            return carry

        def wait(n, carry):
            zero_copy(0).wait()
            return carry

        lax.fori_loop(0, N_EXPERTS + n_tail, start, 0)
        lax.fori_loop(0, N_EXPERTS + n_tail, wait, 0)

    def row_copy(t, k):
        dst = dest_ref[k * t_total + i * tm + t]
        return pltpu.make_async_copy(h_ref.at[pl.ds(t, 1), :], xs_ref.at[pl.ds(dst, 1), :], sem)

    def issue(t, carry):
        for k in range(TOP_K):
            row_copy(t, k).start(priority=k % 2)
        return carry

    lax.fori_loop(0, tm, issue, 0, unroll=8)
    for _ in range(TOP_K):
        pltpu.make_async_copy(h_ref, xs_ref.at[pl.ds(0, tm), :], sem).wait()


def _dispatch(dest_flat, zero_blk, n_used, h, p_rows, tm=256):
    t, d = h.shape
    kern = functools.partial(_dispatch_kernel, tm=tm, t_total=t)
    return pl.pallas_call(
        kern,
        out_shape=jax.ShapeDtypeStruct((p_rows, d), h.dtype),
        grid_spec=pltpu.PrefetchScalarGridSpec(
            num_scalar_prefetch=3,
            grid=(t // tm,),
            in_specs=[pl.BlockSpec((tm, d), lambda i, *_: (i, 0))],
            out_specs=pl.BlockSpec(memory_space=pl.ANY),
            scratch_shapes=[pltpu.VMEM((ROW_BLOCK, d), h.dtype), pltpu.SemaphoreType.DMA,
                            pltpu.SemaphoreType.DMA],
        ),
        compiler_params=_params(("arbitrary",)),
        name="dispatch",
    )(dest_flat, zero_blk, n_used, h)


def _expert_kernel(blk_e_ref, n_used_ref, xs_ref, wg_ref, bg_ref, wu_ref, bu_ref, wd_ref, bd_ref,
                   o_ref, wg_bf, wu_bf, wd_bf):
    i = pl.program_id(0)
    e = blk_e_ref[i]
    prev = blk_e_ref[jnp.maximum(i - 1, 0)]
    used = i < n_used_ref[0]

    @pl.when(used & ((i == 0) | (e != prev)))
    def _():
        wg_bf[...] = wg_ref[0].astype(BF16)
        wu_bf[...] = wu_ref[0].astype(BF16)
        wd_bf[...] = wd_ref[0].astype(BF16)

    @pl.when(used)
    def _():
        x = xs_ref[...].astype(BF16)
        g = jnp.dot(x, wg_bf[...], preferred_element_type=F32) + bg_ref[0]
        u = jnp.dot(x, wu_bf[...], preferred_element_type=F32) + bu_ref[0]
        g = jnp.minimum(g, SWIGLU_LIMIT)
        u = jnp.clip(u, -SWIGLU_LIMIT, SWIGLU_LIMIT)
        glu = g * jax.nn.sigmoid(SWIGLU_ALPHA * g)
        act = ((u + 1.0) * glu).astype(BF16)
        o_ref[...] = jnp.dot(act, wd_bf[...], preferred_element_type=F32) + bd_ref[0]

    @pl.when(jnp.logical_not(used))
    def _():
        o_ref[...] = jnp.zeros(o_ref.shape, o_ref.dtype)


def _experts(blk_e, n_used, xs, wg, bg, wu, bu, wd, bd):
    d = xs.shape[1]
    f = wg.shape[2]
    nb = blk_e.shape[0]
    wmap = lambda i, be, nu: (be[i], 0, 0)
    xmap = lambda i, be, nu: (jnp.minimum(i, nu[0] - 1), 0)
    return pl.pallas_call(
        _expert_kernel,
        out_shape=jax.ShapeDtypeStruct((nb * ROW_BLOCK, d), F32),
        grid_spec=pltpu.PrefetchScalarGridSpec(
            num_scalar_prefetch=2,
            grid=(nb,),
            in_specs=[pl.BlockSpec((ROW_BLOCK, d), xmap),
                      pl.BlockSpec((1, d, f), wmap), pl.BlockSpec((1, 1, f), wmap),
                      pl.BlockSpec((1, d, f), wmap), pl.BlockSpec((1, 1, f), wmap),
                      pl.BlockSpec((1, f, d), wmap), pl.BlockSpec((1, 1, d), wmap)],
            out_specs=pl.BlockSpec((ROW_BLOCK, d), lambda i, be, nu: (i, 0)),
            scratch_shapes=[pltpu.VMEM((d, f), BF16), pltpu.VMEM((d, f), BF16), pltpu.VMEM((f, d), BF16)],
        ),
        compiler_params=_params(("arbitrary",)),
        name="experts",
    )(blk_e, n_used, xs, wg, bg, wu, bu, wd, bd)


def _combine_kernel(base_ref, npass_ref, rows_ref, idx_ref, lr_ref, gate_ref, x1_ref, g_ref, o_ref,
                    wbuf, sem, *, tm):
    i = pl.program_id(0)
    slot = i % 2
    win = COMBINE_WINDOW
    width = N_EXPERTS * win

    def start_fetch(tile, p, s):
        for e in range(N_EXPERTS):
            start = pl.multiple_of(base_ref[tile * N_EXPERTS + e] + p * win, SUBLANES)
            pltpu.make_async_copy(rows_ref.at[pl.ds(start, win), :],
                                  wbuf.at[s, pl.ds(e * win, win), :], sem.at[s]).start()

    def wait_fetch(s):
        pltpu.make_async_copy(rows_ref.at[pl.ds(0, width), :], wbuf.at[s], sem.at[s]).wait()

    @pl.when(i == 0)
    def _():
        start_fetch(0, 0, 0)

    @pl.when(i + 1 < pl.num_programs(0))
    def _():
        start_fetch(i + 1, 0, 1 - slot)

    idx = idx_ref[...]
    lr = lr_ref[...]
    gates = gate_ref[...]
    col = lax.broadcasted_iota(jnp.int32, (tm, width), 1)

    def weights(p):
        w = jnp.zeros((tm, width), F32)
        for k in range(TOP_K):
            r = lr[:, k:k + 1] - p * win
            tgt = jnp.where((r >= 0) & (r < win), idx[:, k:k + 1] * win + r, -1)
            w = jnp.where(col == tgt, gates[:, k:k + 1], w)
        return w.astype(BF16)

    def gathered(p):
        return jnp.dot(weights(p), wbuf[slot].astype(BF16), preferred_element_type=F32)

    wait_fetch(slot)
    y = x1_ref[...] + gathered(0)

    def extra_pass(p, acc):
        start_fetch(i, p, slot)
        wait_fetch(slot)
        return acc + gathered(p)

    y = lax.fori_loop(1, npass_ref[i], extra_pass, y)
    ms = jnp.mean(y * y, axis=-1, keepdims=True)
    o_ref[...] = y * lax.rsqrt(ms + NORM_EPS) * g_ref[...]


def _combine(base_tbl, npass, rows, idx_t, lr_t, gates_t, x1, g, tm):
    t, d = x1.shape
    kern = functools.partial(_combine_kernel, tm=tm)
    tok = lambda i, *_: (i, 0)
    return pl.pallas_call(
        kern,
        out_shape=jax.ShapeDtypeStruct((t, d), F32),
        grid_spec=pltpu.PrefetchScalarGridSpec(
            num_scalar_prefetch=2,
            grid=(t // tm,),
            in_specs=[pl.BlockSpec(memory_space=pl.ANY),
                      pl.BlockSpec((tm, TOP_K), tok), pl.BlockSpec((tm, TOP_K), tok),
                      pl.BlockSpec((tm, TOP_K), tok), pl.BlockSpec((tm, d), tok),
                      pl.BlockSpec((1, d), lambda i, *_: (0, 0))],
            out_specs=pl.BlockSpec((tm, d), tok),
            scratch_shapes=[pltpu.VMEM((2, N_EXPERTS * COMBINE_WINDOW, d), F32),
                            pltpu.SemaphoreType.DMA((2,))],
        ),
        compiler_params=_params(("arbitrary",)),
        name="combine",
    )(base_tbl, npass, rows, idx_t, lr_t, gates_t, x1, g)


def _rope_tables(seq):
    inv = 1.0 / (ROPE_THETA ** (jnp.arange(0, HEAD_DIM, 2, dtype=F32) / HEAD_DIM))
    ang = jnp.arange(seq, dtype=F32)[:, None] * inv[None, :]
    cos, sin = jnp.cos(ang), jnp.sin(ang)
    return jnp.concatenate([cos] * 4, axis=1), jnp.concatenate([-sin, sin, -sin, sin], axis=1)


def kernel(x, norm_mix_g, w_in, lambda_q1, lambda_k1, lambda_q2, lambda_k2, da_subln_g, w_da_out, w_sb_out, w_o, norm_ffn_g, w_router, b_router, w_gate, b_gate, w_up, b_up, w_down, b_down, norm_final_g):
    b, s, d = x.shape
    depth = w_in.shape[0]
    t = b * s
    cos_t, sin_t = _rope_tables(s)
    n_blocks = (t * TOP_K + ROW_BLOCK - 1) // ROW_BLOCK + N_EXPERTS
    p_rows = n_blocks * ROW_BLOCK
    x2 = x.reshape(t, d)
    for l in range(depth):
        lambda_init = 0.8 - 0.6 * math.exp(-0.3 * l)
        qa, ka, va, qb, kb, vb, ga, gb = _in_proj(
            x2, norm_mix_g[l][None, :], w_in[l].astype(BF16), cos_t, sin_t, s)
        lam_p = jnp.stack([lambda_q1[l], lambda_k1[l], lambda_q2[l], lambda_k2[l]]).astype(F32)
        seq3 = lambda a: a.reshape(b, s, a.shape[1])
        oa = _diff_attn(seq3(qa), seq3(ka), seq3(va), lam_p, da_subln_g[l][None, :].astype(F32), lambda_init)
        ob = _sb_attn(seq3(qb), seq3(kb), seq3(vb))
        wr = w_router[l].T.astype(F32)
        wr_hi = wr.astype(BF16)
        wr_lo = (wr - wr_hi.astype(F32)).astype(BF16)
        x1, h, idx, gates, rank, cnt = _post_attn(
            oa.reshape(t, -1), ob.reshape(t, -1), ga, gb, x2,
            w_da_out[l].astype(BF16), w_sb_out[l].astype(BF16), w_o[l].astype(BF16),
            norm_ffn_g[l][None, :], wr_hi, wr_lo, b_router[l][:, None].astype(F32))
        counts = cnt[:, 0].astype(jnp.int32)
        padded = (counts + ROW_BLOCK - 1) // ROW_BLOCK * ROW_BLOCK
        pad_ends = jnp.cumsum(padded)
        pad_starts = pad_ends - padded
        experts = jnp.arange(N_EXPERTS, dtype=jnp.int32)
        chosen = idx[:, :, None] == experts
        base = jnp.sum(jnp.where(chosen, pad_starts, 0), axis=-1)
        dest = (base + rank).reshape(-1)
        n_used = (pad_ends[-1] // ROW_BLOCK).astype(jnp.int32)
        blk = jnp.minimum(jnp.arange(n_blocks + 1, dtype=jnp.int32), n_used - 1) * ROW_BLOCK
        blk_e = jnp.minimum(jnp.sum(pad_ends[None, :] <= blk[:, None], axis=1), N_EXPERTS - 1).astype(jnp.int32)
        zero_blk = jnp.maximum(pad_ends // ROW_BLOCK - 1, 0).astype(jnp.int32)
        n_tiles = t // COMBINE_TILE
        tile_cnt = jnp.sum(chosen.reshape(TOP_K, n_tiles, COMBINE_TILE, N_EXPERTS), axis=(0, 2), dtype=jnp.int32)
        tile_carry = jnp.cumsum(tile_cnt, axis=0) - tile_cnt
        run_start = pad_starts[None, :] + tile_carry
        run_skew = run_start % SUBLANES
        run_base = (run_start - run_skew).reshape(-1).astype(jnp.int32)
        shift_tok = jnp.repeat(tile_carry - run_skew, COMBINE_TILE, axis=0)
        local_rank = rank - jnp.sum(jnp.where(chosen, shift_tok[None], 0), axis=-1)
        n_pass = jnp.maximum((jnp.max(tile_cnt + run_skew, axis=1) + COMBINE_WINDOW - 1) // COMBINE_WINDOW,
                             1).astype(jnp.int32)
        xs = _dispatch(dest, zero_blk, n_used[None], h, p_rows)
        rows = _experts(blk_e, n_used[None], xs, w_gate[l], b_gate[l][:, None, :], w_up[l], b_up[l][:, None, :],
                        w_down[l], b_down[l][:, None, :])
        g_next = norm_final_g[None, :] if l == depth - 1 else jnp.ones((1, d), F32)
        x2 = _combine(run_base, n_pass, rows, idx.T, local_rank.T.astype(jnp.int32), gates.T, x1, g_next,
                      COMBINE_TILE)
        if l != depth - 1:
            raise NotImplementedError("only the final layer's norm is fused into the combine kernel")
    return x2.reshape(b, s, d)
```

```python
import functools
import math

import jax
import jax.numpy as jnp
from jax import lax
from jax.experimental import pallas as pl
from jax.experimental.pallas import tpu as pltpu

F32 = jnp.float32
BF16 = jnp.bfloat16

DA_HEADS = 4
HEAD_DIM = 64
N_EXPERTS = 32
TOP_K = 4
ROPE_THETA = 10000.0
SWIGLU_LIMIT = 7.0
SWIGLU_ALPHA = 1.702
NORM_EPS = 1e-5
ROW_BLOCK = 256
COMBINE_TILE = 256
SUBLANES = 8
COMBINE_WINDOW = 72
LANES = 128
NEG_BIG = -1e30
LOG2E = math.log2(math.e)
SB_DEAD_LOG2 = -150.0

VMEM_LIMIT = 56 * 1024 * 1024


def _params(sem, vmem=VMEM_LIMIT):
    return pltpu.CompilerParams(dimension_semantics=sem, vmem_limit_bytes=vmem)


def _in_proj_kernel(x_ref, g_ref, w_ref, cos_ref, sin_ref,
                    qa_ref, ka_ref, va_ref, qb_ref, kb_ref, vb_ref, ga_ref, gb_ref):
    x = x_ref[...]
    ms = jnp.mean(x * x, axis=-1, keepdims=True)
    h = (x * lax.rsqrt(ms + NORM_EPS) * g_ref[...]).astype(BF16)
    cos = cos_ref[...]
    sin = sin_ref[...]
    lane = lax.broadcasted_iota(jnp.int32, cos.shape, 1)
    first_half = (lane & (HEAD_DIM - 1)) < HEAD_DIM // 2

    def proj(c0, width):
        return jnp.dot(h, w_ref[:, c0:c0 + width], preferred_element_type=F32)

    def rope(r):
        outs = []
        for g in range(r.shape[1] // LANES):
            xg = r[:, g * LANES:(g + 1) * LANES]
            rot = jnp.where(first_half, pltpu.roll(xg, LANES - HEAD_DIM // 2, 1),
                            pltpu.roll(xg, HEAD_DIM // 2, 1))
            outs.append(xg * cos + rot * sin)
        return jnp.concatenate(outs, axis=1)

    scale = HEAD_DIM ** -0.5 * LOG2E
    w = qa_ref.shape[1]
    d = ga_ref.shape[1]
    qa_ref[...] = (rope(proj(0, w)) * scale).astype(BF16)
    ka_ref[...] = rope(proj(w, w)).astype(BF16)
    va_ref[...] = proj(2 * w, w).astype(BF16)
    qb_ref[...] = (proj(3 * w, w) * scale).astype(BF16)
    kb_ref[...] = proj(4 * w, w).astype(BF16)
    vb_ref[...] = proj(5 * w, w).astype(BF16)
    ga_ref[...] = jax.nn.sigmoid(proj(6 * w, d)).astype(BF16)
    gb_ref[...] = jax.nn.sigmoid(proj(6 * w + d, d)).astype(BF16)


def _in_proj(x2, g, w_in_bf, cos_t, sin_t, seq, tm=512):
    t, d = x2.shape
    w = 512
    nseq = seq // tm
    outs = [jax.ShapeDtypeStruct((t, w), BF16)] * 6 + [jax.ShapeDtypeStruct((t, d), BF16)] * 2
    row = lambda i: (i, 0)
    return pl.pallas_call(
        _in_proj_kernel,
        out_shape=outs,
        grid=(t // tm,),
        in_specs=[
            pl.BlockSpec((tm, d), row),
            pl.BlockSpec((1, d), lambda i: (0, 0)),
            pl.BlockSpec(w_in_bf.shape, lambda i: (0, 0)),
            pl.BlockSpec((tm, LANES), lambda i: (i % nseq, 0)),
            pl.BlockSpec((tm, LANES), lambda i: (i % nseq, 0)),
        ],
        out_specs=[pl.BlockSpec((tm, w), row)] * 6 + [pl.BlockSpec((tm, d), row)] * 2,
        compiler_params=_params(("arbitrary",)),
        name="in_proj",
    )(x2, g, w_in_bf, cos_t, sin_t)


def _lane_groups(x):
    return [x[:, g * LANES:(g + 1) * LANES] for g in range(x.shape[1] // LANES)]


def _da_kernel(q_ref, k_ref, v_ref, lam_ref, subg_ref, o_ref, s_buf, mx_ref, m_ref, acc_ref,
               *, tq, tk, lambda_init):
    i = pl.program_id(2)
    q = q_ref[0]
    lane = lax.broadcasted_iota(jnp.int32, q.shape, 1)
    zero = jnp.zeros_like(q)
    qs = (jnp.where(lane < HEAD_DIM, q, zero), jnp.where(lane >= HEAD_DIM, q, zero))
    mx_ref[...] = jnp.full(mx_ref.shape, NEG_BIG, F32)
    acc_ref[...] = jnp.zeros(acc_ref.shape, F32)

    def score_blocks(js, diag_last):
        maxes = [mx_ref[0], mx_ref[1]]
        for n, j in enumerate(js):
            k = k_ref[0, pl.ds(pl.multiple_of(j * tk, tk), tk), :]
            for h in range(2):
                s = lax.dot_general(qs[h], k, (((1,), (1,)), ((), ())), preferred_element_type=F32)
                if diag_last and n == len(js) - 1:
                    rows = lax.broadcasted_iota(jnp.int32, s.shape, 0)
                    cols = lax.broadcasted_iota(jnp.int32, s.shape, 1)
                    s = jnp.where(rows >= cols, s, NEG_BIG)
                s_buf[h, j] = s
                for sg in _lane_groups(s):
                    maxes[h] = jnp.maximum(maxes[h], sg)
        mx_ref[0], mx_ref[1] = maxes

    def score_body(jj, carry):
        score_blocks([2 * jj, 2 * jj + 1], False)
        return carry

    lax.fori_loop(0, i // 2, score_body, 0)

    @pl.when(i % 2 == 0)
    def _():
        score_blocks([i], True)

    @pl.when(i % 2 == 1)
    def _():
        score_blocks([i - 1, i], True)

    for h in range(2):
        m_ref[h] = jnp.broadcast_to(jnp.max(mx_ref[h], axis=1, keepdims=True), (tq, LANES))

    def pv_blocks(js):
        vs = [v_ref[0, pl.ds(pl.multiple_of(j * tk, tk), tk), :] for j in js]
        v_all = jnp.concatenate(vs, axis=0)
        v_ext = jnp.concatenate([v_all, jnp.ones_like(v_all)], axis=1)
        accs = [acc_ref[0], acc_ref[1]]
        for h in range(2):
            m = m_ref[h]
            p = jnp.concatenate([jnp.exp2(sg - m).astype(BF16)
                                 for j in js for sg in _lane_groups(s_buf[h, j])], axis=1)
            accs[h] = accs[h] + jnp.dot(p, v_ext, preferred_element_type=F32)
        acc_ref[0], acc_ref[1] = accs

    def pv_body(jj, carry):
        pv_blocks([2 * jj, 2 * jj + 1])
        return carry

    lax.fori_loop(0, (i + 1) // 2, pv_body, 0)

    @pl.when(i % 2 == 0)
    def _():
        pv_blocks([i])

    lam_p = lam_ref[...]
    lam = (jnp.exp(jnp.sum(lam_p[0:1] * lam_p[1:2], axis=1, keepdims=True))
           - jnp.exp(jnp.sum(lam_p[2:3] * lam_p[3:4], axis=1, keepdims=True)) + lambda_init)
    a0, a1 = acc_ref[0], acc_ref[1]
    o = a0[:, :LANES] / a0[:, LANES:] - lam * (a1[:, :LANES] / a1[:, LANES:])
    ms = jnp.mean(o * o, axis=-1, keepdims=True)
    y = o * lax.rsqrt(ms + NORM_EPS) * subg_ref[...] * (1.0 - lambda_init)
    o_ref[0] = y.astype(o_ref.dtype)


def _diff_attn(qa, ka, va, lam_p, subg, lambda_init, tq=512):
    b, s, w = qa.shape
    nh = w // LANES
    kern = functools.partial(_da_kernel, tq=tq, tk=tq, lambda_init=lambda_init)
    qspec = pl.BlockSpec((1, tq, LANES), lambda bb, h, i: (bb, i, h))
    kvspec = pl.BlockSpec((1, s, LANES), lambda bb, h, i: (bb, 0, h))
    return pl.pallas_call(
        kern,
        out_shape=jax.ShapeDtypeStruct((b, s, w), BF16),
        grid=(b, nh, s // tq),
        in_specs=[qspec, kvspec, kvspec,
                  pl.BlockSpec(lam_p.shape, lambda bb, h, i: (0, 0)),
                  pl.BlockSpec(subg.shape, lambda bb, h, i: (0, 0))],
        out_specs=qspec,
        scratch_shapes=[pltpu.VMEM((2, s // tq, tq, tq), F32),
                        pltpu.VMEM((2, tq, LANES), F32),
                        pltpu.VMEM((2, tq, LANES), F32),
                        pltpu.VMEM((2, tq, 2 * LANES), F32)],
        compiler_params=_params(("arbitrary",) * 3),
        name="diff_attn",
    )(qa, ka, va, lam_p, subg)


def _sb_kernel(q_ref, k_ref, v_ref, tri_ref, o_ref, c_ref, acc_ref, *, tq, tk):
    i = pl.program_id(2)
    tri = tri_ref[...]
    lane = lax.broadcasted_iota(jnp.int32, (tk, LANES), 1)

    def row_tile(r):
        q = q_ref[0, r * tk:(r + 1) * tk, :]
        zero = jnp.zeros_like(q)
        qs = (jnp.where(lane < HEAD_DIM, q, zero), jnp.where(lane >= HEAD_DIM, q, zero))
        c_ref[...] = jnp.zeros(c_ref.shape, F32)
        acc_ref[...] = jnp.zeros(acc_ref.shape, F32)

        def window(js, diag_first):
            ks = [k_ref[0, pl.ds(pl.multiple_of(j * tk, tk), tk), :] for j in js]
            vs = [v_ref[0, pl.ds(pl.multiple_of(j * tk, tk), tk), :] for j in js]
            v_all = jnp.concatenate(vs, axis=0)
            state = [(c_ref[h], acc_ref[h]) for h in range(2)]
            new_state = []
            for h in range(2):
                c_run, acc_prev = state[h]
                weights = []
                for n, k in enumerate(ks):
                    masked = diag_first and n == 0
                    z = lax.dot_general(qs[h], k, (((1,), (1,)), ((), ())), preferred_element_type=F32)
                    nz = -z
                    w = jnp.log2(1.0 + jnp.exp2(jnp.minimum(z, nz)))
                    log_1m = jnp.minimum(nz, 0.0) - w
                    log_b = jnp.minimum(z, 0.0) - w
                    if masked:
                        mask = (lax.broadcasted_iota(jnp.int32, z.shape, 1)
                                < lax.broadcasted_iota(jnp.int32, z.shape, 0))
                        log_1m = jnp.where(mask, log_1m, 0.0)
                    res = jnp.dot(log_1m.astype(BF16), tri, preferred_element_type=F32)
                    suffix, total = res[:, :tk], res[:, tk:]
                    a = jnp.concatenate([jnp.exp2(lb + sf + c_run)
                                         for lb, sf in zip(_lane_groups(log_b), _lane_groups(suffix))], axis=1)
                    if masked:
                        a = jnp.where(mask, a, 0.0)
                    weights.append(a.astype(BF16))
                    c_run = c_run + total
                a_all = jnp.concatenate(weights, axis=1)
                new_state.append((c_run, acc_prev + jnp.dot(a_all, v_all, preferred_element_type=F32)))
            for h in range(2):
                c_ref[h], acc_ref[h] = new_state[h]

        jd = i * (tq // tk) + r
        if r == 0:
            @pl.when(i == 0)
            def _():
                window([jd], True)

            @pl.when(i > 0)
            def _():
                window([jd, jd - 1], True)
        else:
            window([jd, jd - 1], True)

        def cond(carry):
            j, live = carry
            return jnp.logical_and(j >= 0, live)

        def live():
            return jnp.max(jnp.maximum(c_ref[0], c_ref[1])) > SB_DEAD_LOG2

        def body(carry):
            j, _ = carry
            window([j], False)
            return j - 1, live()

        lax.while_loop(cond, body, (jd - 2, live()))
        o_ref[0, r * tk:(r + 1) * tk, :] = jnp.where(lane < HEAD_DIM, acc_ref[0], acc_ref[1]).astype(o_ref.dtype)

    for r in range(tq // tk):
        row_tile(r)


def _sb_attn(qb, kb, vb, tq=512, tk=256):
    b, s, w = qb.shape
    nh = w // LANES
    r = lax.broadcasted_iota(jnp.int32, (tk, tk), 0)
    c = lax.broadcasted_iota(jnp.int32, (tk, tk), 1)
    tri = jnp.concatenate([(r > c).astype(BF16), jnp.ones((tk, LANES), BF16)], axis=1)
    kern = functools.partial(_sb_kernel, tq=tq, tk=tk)
    qspec = pl.BlockSpec((1, tq, LANES), lambda bb, h, i: (bb, i, h))
    kvspec = pl.BlockSpec((1, s, LANES), lambda bb, h, i: (bb, 0, h))
    return pl.pallas_call(
        kern,
        out_shape=jax.ShapeDtypeStruct((b, s, w), BF16),
        grid=(b, nh, s // tq),
        in_specs=[qspec, kvspec, kvspec, pl.BlockSpec(tri.shape, lambda bb, h, i: (0, 0))],
        out_specs=qspec,
        scratch_shapes=[pltpu.VMEM((2, tk, LANES), F32), pltpu.VMEM((2, tk, LANES), F32)],
        compiler_params=_params(("arbitrary",) * 3),
        name="sb_attn",
    )(qb, kb, vb, tri)


def _post_attn_kernel(oa_ref, ob_ref, ga_ref, gb_ref, x_ref, wda_ref, wsb_ref, wo_ref, g_ref,
                      wr_hi_ref, wr_lo_ref, br_ref, tri_ref,
                      x1_ref, h_ref, idx_ref, gate_ref, rank_ref, cnt_ref, carry_ref):
    i = pl.program_id(0)

    @pl.when(i == 0)
    def _():
        carry_ref[...] = jnp.zeros(carry_ref.shape, F32)

    ya = jnp.dot(oa_ref[...], wda_ref[...], preferred_element_type=F32)
    yb = jnp.dot(ob_ref[...], wsb_ref[...], preferred_element_type=F32)
    mix = ga_ref[...].astype(F32) * ya + gb_ref[...].astype(F32) * yb
    x1 = x_ref[...] + jnp.dot(mix.astype(BF16), wo_ref[...], preferred_element_type=F32)
    x1_ref[...] = x1
    ms = jnp.mean(x1 * x1, axis=-1, keepdims=True)
    h = x1 * lax.rsqrt(ms + NORM_EPS) * g_ref[...]
    h_ref[...] = h

    h_hi = h.astype(BF16)
    h_lo = (h - h_hi.astype(F32)).astype(BF16)
    nt = (((1,), (1,)), ((), ()))
    logits = (lax.dot_general(wr_hi_ref[...], h_hi, nt, preferred_element_type=F32)
              + lax.dot_general(wr_hi_ref[...], h_lo, nt, preferred_element_type=F32)
              + lax.dot_general(wr_lo_ref[...], h_hi, nt, preferred_element_type=F32)
              + br_ref[...])
    ne, tm = logits.shape
    eid = lax.broadcasted_iota(jnp.int32, (ne, tm), 0).astype(F32)
    vals, ids = [], []
    work = logits
    for _ in range(TOP_K):
        mx = jnp.max(work, axis=0, keepdims=True)
        sel = jnp.min(jnp.where(work == mx, eid, float(ne)), axis=0, keepdims=True)
        vals.append(mx)
        ids.append(sel)
        work = jnp.where(eid == sel, -jnp.inf, work)
    exps = [jnp.exp(v - vals[0]) for v in vals]
    denom = exps[0] + exps[1] + exps[2] + exps[3]
    onehots = [(eid == sel).astype(F32) for sel in ids]
    assigned = onehots[0] + onehots[1] + onehots[2] + onehots[3]
    before = jnp.dot(assigned.astype(BF16), tri_ref[...], preferred_element_type=F32) + carry_ref[...]
    for r in range(TOP_K):
        idx_ref[r:r + 1, :] = ids[r].astype(jnp.int32)
        gate_ref[r:r + 1, :] = exps[r] / denom
        rank_ref[r:r + 1, :] = jnp.sum(onehots[r] * before, axis=0, keepdims=True).astype(jnp.int32)
    carry = carry_ref[...] + jnp.sum(assigned, axis=1, keepdims=True)
    carry_ref[...] = carry
    cnt_ref[...] = jnp.broadcast_to(carry, cnt_ref.shape)


def _post_attn(oa, ob, ga, gb, x2, wda, wsb, wo, g, wr_hi, wr_lo, br, tm=512):
    t, d = x2.shape
    w = oa.shape[1]
    r = lax.broadcasted_iota(jnp.int32, (tm, tm), 0)
    c = lax.broadcasted_iota(jnp.int32, (tm, tm), 1)
    tri = (r < c).astype(BF16)
    row = lambda i: (i, 0)
    col = lambda i: (0, i)
    const = lambda i: (0, 0)
    full = lambda a: pl.BlockSpec(a.shape, const)
    return pl.pallas_call(
        _post_attn_kernel,
        out_shape=[jax.ShapeDtypeStruct((t, d), F32), jax.ShapeDtypeStruct((t, d), F32),
                   jax.ShapeDtypeStruct((TOP_K, t), jnp.int32), jax.ShapeDtypeStruct((TOP_K, t), F32),
                   jax.ShapeDtypeStruct((TOP_K, t), jnp.int32),
                   jax.ShapeDtypeStruct((N_EXPERTS, LANES), F32)],
        grid=(t // tm,),
        in_specs=[pl.BlockSpec((tm, w), row), pl.BlockSpec((tm, w), row),
                  pl.BlockSpec((tm, d), row), pl.BlockSpec((tm, d), row), pl.BlockSpec((tm, d), row),
                  full(wda), full(wsb), full(wo), full(g), full(wr_hi), full(wr_lo), full(br), full(tri)],
        out_specs=[pl.BlockSpec((tm, d), row), pl.BlockSpec((tm, d), row),
                   pl.BlockSpec((TOP_K, tm), col), pl.BlockSpec((TOP_K, tm), col),
                   pl.BlockSpec((TOP_K, tm), col), pl.BlockSpec((N_EXPERTS, LANES), const)],
        scratch_shapes=[pltpu.VMEM((N_EXPERTS, 1), F32)],
        compiler_params=_params(("arbitrary",)),
        name="post_attn",
    )(oa, ob, ga, gb, x2, wda, wsb, wo, g, wr_hi, wr_lo, br, tri)


def _dispatch_kernel(dest_ref, zero_blk_ref, n_used_ref, h_ref, xs_ref, zeros, sem, zsem, *, tm, t_total):
    i = pl.program_id(0)
    n_blocks = xs_ref.shape[0] // ROW_BLOCK

    def zero_copy(blk):
        row = pl.multiple_of(blk * ROW_BLOCK, ROW_BLOCK)
        return pltpu.make_async_copy(zeros, xs_ref.at[pl.ds(row, ROW_BLOCK), :], zsem)

    @pl.when(i == 0)
    def _():
        zeros[...] = jnp.zeros(zeros.shape, zeros.dtype)
        n_tail = n_blocks - n_used_ref[0]

        def start(n, carry):
            zero_copy(jnp.where(n < N_EXPERTS, zero_blk_ref[jnp.minimum(n, N_EXPERTS - 1)],
                                n_used_ref[0] + n - N_EXPERTS)).start()
            return carry

        def wait(n, carry):
            zero_copy(0).wait()
            return carry

        lax.fori_loop(0, N_EXPERTS + n_tail, start, 0)
        lax.fori_loop(0, N_EXPERTS + n_tail, wait, 0)

    def row_copy(t, k):
        dst = dest_ref[k * t_total + i * tm + t]
        return pltpu.make_async_copy(h_ref.at[pl.ds(t, 1), :], xs_ref.at[pl.ds(dst, 1), :], sem)

    def issue(t, carry):
        for k in range(TOP_K):
            row_copy(t, k).start(priority=k % 2)
        return carry

    lax.fori_loop(0, tm, issue, 0, unroll=8)
    for _ in range(TOP_K):
        pltpu.make_async_copy(h_ref, xs_ref.at[pl.ds(0, tm), :], sem).wait()


def _dispatch(dest_flat, zero_blk, n_used, h, p_rows, tm=256):
    t, d = h.shape
    kern = functools.partial(_dispatch_kernel, tm=tm, t_total=t)
    return pl.pallas_call(
        kern,
        out_shape=jax.ShapeDtypeStruct((p_rows, d), h.dtype),
        grid_spec=pltpu.PrefetchScalarGridSpec(
            num_scalar_prefetch=3,
            grid=(t // tm,),
            in_specs=[pl.BlockSpec((tm, d), lambda i, *_: (i, 0))],
            out_specs=pl.BlockSpec(memory_space=pl.ANY),
            scratch_shapes=[pltpu.VMEM((ROW_BLOCK, d), h.dtype), pltpu.SemaphoreType.DMA,
                            pltpu.SemaphoreType.DMA],
        ),
        compiler_params=_params(("arbitrary",)),
        name="dispatch",
    )(dest_flat, zero_blk, n_used, h)


def _expert_kernel(first_ref, count_ref, n_used_ref, xs_ref, wg_ref, bg_ref, wu_ref, bu_ref, wd_ref, bd_ref,
                   o_ref, wg_bf, wu_bf, wd_bf, xbuf, obuf, in_sem, out_sem):
    e = pl.program_id(0)
    first = first_ref[e]
    count = count_ref[e]

    def rows(blk):
        return pl.ds(pl.multiple_of(blk * ROW_BLOCK, ROW_BLOCK), ROW_BLOCK)

    def fetch(blk, slot):
        return pltpu.make_async_copy(xs_ref.at[rows(blk), :], xbuf.at[slot], in_sem.at[slot])

    def flush(blk, slot):
        return pltpu.make_async_copy(obuf.at[slot], o_ref.at[rows(blk), :], out_sem.at[slot])

    @pl.when(count > 0)
    def _():
        fetch(first, 0).start()
        wg_bf[...] = wg_ref[0].astype(BF16)
        wu_bf[...] = wu_ref[0].astype(BF16)
        wd_bf[...] = wd_ref[0].astype(BF16)

    def block(j, carry):
        slot = j % 2
        fetch(first + j, slot).wait()

        @pl.when(j + 1 < count)
        def _():
            fetch(first + j + 1, 1 - slot).start()

        x = xbuf[slot].astype(BF16)
        g = jnp.dot(x, wg_bf[...], preferred_element_type=F32) + bg_ref[0]
        u = jnp.dot(x, wu_bf[...], preferred_element_type=F32) + bu_ref[0]
        g = jnp.minimum(g, SWIGLU_LIMIT)
        u = jnp.clip(u, -SWIGLU_LIMIT, SWIGLU_LIMIT)
        glu = g * jax.nn.sigmoid(SWIGLU_ALPHA * g)
        act = ((u + 1.0) * glu).astype(BF16)
        y = jnp.dot(act, wd_bf[...], preferred_element_type=F32) + bd_ref[0]

        @pl.when(j >= 2)
        def _():
            flush(first + j - 2, slot).wait()

        obuf[slot] = y
        flush(first + j, slot).start()
        return carry

    lax.fori_loop(0, count, block, 0)

    @pl.when(count >= 2)
    def _():
        flush(first + count - 2, count % 2).wait()

    @pl.when(count >= 1)
    def _():
        flush(first + count - 1, (count - 1) % 2).wait()

    @pl.when(e == pl.num_programs(0) - 1)
    def _():
        obuf[0] = jnp.zeros(obuf.shape[1:], obuf.dtype)
        n_used = n_used_ref[0]
        n_total = o_ref.shape[0] // ROW_BLOCK

        def start(blk, carry):
            flush(blk, 0).start()
            return carry

        def wait(blk, carry):
            flush(blk, 0).wait()
            return carry

        lax.fori_loop(n_used, n_total, start, 0)
        lax.fori_loop(n_used, n_total, wait, 0)


def _experts(first_blk, blk_count, n_used, n_out_blocks, xs, wg, bg, wu, bu, wd, bd):
    d = xs.shape[1]
    n_exp, _, f = wg.shape
    wmap = lambda e, *_: (e, 0, 0)
    return pl.pallas_call(
        _expert_kernel,
        out_shape=jax.ShapeDtypeStruct((n_out_blocks * ROW_BLOCK, d), F32),
        grid_spec=pltpu.PrefetchScalarGridSpec(
            num_scalar_prefetch=3,
            grid=(n_exp,),
            in_specs=[pl.BlockSpec(memory_space=pl.ANY),
                      pl.BlockSpec((1, d, f), wmap), pl.BlockSpec((1, 1, f), wmap),
                      pl.BlockSpec((1, d, f), wmap), pl.BlockSpec((1, 1, f), wmap),
                      pl.BlockSpec((1, f, d), wmap), pl.BlockSpec((1, 1, d), wmap)],
            out_specs=pl.BlockSpec(memory_space=pl.ANY),
            scratch_shapes=[pltpu.VMEM((d, f), BF16), pltpu.VMEM((d, f), BF16), pltpu.VMEM((f, d), BF16),
                            pltpu.VMEM((2, ROW_BLOCK, d), F32), pltpu.VMEM((2, ROW_BLOCK, d), F32),
                            pltpu.SemaphoreType.DMA((2,)), pltpu.SemaphoreType.DMA((2,))],
        ),
        compiler_params=_params(("arbitrary",)),
        name="experts",
    )(first_blk, blk_count, n_used, xs, wg, bg, wu, bu, wd, bd)


def _combine_kernel(base_ref, npass_ref, rows_ref, idx_ref, lr_ref, gate_ref, x1_ref, g_ref, o_ref,
                    wbuf, sem, *, tm):
    i = pl.program_id(0)
    slot = i % 2
    win = COMBINE_WINDOW
    width = N_EXPERTS * win

    def start_fetch(tile, p, s):
        for e in range(N_EXPERTS):
            start = pl.multiple_of(base_ref[tile * N_EXPERTS + e] + p * win, SUBLANES)
            pltpu.make_async_copy(rows_ref.at[pl.ds(start, win), :],
                                  wbuf.at[s, pl.ds(e * win, win), :], sem.at[s]).start()

    def wait_fetch(s):
        pltpu.make_async_copy(rows_ref.at[pl.ds(0, width), :], wbuf.at[s], sem.at[s]).wait()

    @pl.when(i == 0)
    def _():
        start_fetch(0, 0, 0)

    @pl.when(i + 1 < pl.num_programs(0))
    def _():
        start_fetch(i + 1, 0, 1 - slot)

    idx = idx_ref[...]
    lr = lr_ref[...]
    gates = gate_ref[...]
    col = lax.broadcasted_iota(jnp.int32, (tm, width), 1)

    def weights(p):
        w = jnp.zeros((tm, width), F32)
        for k in range(TOP_K):
            r = lr[:, k:k + 1] - p * win
            tgt = jnp.where((r >= 0) & (r < win), idx[:, k:k + 1] * win + r, -1)
            w = jnp.where(col == tgt, gates[:, k:k + 1], w)
        return w.astype(BF16)

    def gathered(p):
        return jnp.dot(weights(p), wbuf[slot].astype(BF16), preferred_element_type=F32)

    wait_fetch(slot)
    y = x1_ref[...] + gathered(0)

    def extra_pass(p, acc):
        start_fetch(i, p, slot)
        wait_fetch(slot)
        return acc + gathered(p)

    y = lax.fori_loop(1, npass_ref[i], extra_pass, y)
    ms = jnp.mean(y * y, axis=-1, keepdims=True)
    o_ref[...] = y * lax.rsqrt(ms + NORM_EPS) * g_ref[...]


def _combine(base_tbl, npass, rows, idx_t, lr_t, gates_t, x1, g, tm):
    t, d = x1.shape
    kern = functools.partial(_combine_kernel, tm=tm)
    tok = lambda i, *_: (i, 0)
    return pl.pallas_call(
        kern,
        out_shape=jax.ShapeDtypeStruct((t, d), F32),
        grid_spec=pltpu.PrefetchScalarGridSpec(
            num_scalar_prefetch=2,
            grid=(t // tm,),
            in_specs=[pl.BlockSpec(memory_space=pl.ANY),
                      pl.BlockSpec((tm, TOP_K), tok), pl.BlockSpec((tm, TOP_K), tok),
                      pl.BlockSpec((tm, TOP_K), tok), pl.BlockSpec((tm, d), tok),
                      pl.BlockSpec((1, d), lambda i, *_: (0, 0))],
            out_specs=pl.BlockSpec((tm, d), tok),
            scratch_shapes=[pltpu.VMEM((2, N_EXPERTS * COMBINE_WINDOW, d), F32),
                            pltpu.SemaphoreType.DMA((2,))],
        ),
        compiler_params=_params(("arbitrary",)),
        name="combine",
    )(base_tbl, npass, rows, idx_t, lr_t, gates_t, x1, g)


def _rope_tables(seq):
    inv = 1.0 / (ROPE_THETA ** (jnp.arange(0, HEAD_DIM, 2, dtype=F32) / HEAD_DIM))
    ang = jnp.arange(seq, dtype=F32)[:, None] * inv[None, :]
    cos, sin = jnp.cos(ang), jnp.sin(ang)
    return jnp.concatenate([cos] * 4, axis=1), jnp.concatenate([-sin, sin, -sin, sin], axis=1)


def kernel(x, norm_mix_g, w_in, lambda_q1, lambda_k1, lambda_q2, lambda_k2, da_subln_g, w_da_out, w_sb_out, w_o, norm_ffn_g, w_router, b_router, w_gate, b_gate, w_up, b_up, w_down, b_down, norm_final_g):
    b, s, d = x.shape
    depth = w_in.shape[0]
    t = b * s
    cos_t, sin_t = _rope_tables(s)
    n_blocks = (t * TOP_K + ROW_BLOCK - 1) // ROW_BLOCK + N_EXPERTS
    p_rows = n_blocks * ROW_BLOCK
    x2 = x.reshape(t, d)
    for l in range(depth):
        lambda_init = 0.8 - 0.6 * math.exp(-0.3 * l)
        qa, ka, va, qb, kb, vb, ga, gb = _in_proj(
            x2, norm_mix_g[l][None, :], w_in[l].astype(BF16), cos_t, sin_t, s)
        lam_p = jnp.stack([lambda_q1[l], lambda_k1[l], lambda_q2[l], lambda_k2[l]]).astype(F32)
        seq3 = lambda a: a.reshape(b, s, a.shape[1])
        oa = _diff_attn(seq3(qa), seq3(ka), seq3(va), lam_p, da_subln_g[l][None, :].astype(F32), lambda_init)
        ob = _sb_attn(seq3(qb), seq3(kb), seq3(vb))
        wr = w_router[l].T.astype(F32)
        wr_hi = wr.astype(BF16)
        wr_lo = (wr - wr_hi.astype(F32)).astype(BF16)
        x1, h, idx, gates, rank, cnt = _post_attn(
            oa.reshape(t, -1), ob.reshape(t, -1), ga, gb, x2,
            w_da_out[l].astype(BF16), w_sb_out[l].astype(BF16), w_o[l].astype(BF16),
            norm_ffn_g[l][None, :], wr_hi, wr_lo, b_router[l][:, None].astype(F32))
        counts = cnt[:, 0].astype(jnp.int32)
        padded = (counts + ROW_BLOCK - 1) // ROW_BLOCK * ROW_BLOCK
        pad_ends = jnp.cumsum(padded)
        pad_starts = pad_ends - padded
        experts = jnp.arange(N_EXPERTS, dtype=jnp.int32)
        chosen = idx[:, :, None] == experts
        base = jnp.sum(jnp.where(chosen, pad_starts, 0), axis=-1)
        dest = (base + rank).reshape(-1)
        n_used = (pad_ends[-1] // ROW_BLOCK).astype(jnp.int32)
        first_blk = (pad_starts // ROW_BLOCK).astype(jnp.int32)
        blk_count = (padded // ROW_BLOCK).astype(jnp.int32)
        zero_blk = jnp.maximum(pad_ends // ROW_BLOCK - 1, 0).astype(jnp.int32)
        n_tiles = t // COMBINE_TILE
        tile_cnt = jnp.sum(chosen.reshape(TOP_K, n_tiles, COMBINE_TILE, N_EXPERTS), axis=(0, 2), dtype=jnp.int32)
        tile_carry = jnp.cumsum(tile_cnt, axis=0) - tile_cnt
        run_start = pad_starts[None, :] + tile_carry
        run_skew = run_start % SUBLANES
        run_base = (run_start - run_skew).reshape(-1).astype(jnp.int32)
        shift_tok = jnp.repeat(tile_carry - run_skew, COMBINE_TILE, axis=0)
        local_rank = rank - jnp.sum(jnp.where(chosen, shift_tok[None], 0), axis=-1)
        n_pass = jnp.maximum((jnp.max(tile_cnt + run_skew, axis=1) + COMBINE_WINDOW - 1) // COMBINE_WINDOW,
                             1).astype(jnp.int32)
        xs = _dispatch(dest, zero_blk, n_used[None], h, p_rows)
        rows = _experts(first_blk, blk_count, n_used[None], n_blocks + 1, xs, w_gate[l], b_gate[l][:, None, :],
                        w_up[l], b_up[l][:, None, :], w_down[l], b_down[l][:, None, :])
        g_next = norm_final_g[None, :] if l == depth - 1 else jnp.ones((1, d), F32)
        x2 = _combine(run_base, n_pass, rows, idx.T, local_rank.T.astype(jnp.int32), gates.T, x1, g_next,
                      COMBINE_TILE)
        if l != depth - 1:
            raise NotImplementedError("only the final layer's norm is fused into the combine kernel")
    return x2.reshape(b, s, d)
```

```python
import functools
import math

import jax
import jax.numpy as jnp
from jax import lax
from jax.experimental import pallas as pl
from jax.experimental.pallas import tpu as pltpu

F32 = jnp.float32
BF16 = jnp.bfloat16

DA_HEADS = 4
HEAD_DIM = 64
N_EXPERTS = 32
TOP_K = 4
ROPE_THETA = 10000.0
SWIGLU_LIMIT = 7.0
SWIGLU_ALPHA = 1.702
NORM_EPS = 1e-5
ROW_BLOCK = 256
COMBINE_TILE = 256
SUBLANES = 8
COMBINE_WINDOW = 72
LANES = 128
NEG_BIG = -1e30
LOG2E = math.log2(math.e)
SB_DEAD_LOG2 = -150.0

VMEM_LIMIT = 56 * 1024 * 1024


def _params(sem, vmem=VMEM_LIMIT):
    return pltpu.CompilerParams(dimension_semantics=sem, vmem_limit_bytes=vmem)


def _in_proj_kernel(x_ref, g_ref, w_ref, cos_ref, sin_ref,
                    qa_ref, ka_ref, va_ref, qb_ref, kb_ref, vb_ref, ga_ref, gb_ref):
    x = x_ref[...]
    ms = jnp.mean(x * x, axis=-1, keepdims=True)
    h = (x * lax.rsqrt(ms + NORM_EPS) * g_ref[...]).astype(BF16)
    cos = cos_ref[...]
    sin = sin_ref[...]
    lane = lax.broadcasted_iota(jnp.int32, cos.shape, 1)
    first_half = (lane & (HEAD_DIM - 1)) < HEAD_DIM // 2

    def proj(c0, width):
        return jnp.dot(h, w_ref[:, c0:c0 + width], preferred_element_type=F32)

    def rope(r):
        outs = []
        for g in range(r.shape[1] // LANES):
            xg = r[:, g * LANES:(g + 1) * LANES]
            rot = jnp.where(first_half, pltpu.roll(xg, LANES - HEAD_DIM // 2, 1),
                            pltpu.roll(xg, HEAD_DIM // 2, 1))
            outs.append(xg * cos + rot * sin)
        return jnp.concatenate(outs, axis=1)

    scale = HEAD_DIM ** -0.5 * LOG2E
    w = qa_ref.shape[1]
    d = ga_ref.shape[1]
    qa_ref[...] = (rope(proj(0, w)) * scale).astype(BF16)
    ka_ref[...] = rope(proj(w, w)).astype(BF16)
    va_ref[...] = proj(2 * w, w).astype(BF16)
    qb_ref[...] = (proj(3 * w, w) * scale).astype(BF16)
    kb_ref[...] = proj(4 * w, w).astype(BF16)
    vb_ref[...] = proj(5 * w, w).astype(BF16)
    ga_ref[...] = jax.nn.sigmoid(proj(6 * w, d)).astype(BF16)
    gb_ref[...] = jax.nn.sigmoid(proj(6 * w + d, d)).astype(BF16)


def _in_proj(x2, g, w_in_bf, cos_t, sin_t, seq, tm=512):
    t, d = x2.shape
    w = 512
    nseq = seq // tm
    outs = [jax.ShapeDtypeStruct((t, w), BF16)] * 6 + [jax.ShapeDtypeStruct((t, d), BF16)] * 2
    row = lambda i: (i, 0)
    return pl.pallas_call(
        _in_proj_kernel,
        out_shape=outs,
        grid=(t // tm,),
        in_specs=[
            pl.BlockSpec((tm, d), row),
            pl.BlockSpec((1, d), lambda i: (0, 0)),
            pl.BlockSpec(w_in_bf.shape, lambda i: (0, 0)),
            pl.BlockSpec((tm, LANES), lambda i: (i % nseq, 0)),
            pl.BlockSpec((tm, LANES), lambda i: (i % nseq, 0)),
        ],
        out_specs=[pl.BlockSpec((tm, w), row)] * 6 + [pl.BlockSpec((tm, d), row)] * 2,
        compiler_params=_params(("arbitrary",)),
        name="in_proj",
    )(x2, g, w_in_bf, cos_t, sin_t)


def _lane_groups(x):
    return [x[:, g * LANES:(g + 1) * LANES] for g in range(x.shape[1] // LANES)]


def _da_kernel(q_ref, k_ref, v_ref, lam_ref, subg_ref, o_ref, s_buf, mx_ref, m_ref, acc_ref,
               *, tq, tk, lambda_init):
    i = pl.program_id(2)
    q = q_ref[0]
    lane = lax.broadcasted_iota(jnp.int32, q.shape, 1)
    zero = jnp.zeros_like(q)
    qs = (jnp.where(lane < HEAD_DIM, q, zero), jnp.where(lane >= HEAD_DIM, q, zero))
    mx_ref[...] = jnp.full(mx_ref.shape, NEG_BIG, F32)
    acc_ref[...] = jnp.zeros(acc_ref.shape, F32)

    def score_blocks(js, diag_last):
        maxes = [mx_ref[0], mx_ref[1]]
        for n, j in enumerate(js):
            k = k_ref[0, pl.ds(pl.multiple_of(j * tk, tk), tk), :]
            for h in range(2):
                s = lax.dot_general(qs[h], k, (((1,), (1,)), ((), ())), preferred_element_type=F32)
                if diag_last and n == len(js) - 1:
                    rows = lax.broadcasted_iota(jnp.int32, s.shape, 0)
                    cols = lax.broadcasted_iota(jnp.int32, s.shape, 1)
                    s = jnp.where(rows >= cols, s, NEG_BIG)
                s_buf[h, j] = s
                for sg in _lane_groups(s):
                    maxes[h] = jnp.maximum(maxes[h], sg)
        mx_ref[0], mx_ref[1] = maxes

    def score_body(jj, carry):
        score_blocks([2 * jj, 2 * jj + 1], False)
        return carry

    lax.fori_loop(0, i // 2, score_body, 0)

    @pl.when(i % 2 == 0)
    def _():
        score_blocks([i], True)

    @pl.when(i % 2 == 1)
    def _():
        score_blocks([i - 1, i], True)

    for h in range(2):
        m_ref[h] = jnp.broadcast_to(jnp.max(mx_ref[h], axis=1, keepdims=True), (tq, LANES))

    def pv_blocks(js):
        vs = [v_ref[0, pl.ds(pl.multiple_of(j * tk, tk), tk), :] for j in js]
        v_all = jnp.concatenate(vs, axis=0)
        v_ext = jnp.concatenate([v_all, jnp.ones_like(v_all)], axis=1)
        accs = [acc_ref[0], acc_ref[1]]
        for h in range(2):
            m = m_ref[h]
            p = jnp.concatenate([jnp.exp2(sg - m).astype(BF16)
                                 for j in js for sg in _lane_groups(s_buf[h, j])], axis=1)
            accs[h] = accs[h] + jnp.dot(p, v_ext, preferred_element_type=F32)
        acc_ref[0], acc_ref[1] = accs

    def pv_body(jj, carry):
        pv_blocks([2 * jj, 2 * jj + 1])
        return carry

    lax.fori_loop(0, (i + 1) // 2, pv_body, 0)

    @pl.when(i % 2 == 0)
    def _():
        pv_blocks([i])

    lam_p = lam_ref[...]
    lam = (jnp.exp(jnp.sum(lam_p[0:1] * lam_p[1:2], axis=1, keepdims=True))
           - jnp.exp(jnp.sum(lam_p[2:3] * lam_p[3:4], axis=1, keepdims=True)) + lambda_init)
    a0, a1 = acc_ref[0], acc_ref[1]
    o = a0[:, :LANES] / a0[:, LANES:] - lam * (a1[:, :LANES] / a1[:, LANES:])
    ms = jnp.mean(o * o, axis=-1, keepdims=True)
    y = o * lax.rsqrt(ms + NORM_EPS) * subg_ref[...] * (1.0 - lambda_init)
    o_ref[0] = y.astype(o_ref.dtype)


def _diff_attn(qa, ka, va, lam_p, subg, lambda_init, tq=512):
    b, s, w = qa.shape
    nh = w // LANES
    kern = functools.partial(_da_kernel, tq=tq, tk=tq, lambda_init=lambda_init)
    qspec = pl.BlockSpec((1, tq, LANES), lambda bb, h, i: (bb, i, h))
    kvspec = pl.BlockSpec((1, s, LANES), lambda bb, h, i: (bb, 0, h))
    return pl.pallas_call(
        kern,
        out_shape=jax.ShapeDtypeStruct((b, s, w), BF16),
        grid=(b, nh, s // tq),
        in_specs=[qspec, kvspec, kvspec,
                  pl.BlockSpec(lam_p.shape, lambda bb, h, i: (0, 0)),
                  pl.BlockSpec(subg.shape, lambda bb, h, i: (0, 0))],
        out_specs=qspec,
        scratch_shapes=[pltpu.VMEM((2, s // tq, tq, tq), F32),
                        pltpu.VMEM((2, tq, LANES), F32),
                        pltpu.VMEM((2, tq, LANES), F32),
                        pltpu.VMEM((2, tq, 2 * LANES), F32)],
        compiler_params=_params(("arbitrary",) * 3),
        name="diff_attn",
    )(qa, ka, va, lam_p, subg)


def _sb_kernel(q_ref, k_ref, v_ref, tri_ref, o_ref, c_ref, acc_ref, *, tq, tk):
    i = pl.program_id(2)
    tri = tri_ref[...]
    lane = lax.broadcasted_iota(jnp.int32, (tk, LANES), 1)

    def row_tile(r):
        q = q_ref[0, r * tk:(r + 1) * tk, :]
        zero = jnp.zeros_like(q)
        qs = (jnp.where(lane < HEAD_DIM, q, zero), jnp.where(lane >= HEAD_DIM, q, zero))
        c_ref[...] = jnp.zeros(c_ref.shape, F32)
        acc_ref[...] = jnp.zeros(acc_ref.shape, F32)

        def window(js, diag_first):
            ks = [k_ref[0, pl.ds(pl.multiple_of(j * tk, tk), tk), :] for j in js]
            vs = [v_ref[0, pl.ds(pl.multiple_of(j * tk, tk), tk), :] for j in js]
            v_all = jnp.concatenate(vs, axis=0)
            state = [(c_ref[h], acc_ref[h]) for h in range(2)]
            new_state = []
            for h in range(2):
                c_run, acc_prev = state[h]
                weights = []
                for n, k in enumerate(ks):
                    masked = diag_first and n == 0
                    z = lax.dot_general(qs[h], k, (((1,), (1,)), ((), ())), preferred_element_type=F32)
                    nz = -z
                    w = jnp.log2(1.0 + jnp.exp2(jnp.minimum(z, nz)))
                    log_1m = jnp.minimum(nz, 0.0) - w
                    log_b = jnp.minimum(z, 0.0) - w
                    if masked:
                        mask = (lax.broadcasted_iota(jnp.int32, z.shape, 1)
                                < lax.broadcasted_iota(jnp.int32, z.shape, 0))
                        log_1m = jnp.where(mask, log_1m, 0.0)
                    res = jnp.dot(log_1m.astype(BF16), tri, preferred_element_type=F32)
                    suffix, total = res[:, :tk], res[:, tk:]
                    a = jnp.concatenate([jnp.exp2(lb + sf + c_run)
                                         for lb, sf in zip(_lane_groups(log_b), _lane_groups(suffix))], axis=1)
                    if masked:
                        a = jnp.where(mask, a, 0.0)
                    weights.append(a.astype(BF16))
                    c_run = c_run + total
                a_all = jnp.concatenate(weights, axis=1)
                new_state.append((c_run, acc_prev + jnp.dot(a_all, v_all, preferred_element_type=F32)))
            for h in range(2):
                c_ref[h], acc_ref[h] = new_state[h]

        jd = i * (tq // tk) + r
        if r == 0:
            @pl.when(i == 0)
            def _():
                window([jd], True)

            @pl.when(i > 0)
            def _():
                window([jd, jd - 1], True)
        else:
            window([jd, jd - 1], True)

        def cond(carry):
            j, live = carry
            return jnp.logical_and(j >= 0, live)

        def live():
            return jnp.max(jnp.maximum(c_ref[0], c_ref[1])) > SB_DEAD_LOG2

        def body(carry):
            j, _ = carry
            window([j], False)
            return j - 1, live()

        lax.while_loop(cond, body, (jd - 2, live()))
        o_ref[0, r * tk:(r + 1) * tk, :] = jnp.where(lane < HEAD_DIM, acc_ref[0], acc_ref[1]).astype(o_ref.dtype)

    for r in range(tq // tk):
        row_tile(r)


def _sb_attn(qb, kb, vb, tq=512, tk=256):
    b, s, w = qb.shape
    nh = w // LANES
    r = lax.broadcasted_iota(jnp.int32, (tk, tk), 0)
    c = lax.broadcasted_iota(jnp.int32, (tk, tk), 1)
    tri = jnp.concatenate([(r > c).astype(BF16), jnp.ones((tk, LANES), BF16)], axis=1)
    kern = functools.partial(_sb_kernel, tq=tq, tk=tk)
    qspec = pl.BlockSpec((1, tq, LANES), lambda bb, h, i: (bb, i, h))
    kvspec = pl.BlockSpec((1, s, LANES), lambda bb, h, i: (bb, 0, h))
    return pl.pallas_call(
        kern,
        out_shape=jax.ShapeDtypeStruct((b, s, w), BF16),
        grid=(b, nh, s // tq),
        in_specs=[qspec, kvspec, kvspec, pl.BlockSpec(tri.shape, lambda bb, h, i: (0, 0))],
        out_specs=qspec,
        scratch_shapes=[pltpu.VMEM((2, tk, LANES), F32), pltpu.VMEM((2, tk, LANES), F32)],
        compiler_params=_params(("arbitrary",) * 3),
        name="sb_attn",
    )(qb, kb, vb, tri)


def _post_attn_kernel(oa_ref, ob_ref, ga_ref, gb_ref, x_ref, wda_ref, wsb_ref, wo_ref, g_ref,
                      wr_hi_ref, wr_lo_ref, br_ref, tri_ref,
                      x1_ref, h_ref, idx_ref, gate_ref, rank_ref, cnt_ref, carry_ref):
    i = pl.program_id(0)

    @pl.when(i == 0)
    def _():
        carry_ref[...] = jnp.zeros(carry_ref.shape, F32)

    ya = jnp.dot(oa_ref[...], wda_ref[...], preferred_element_type=F32)
    yb = jnp.dot(ob_ref[...], wsb_ref[...], preferred_element_type=F32)
    mix = ga_ref[...].astype(F32) * ya + gb_ref[...].astype(F32) * yb
    x1 = x_ref[...] + jnp.dot(mix.astype(BF16), wo_ref[...], preferred_element_type=F32)
    x1_ref[...] = x1
    ms = jnp.mean(x1 * x1, axis=-1, keepdims=True)
    h = x1 * lax.rsqrt(ms + NORM_EPS) * g_ref[...]
    h_ref[...] = h

    h_hi = h.astype(BF16)
    h_lo = (h - h_hi.astype(F32)).astype(BF16)
    nt = (((1,), (1,)), ((), ()))
    logits = (lax.dot_general(wr_hi_ref[...], h_hi, nt, preferred_element_type=F32)
              + lax.dot_general(wr_hi_ref[...], h_lo, nt, preferred_element_type=F32)
              + lax.dot_general(wr_lo_ref[...], h_hi, nt, preferred_element_type=F32)
              + br_ref[...])
    ne, tm = logits.shape
    eid = lax.broadcasted_iota(jnp.int32, (ne, tm), 0).astype(F32)
    vals, ids = [], []
    work = logits
    for _ in range(TOP_K):
        mx = jnp.max(work, axis=0, keepdims=True)
        sel = jnp.min(jnp.where(work == mx, eid, float(ne)), axis=0, keepdims=True)
        vals.append(mx)
        ids.append(sel)
        work = jnp.where(eid == sel, -jnp.inf, work)
    exps = [jnp.exp(v - vals[0]) for v in vals]
    denom = exps[0] + exps[1] + exps[2] + exps[3]
    onehots = [(eid == sel).astype(F32) for sel in ids]
    assigned = onehots[0] + onehots[1] + onehots[2] + onehots[3]
    before = jnp.dot(assigned.astype(BF16), tri_ref[...], preferred_element_type=F32) + carry_ref[...]
    for r in range(TOP_K):
        idx_ref[r:r + 1, :] = ids[r].astype(jnp.int32)
        gate_ref[r:r + 1, :] = exps[r] / denom
        rank_ref[r:r + 1, :] = jnp.sum(onehots[r] * before, axis=0, keepdims=True).astype(jnp.int32)
    carry = carry_ref[...] + jnp.sum(assigned, axis=1, keepdims=True)
    carry_ref[...] = carry
    cnt_ref[...] = jnp.broadcast_to(carry, cnt_ref.shape)


def _post_attn(oa, ob, ga, gb, x2, wda, wsb, wo, g, wr_hi, wr_lo, br, tm=512):
    t, d = x2.shape
    w = oa.shape[1]
    r = lax.broadcasted_iota(jnp.int32, (tm, tm), 0)
    c = lax.broadcasted_iota(jnp.int32, (tm, tm), 1)
    tri = (r < c).astype(BF16)
    row = lambda i: (i, 0)
    col = lambda i: (0, i)
    const = lambda i: (0, 0)
    full = lambda a: pl.BlockSpec(a.shape, const)
    return pl.pallas_call(
        _post_attn_kernel,
        out_shape=[jax.ShapeDtypeStruct((t, d), F32), jax.ShapeDtypeStruct((t, d), F32),
                   jax.ShapeDtypeStruct((TOP_K, t), jnp.int32), jax.ShapeDtypeStruct((TOP_K, t), F32),
                   jax.ShapeDtypeStruct((TOP_K, t), jnp.int32),
                   jax.ShapeDtypeStruct((N_EXPERTS, LANES), F32)],
        grid=(t // tm,),
        in_specs=[pl.BlockSpec((tm, w), row), pl.BlockSpec((tm, w), row),
                  pl.BlockSpec((tm, d), row), pl.BlockSpec((tm, d), row), pl.BlockSpec((tm, d), row),
                  full(wda), full(wsb), full(wo), full(g), full(wr_hi), full(wr_lo), full(br), full(tri)],
        out_specs=[pl.BlockSpec((tm, d), row), pl.BlockSpec((tm, d), row),
                   pl.BlockSpec((TOP_K, tm), col), pl.BlockSpec((TOP_K, tm), col),
                   pl.BlockSpec((TOP_K, tm), col), pl.BlockSpec((N_EXPERTS, LANES), const)],
        scratch_shapes=[pltpu.VMEM((N_EXPERTS, 1), F32)],
        compiler_params=_params(("arbitrary",)),
        name="post_attn",
    )(oa, ob, ga, gb, x2, wda, wsb, wo, g, wr_hi, wr_lo, br, tri)


def _dispatch_kernel(dest_ref, zero_blk_ref, n_used_ref, h_ref, xs_ref, zeros, sem, zsem, *, tm, t_total):
    i = pl.program_id(0)
    n_blocks = xs_ref.shape[0] // ROW_BLOCK

    def zero_copy(blk):
        row = pl.multiple_of(blk * ROW_BLOCK, ROW_BLOCK)
        return pltpu.make_async_copy(zeros, xs_ref.at[pl.ds(row, ROW_BLOCK), :], zsem)

    @pl.when(i == 0)
    def _():
        zeros[...] = jnp.zeros(zeros.shape, zeros.dtype)
        n_tail = n_blocks - n_used_ref[0]

        def start(n, carry):
            zero_copy(jnp.where(n < N_EXPERTS, zero_blk_ref[jnp.minimum(n, N_EXPERTS - 1)],
                                n_used_ref[0] + n - N_EXPERTS)).start()
            return carry

        def wait(n, carry):
            zero_copy(0).wait()
            return carry

        lax.fori_loop(0, N_EXPERTS + n_tail, start, 0)
        lax.fori_loop(0, N_EXPERTS + n_tail, wait, 0)

    def row_copy(t, k):
        dst = dest_ref[k * t_total + i * tm + t]
        return pltpu.make_async_copy(h_ref.at[pl.ds(t, 1), :], xs_ref.at[pl.ds(dst, 1), :], sem)

    def issue(t, carry):
        for k in range(TOP_K):
            row_copy(t, k).start(priority=k % 2)
        return carry

    lax.fori_loop(0, tm, issue, 0, unroll=8)
    for _ in range(TOP_K):
        pltpu.make_async_copy(h_ref, xs_ref.at[pl.ds(0, tm), :], sem).wait()


def _dispatch(dest_flat, zero_blk, n_used, h, p_rows, tm=256):
    t, d = h.shape
    kern = functools.partial(_dispatch_kernel, tm=tm, t_total=t)
    return pl.pallas_call(
        kern,
        out_shape=jax.ShapeDtypeStruct((p_rows, d), h.dtype),
        grid_spec=pltpu.PrefetchScalarGridSpec(
            num_scalar_prefetch=3,
            grid=(t // tm,),
            in_specs=[pl.BlockSpec((tm, d), lambda i, *_: (i, 0))],
            out_specs=pl.BlockSpec(memory_space=pl.ANY),
            scratch_shapes=[pltpu.VMEM((ROW_BLOCK, d), h.dtype), pltpu.SemaphoreType.DMA,
                            pltpu.SemaphoreType.DMA],
        ),
        compiler_params=_params(("arbitrary",)),
        name="dispatch",
    )(dest_flat, zero_blk, n_used, h)


def _expert_kernel(first_ref, count_ref, n_used_ref, xs_ref, wg_ref, bg_ref, wu_ref, bu_ref, wd_ref, bd_ref,
                   o_ref, wg_bf, wu_bf, wd_bf, xbuf, obuf, in_sem, out_sem):
    e = pl.program_id(0)
    first = first_ref[e]
    count = count_ref[e]

    n_pairs = count // 2
    odd = count % 2

    def rows(blk, nb):
        return pl.ds(pl.multiple_of(blk * ROW_BLOCK, ROW_BLOCK), nb * ROW_BLOCK)

    def fetch(blk, nb, slot):
        return pltpu.make_async_copy(xs_ref.at[rows(blk, nb), :], xbuf.at[slot, pl.ds(0, nb * ROW_BLOCK), :],
                                     in_sem.at[slot])

    def flush(blk, nb, slot):
        return pltpu.make_async_copy(obuf.at[slot, pl.ds(0, nb * ROW_BLOCK), :], o_ref.at[rows(blk, nb), :],
                                     out_sem.at[slot])

    def mlp(x):
        x = x.astype(BF16)
        g = jnp.dot(x, wg_bf[...], preferred_element_type=F32) + bg_ref[0]
        u = jnp.dot(x, wu_bf[...], preferred_element_type=F32) + bu_ref[0]
        g = jnp.minimum(g, SWIGLU_LIMIT)
        u = jnp.clip(u, -SWIGLU_LIMIT, SWIGLU_LIMIT)
        glu = g * jax.nn.sigmoid(SWIGLU_ALPHA * g)
        act = ((u + 1.0) * glu).astype(BF16)
        return jnp.dot(act, wd_bf[...], preferred_element_type=F32) + bd_ref[0]

    @pl.when(count > 0)
    def _():
        @pl.when(n_pairs > 0)
        def _():
            fetch(first, 2, 0).start()

        @pl.when(n_pairs == 0)
        def _():
            fetch(first, 1, 0).start()

        wg_bf[...] = wg_ref[0].astype(BF16)
        wu_bf[...] = wu_ref[0].astype(BF16)
        wd_bf[...] = wd_ref[0].astype(BF16)

    def pair(jj, carry):
        slot = jj % 2
        blk = first + 2 * jj
        fetch(blk, 2, slot).wait()

        @pl.when(jj + 1 < n_pairs)
        def _():
            fetch(blk + 2, 2, 1 - slot).start()

        @pl.when((jj + 1 == n_pairs) & (odd == 1))
        def _():
            fetch(blk + 2, 1, 1 - slot).start()

        y = mlp(xbuf[slot])

        @pl.when(jj >= 2)
        def _():
            flush(blk - 4, 2, slot).wait()

        obuf[slot] = y
        flush(blk, 2, slot).start()
        return carry

    lax.fori_loop(0, n_pairs, pair, 0)
    last = first + 2 * n_pairs
    last_slot = n_pairs % 2

    @pl.when(odd == 1)
    def _():
        fetch(last, 1, last_slot).wait()
        y = mlp(xbuf[last_slot, :ROW_BLOCK, :])

        @pl.when(n_pairs >= 2)
        def _():
            flush(last - 4, 2, last_slot).wait()

        obuf[last_slot, :ROW_BLOCK, :] = y
        flush(last, 1, last_slot).start()

    @pl.when(n_pairs >= 1)
    def _():
        flush(last - 2, 2, 1 - last_slot).wait()

    @pl.when((n_pairs >= 2) & (odd == 0))
    def _():
        flush(last - 4, 2, last_slot).wait()

    @pl.when(odd == 1)
    def _():
        flush(last, 1, last_slot).wait()

    @pl.when(e == pl.num_programs(0) - 1)
    def _():
        obuf[0, :ROW_BLOCK, :] = jnp.zeros((ROW_BLOCK, obuf.shape[2]), obuf.dtype)
        n_used = n_used_ref[0]
        n_total = o_ref.shape[0] // ROW_BLOCK

        def start(blk, carry):
            flush(blk, 1, 0).start()
            return carry

        def wait(blk, carry):
            flush(blk, 1, 0).wait()
            return carry

        lax.fori_loop(n_used, n_total, start, 0)
        lax.fori_loop(n_used, n_total, wait, 0)


def _experts(first_blk, blk_count, n_used, n_out_blocks, xs, wg, bg, wu, bu, wd, bd):
    d = xs.shape[1]
    n_exp, _, f = wg.shape
    wmap = lambda e, *_: (e, 0, 0)
    return pl.pallas_call(
        _expert_kernel,
        out_shape=jax.ShapeDtypeStruct((n_out_blocks * ROW_BLOCK, d), F32),
        grid_spec=pltpu.PrefetchScalarGridSpec(
            num_scalar_prefetch=3,
            grid=(n_exp,),
            in_specs=[pl.BlockSpec(memory_space=pl.ANY),
                      pl.BlockSpec((1, d, f), wmap), pl.BlockSpec((1, 1, f), wmap),
                      pl.BlockSpec((1, d, f), wmap), pl.BlockSpec((1, 1, f), wmap),
                      pl.BlockSpec((1, f, d), wmap), pl.BlockSpec((1, 1, d), wmap)],
            out_specs=pl.BlockSpec(memory_space=pl.ANY),
            scratch_shapes=[pltpu.VMEM((d, f), BF16), pltpu.VMEM((d, f), BF16), pltpu.VMEM((f, d), BF16),
                            pltpu.VMEM((2, 2 * ROW_BLOCK, d), F32), pltpu.VMEM((2, 2 * ROW_BLOCK, d), F32),
                            pltpu.SemaphoreType.DMA((2,)), pltpu.SemaphoreType.DMA((2,))],
        ),
        compiler_params=_params(("arbitrary",)),
        name="experts",
    )(first_blk, blk_count, n_used, xs, wg, bg, wu, bu, wd, bd)


def _combine_kernel(base_ref, npass_ref, rows_ref, idx_ref, lr_ref, gate_ref, x1_ref, g_ref, o_ref,
                    wbuf, sem, *, tm):
    i = pl.program_id(0)
    slot = i % 2
    win = COMBINE_WINDOW
    width = N_EXPERTS * win

    def start_fetch(tile, p, s):
        for e in range(N_EXPERTS):
            start = pl.multiple_of(base_ref[tile * N_EXPERTS + e] + p * win, SUBLANES)
            pltpu.make_async_copy(rows_ref.at[pl.ds(start, win), :],
                                  wbuf.at[s, pl.ds(e * win, win), :], sem.at[s]).start()

    def wait_fetch(s):
        pltpu.make_async_copy(rows_ref.at[pl.ds(0, width), :], wbuf.at[s], sem.at[s]).wait()

    @pl.when(i == 0)
    def _():
        start_fetch(0, 0, 0)

    @pl.when(i + 1 < pl.num_programs(0))
    def _():
        start_fetch(i + 1, 0, 1 - slot)

    idx = idx_ref[...]
    lr = lr_ref[...]
    gates = gate_ref[...]
    col = lax.broadcasted_iota(jnp.int32, (tm, width), 1)

    def weights(p):
        w = jnp.zeros((tm, width), F32)
        for k in range(TOP_K):
            r = lr[:, k:k + 1] - p * win
            tgt = jnp.where((r >= 0) & (r < win), idx[:, k:k + 1] * win + r, -1)
            w = jnp.where(col == tgt, gates[:, k:k + 1], w)
        return w.astype(BF16)

    def gathered(p):
        return jnp.dot(weights(p), wbuf[slot].astype(BF16), preferred_element_type=F32)

    wait_fetch(slot)
    y = x1_ref[...] + gathered(0)

    def extra_pass(p, acc):
        start_fetch(i, p, slot)
        wait_fetch(slot)
        return acc + gathered(p)

    y = lax.fori_loop(1, npass_ref[i], extra_pass, y)
    ms = jnp.mean(y * y, axis=-1, keepdims=True)
    o_ref[...] = y * lax.rsqrt(ms + NORM_EPS) * g_ref[...]


def _combine(base_tbl, npass, rows, idx_t, lr_t, gates_t, x1, g, tm):
    t, d = x1.shape
    kern = functools.partial(_combine_kernel, tm=tm)
    tok = lambda i, *_: (i, 0)
    return pl.pallas_call(
        kern,
        out_shape=jax.ShapeDtypeStruct((t, d), F32),
        grid_spec=pltpu.PrefetchScalarGridSpec(
            num_scalar_prefetch=2,
            grid=(t // tm,),
            in_specs=[pl.BlockSpec(memory_space=pl.ANY),
                      pl.BlockSpec((tm, TOP_K), tok), pl.BlockSpec((tm, TOP_K), tok),
                      pl.BlockSpec((tm, TOP_K), tok), pl.BlockSpec((tm, d), tok),
                      pl.BlockSpec((1, d), lambda i, *_: (0, 0))],
            out_specs=pl.BlockSpec((tm, d), tok),
            scratch_shapes=[pltpu.VMEM((2, N_EXPERTS * COMBINE_WINDOW, d), F32),
                            pltpu.SemaphoreType.DMA((2,))],
        ),
        compiler_params=_params(("arbitrary",)),
        name="combine",
    )(base_tbl, npass, rows, idx_t, lr_t, gates_t, x1, g)


def _rope_tables(seq):
    inv = 1.0 / (ROPE_THETA ** (jnp.arange(0, HEAD_DIM, 2, dtype=F32) / HEAD_DIM))
    ang = jnp.arange(seq, dtype=F32)[:, None] * inv[None, :]
    cos, sin = jnp.cos(ang), jnp.sin(ang)
    return jnp.concatenate([cos] * 4, axis=1), jnp.concatenate([-sin, sin, -sin, sin], axis=1)


def kernel(x, norm_mix_g, w_in, lambda_q1, lambda_k1, lambda_q2, lambda_k2, da_subln_g, w_da_out, w_sb_out, w_o, norm_ffn_g, w_router, b_router, w_gate, b_gate, w_up, b_up, w_down, b_down, norm_final_g):
    b, s, d = x.shape
    depth = w_in.shape[0]
    t = b * s
    cos_t, sin_t = _rope_tables(s)
    n_blocks = (t * TOP_K + ROW_BLOCK - 1) // ROW_BLOCK + N_EXPERTS
    p_rows = n_blocks * ROW_BLOCK
    x2 = x.reshape(t, d)
    for l in range(depth):
        lambda_init = 0.8 - 0.6 * math.exp(-0.3 * l)
        qa, ka, va, qb, kb, vb, ga, gb = _in_proj(
            x2, norm_mix_g[l][None, :], w_in[l].astype(BF16), cos_t, sin_t, s)
        lam_p = jnp.stack([lambda_q1[l], lambda_k1[l], lambda_q2[l], lambda_k2[l]]).astype(F32)
        seq3 = lambda a: a.reshape(b, s, a.shape[1])
        oa = _diff_attn(seq3(qa), seq3(ka), seq3(va), lam_p, da_subln_g[l][None, :].astype(F32), lambda_init)
        ob = _sb_attn(seq3(qb), seq3(kb), seq3(vb))
        wr = w_router[l].T.astype(F32)
        wr_hi = wr.astype(BF16)
        wr_lo = (wr - wr_hi.astype(F32)).astype(BF16)
        x1, h, idx, gates, rank, cnt = _post_attn(
            oa.reshape(t, -1), ob.reshape(t, -1), ga, gb, x2,
            w_da_out[l].astype(BF16), w_sb_out[l].astype(BF16), w_o[l].astype(BF16),
            norm_ffn_g[l][None, :], wr_hi, wr_lo, b_router[l][:, None].astype(F32))
        counts = cnt[:, 0].astype(jnp.int32)
        padded = (counts + ROW_BLOCK - 1) // ROW_BLOCK * ROW_BLOCK
        pad_ends = jnp.cumsum(padded)
        pad_starts = pad_ends - padded
        experts = jnp.arange(N_EXPERTS, dtype=jnp.int32)
        chosen = idx[:, :, None] == experts
        base = jnp.sum(jnp.where(chosen, pad_starts, 0), axis=-1)
        dest = (base + rank).reshape(-1)
        n_used = (pad_ends[-1] // ROW_BLOCK).astype(jnp.int32)
        first_blk = (pad_starts // ROW_BLOCK).astype(jnp.int32)
        blk_count = (padded // ROW_BLOCK).astype(jnp.int32)
        zero_blk = jnp.maximum(pad_ends // ROW_BLOCK - 1, 0).astype(jnp.int32)
        n_tiles = t // COMBINE_TILE
        tile_cnt = jnp.sum(chosen.reshape(TOP_K, n_tiles, COMBINE_TILE, N_EXPERTS), axis=(0, 2), dtype=jnp.int32)
        tile_carry = jnp.cumsum(tile_cnt, axis=0) - tile_cnt
        run_start = pad_starts[None, :] + tile_carry
        run_skew = run_start % SUBLANES
        run_base = (run_start - run_skew).reshape(-1).astype(jnp.int32)
        shift_tok = jnp.repeat(tile_carry - run_skew, COMBINE_TILE, axis=0)
        local_rank = rank - jnp.sum(jnp.where(chosen, shift_tok[None], 0), axis=-1)
        n_pass = jnp.maximum((jnp.max(tile_cnt + run_skew, axis=1) + COMBINE_WINDOW - 1) // COMBINE_WINDOW,
                             1).astype(jnp.int32)
        xs = _dispatch(dest, zero_blk, n_used[None], h, p_rows)
        rows = _experts(first_blk, blk_count, n_used[None], n_blocks + 1, xs, w_gate[l], b_gate[l][:, None, :],
                        w_up[l], b_up[l][:, None, :], w_down[l], b_down[l][:, None, :])
        g_next = norm_final_g[None, :] if l == depth - 1 else jnp.ones((1, d), F32)
        x2 = _combine(run_base, n_pass, rows, idx.T, local_rank.T.astype(jnp.int32), gates.T, x1, g_next,
                      COMBINE_TILE)
        if l != depth - 1:
            raise NotImplementedError("only the final layer's norm is fused into the combine kernel")
    return x2.reshape(b, s, d)
```

```python
import functools
import math

import jax
import jax.numpy as jnp
from jax import lax
from jax.experimental import pallas as pl
from jax.experimental.pallas import tpu as pltpu

F32 = jnp.float32
BF16 = jnp.bfloat16

DA_HEADS = 4
HEAD_DIM = 64
N_EXPERTS = 32
TOP_K = 4
ROPE_THETA = 10000.0
SWIGLU_LIMIT = 7.0
SWIGLU_ALPHA = 1.702
NORM_EPS = 1e-5
ROW_BLOCK = 256
COMBINE_TILE = 256
SUBLANES = 8
COMBINE_WINDOW = 72
LANES = 128
NEG_BIG = -1e30
LOG2E = math.log2(math.e)
SB_DEAD_LOG2 = -150.0

VMEM_LIMIT = 56 * 1024 * 1024


def _params(sem, vmem=VMEM_LIMIT):
    return pltpu.CompilerParams(dimension_semantics=sem, vmem_limit_bytes=vmem)


def _in_proj_kernel(x_ref, g_ref, w_ref, cos_ref, sin_ref,
                    qa_ref, ka_ref, va_ref, qb_ref, kb_ref, vb_ref, ga_ref, gb_ref):
    x = x_ref[...]
    ms = jnp.mean(x * x, axis=-1, keepdims=True)
    h = (x * lax.rsqrt(ms + NORM_EPS) * g_ref[...]).astype(BF16)
    cos = cos_ref[...]
    sin = sin_ref[...]
    lane = lax.broadcasted_iota(jnp.int32, cos.shape, 1)
    first_half = (lane & (HEAD_DIM - 1)) < HEAD_DIM // 2

    def proj(c0, width):
        return jnp.dot(h, w_ref[:, c0:c0 + width], preferred_element_type=F32)

    def rope(r):
        outs = []
        for g in range(r.shape[1] // LANES):
            xg = r[:, g * LANES:(g + 1) * LANES]
            rot = jnp.where(first_half, pltpu.roll(xg, LANES - HEAD_DIM // 2, 1),
                            pltpu.roll(xg, HEAD_DIM // 2, 1))
            outs.append(xg * cos + rot * sin)
        return jnp.concatenate(outs, axis=1)

    scale = HEAD_DIM ** -0.5 * LOG2E
    w = qa_ref.shape[1]
    d = ga_ref.shape[1]
    qa_ref[...] = (rope(proj(0, w)) * scale).astype(BF16)
    ka_ref[...] = rope(proj(w, w)).astype(BF16)
    va_ref[...] = proj(2 * w, w).astype(BF16)
    qb_ref[...] = (proj(3 * w, w) * scale).astype(BF16)
    kb_ref[...] = proj(4 * w, w).astype(BF16)
    vb_ref[...] = proj(5 * w, w).astype(BF16)
    ga_ref[...] = jax.nn.sigmoid(proj(6 * w, d)).astype(BF16)
    gb_ref[...] = jax.nn.sigmoid(proj(6 * w + d, d)).astype(BF16)


def _in_proj(x2, g, w_in_bf, cos_t, sin_t, seq, tm=512):
    t, d = x2.shape
    w = 512
    nseq = seq // tm
    outs = [jax.ShapeDtypeStruct((t, w), BF16)] * 6 + [jax.ShapeDtypeStruct((t, d), BF16)] * 2
    row = lambda i: (i, 0)
    return pl.pallas_call(
        _in_proj_kernel,
        out_shape=outs,
        grid=(t // tm,),
        in_specs=[
            pl.BlockSpec((tm, d), row),
            pl.BlockSpec((1, d), lambda i: (0, 0)),
            pl.BlockSpec(w_in_bf.shape, lambda i: (0, 0)),
            pl.BlockSpec((tm, LANES), lambda i: (i % nseq, 0)),
            pl.BlockSpec((tm, LANES), lambda i: (i % nseq, 0)),
        ],
        out_specs=[pl.BlockSpec((tm, w), row)] * 6 + [pl.BlockSpec((tm, d), row)] * 2,
        compiler_params=_params(("arbitrary",)),
        name="in_proj",
    )(x2, g, w_in_bf, cos_t, sin_t)


def _lane_groups(x):
    return [x[:, g * LANES:(g + 1) * LANES] for g in range(x.shape[1] // LANES)]


def _da_kernel(q_ref, k_ref, v_ref, lam_ref, subg_ref, o_ref, s_buf, mx_ref, m_ref, acc_ref,
               *, tq, tk, lambda_init):
    i = pl.program_id(2)
    q = q_ref[0]
    lane = lax.broadcasted_iota(jnp.int32, q.shape, 1)
    zero = jnp.zeros_like(q)
    qs = (jnp.where(lane < HEAD_DIM, q, zero), jnp.where(lane >= HEAD_DIM, q, zero))
    mx_ref[...] = jnp.full(mx_ref.shape, NEG_BIG, F32)
    acc_ref[...] = jnp.zeros(acc_ref.shape, F32)

    def score_blocks(js, diag_last):
        maxes = [mx_ref[0], mx_ref[1]]
        for n, j in enumerate(js):
            k = k_ref[0, pl.ds(pl.multiple_of(j * tk, tk), tk), :]
            for h in range(2):
                s = lax.dot_general(qs[h], k, (((1,), (1,)), ((), ())), preferred_element_type=F32)
                if diag_last and n == len(js) - 1:
                    rows = lax.broadcasted_iota(jnp.int32, s.shape, 0)
                    cols = lax.broadcasted_iota(jnp.int32, s.shape, 1)
                    s = jnp.where(rows >= cols, s, NEG_BIG)
                s_buf[h, j] = s
                for sg in _lane_groups(s):
                    maxes[h] = jnp.maximum(maxes[h], sg)
        mx_ref[0], mx_ref[1] = maxes

    def score_body(jj, carry):
        score_blocks([2 * jj, 2 * jj + 1], False)
        return carry

    lax.fori_loop(0, i // 2, score_body, 0)

    @pl.when(i % 2 == 0)
    def _():
        score_blocks([i], True)

    @pl.when(i % 2 == 1)
    def _():
        score_blocks([i - 1, i], True)

    for h in range(2):
        m_ref[h] = jnp.broadcast_to(jnp.max(mx_ref[h], axis=1, keepdims=True), (tq, LANES))

    def pv_blocks(js):
        vs = [v_ref[0, pl.ds(pl.multiple_of(j * tk, tk), tk), :] for j in js]
        v_all = jnp.concatenate(vs, axis=0)
        v_ext = jnp.concatenate([v_all, jnp.ones_like(v_all)], axis=1)
        accs = [acc_ref[0], acc_ref[1]]
        for h in range(2):
            m = m_ref[h]
            p = jnp.concatenate([jnp.exp2(sg - m).astype(BF16)
                                 for j in js for sg in _lane_groups(s_buf[h, j])], axis=1)
            accs[h] = accs[h] + jnp.dot(p, v_ext, preferred_element_type=F32)
        acc_ref[0], acc_ref[1] = accs

    def pv_body(jj, carry):
        pv_blocks([2 * jj, 2 * jj + 1])
        return carry

    lax.fori_loop(0, (i + 1) // 2, pv_body, 0)

    @pl.when(i % 2 == 0)
    def _():
        pv_blocks([i])

    lam_p = lam_ref[...]
    lam = (jnp.exp(jnp.sum(lam_p[0:1] * lam_p[1:2], axis=1, keepdims=True))
           - jnp.exp(jnp.sum(lam_p[2:3] * lam_p[3:4], axis=1, keepdims=True)) + lambda_init)
    a0, a1 = acc_ref[0], acc_ref[1]
    o = a0[:, :LANES] / a0[:, LANES:] - lam * (a1[:, :LANES] / a1[:, LANES:])
    ms = jnp.mean(o * o, axis=-1, keepdims=True)
    y = o * lax.rsqrt(ms + NORM_EPS) * subg_ref[...] * (1.0 - lambda_init)
    o_ref[0] = y.astype(o_ref.dtype)


def _diff_attn(qa, ka, va, lam_p, subg, lambda_init, tq=512):
    b, s, w = qa.shape
    nh = w // LANES
    kern = functools.partial(_da_kernel, tq=tq, tk=tq, lambda_init=lambda_init)
    qspec = pl.BlockSpec((1, tq, LANES), lambda bb, h, i: (bb, i, h))
    kvspec = pl.BlockSpec((1, s, LANES), lambda bb, h, i: (bb, 0, h))
    return pl.pallas_call(
        kern,
        out_shape=jax.ShapeDtypeStruct((b, s, w), BF16),
        grid=(b, nh, s // tq),
        in_specs=[qspec, kvspec, kvspec,
                  pl.BlockSpec(lam_p.shape, lambda bb, h, i: (0, 0)),
                  pl.BlockSpec(subg.shape, lambda bb, h, i: (0, 0))],
        out_specs=qspec,
        scratch_shapes=[pltpu.VMEM((2, s // tq, tq, tq), F32),
                        pltpu.VMEM((2, tq, LANES), F32),
                        pltpu.VMEM((2, tq, LANES), F32),
                        pltpu.VMEM((2, tq, 2 * LANES), F32)],
        compiler_params=_params(("arbitrary",) * 3),
        name="diff_attn",
    )(qa, ka, va, lam_p, subg)


def _sb_kernel(q_ref, k_ref, v_ref, tri_ref, o_ref, c_ref, acc_ref, *, tq, tk):
    i = pl.program_id(2)
    n_tiles = tq // tk
    tri = tri_ref[...]
    lane = lax.broadcasted_iota(jnp.int32, (tk, LANES), 1)
    qs = []
    for r in range(n_tiles):
        q = q_ref[0, r * tk:(r + 1) * tk, :]
        zero = jnp.zeros_like(q)
        qs.append((jnp.where(lane < HEAD_DIM, q, zero), jnp.where(lane >= HEAD_DIM, q, zero)))
    c_ref[...] = jnp.zeros(c_ref.shape, F32)
    acc_ref[...] = jnp.zeros(acc_ref.shape, F32)

    def load_kv(j):
        rows = pl.ds(pl.multiple_of(j * tk, tk), tk)
        return k_ref[0, rows, :], v_ref[0, rows, :]

    def window(jobs):
        keys = [(r, h) for r, _, _ in jobs for h in range(2)]
        state = {key: (c_ref[key], acc_ref[key]) for key in keys}
        new_state = {}
        for r, kvs, diag_first in jobs:
            v_all = jnp.concatenate([v for _, v in kvs], axis=0)
            for h in range(2):
                c_run, acc_prev = state[(r, h)]
                weights = []
                for n, (k, _) in enumerate(kvs):
                    masked = diag_first and n == 0
                    z = lax.dot_general(qs[r][h], k, (((1,), (1,)), ((), ())), preferred_element_type=F32)
                    soft = jnp.maximum(z, 0.0) + jnp.log2(1.0 + jnp.exp2(-jnp.abs(z)))
                    log_b = z - soft
                    if masked:
                        mask = (lax.broadcasted_iota(jnp.int32, z.shape, 1)
                                < lax.broadcasted_iota(jnp.int32, z.shape, 0))
                        soft = jnp.where(mask, soft, 0.0)
                    res = jnp.dot(soft.astype(BF16), tri, preferred_element_type=F32)
                    suffix, total = res[:, :tk], res[:, tk:]
                    a = jnp.concatenate([jnp.exp2(lb + sf + c_run)
                                         for lb, sf in zip(_lane_groups(log_b), _lane_groups(suffix))], axis=1)
                    if masked:
                        a = jnp.where(mask, a, 0.0)
                    weights.append(a.astype(BF16))
                    c_run = c_run + total
                a_all = jnp.concatenate(weights, axis=1)
                new_state[(r, h)] = (c_run, acc_prev + jnp.dot(a_all, v_all, preferred_element_type=F32))
        for key in keys:
            c_ref[key], acc_ref[key] = new_state[key]

    first = i * n_tiles

    @pl.when(i == 0)
    def _():
        blocks = [load_kv(r) for r in range(n_tiles)]
        window([(r, blocks[max(r - 1, 0):r + 1][::-1], True) for r in range(n_tiles)])

    @pl.when(i > 0)
    def _():
        blocks = [load_kv(first - 1 + n) for n in range(n_tiles + 1)]
        window([(r, [blocks[r + 1], blocks[r]], True) for r in range(n_tiles)])

    for r in range(n_tiles):
        def live(r=r):
            return jnp.max(jnp.maximum(c_ref[r, 0], c_ref[r, 1])) > SB_DEAD_LOG2

        def cond(carry):
            j, alive = carry
            return jnp.logical_and(j >= 0, alive)

        def body(carry, r=r, live=live):
            j, _ = carry
            window([(r, [load_kv(j)], False)])
            return j - 1, live()

        lax.while_loop(cond, body, (first + r - 2, live()))
        o_ref[0, r * tk:(r + 1) * tk, :] = jnp.where(lane < HEAD_DIM, acc_ref[r, 0],
                                                     acc_ref[r, 1]).astype(o_ref.dtype)


def _sb_attn(qb, kb, vb, tq=512, tk=256):
    b, s, w = qb.shape
    nh = w // LANES
    r = lax.broadcasted_iota(jnp.int32, (tk, tk), 0)
    c = lax.broadcasted_iota(jnp.int32, (tk, tk), 1)
    tri = -jnp.concatenate([(r > c).astype(BF16), jnp.ones((tk, LANES), BF16)], axis=1)
    kern = functools.partial(_sb_kernel, tq=tq, tk=tk)
    qspec = pl.BlockSpec((1, tq, LANES), lambda bb, h, i: (bb, i, h))
    kvspec = pl.BlockSpec((1, s, LANES), lambda bb, h, i: (bb, 0, h))
    return pl.pallas_call(
        kern,
        out_shape=jax.ShapeDtypeStruct((b, s, w), BF16),
        grid=(b, nh, s // tq),
        in_specs=[qspec, kvspec, kvspec, pl.BlockSpec(tri.shape, lambda bb, h, i: (0, 0))],
        out_specs=qspec,
        scratch_shapes=[pltpu.VMEM((tq // tk, 2, tk, LANES), F32), pltpu.VMEM((tq // tk, 2, tk, LANES), F32)],
        compiler_params=_params(("arbitrary",) * 3),
        name="sb_attn",
    )(qb, kb, vb, tri)


def _post_attn_kernel(oa_ref, ob_ref, ga_ref, gb_ref, x_ref, wda_ref, wsb_ref, wo_ref, g_ref,
                      wr_hi_ref, wr_lo_ref, br_ref, tri_ref,
                      x1_ref, h_ref, idx_ref, gate_ref, rank_ref, cnt_ref, carry_ref):
    i = pl.program_id(0)

    @pl.when(i == 0)
    def _():
        carry_ref[...] = jnp.zeros(carry_ref.shape, F32)

    ya = jnp.dot(oa_ref[...], wda_ref[...], preferred_element_type=F32)
    yb = jnp.dot(ob_ref[...], wsb_ref[...], preferred_element_type=F32)
    mix = ga_ref[...].astype(F32) * ya + gb_ref[...].astype(F32) * yb
    x1 = x_ref[...] + jnp.dot(mix.astype(BF16), wo_ref[...], preferred_element_type=F32)
    x1_ref[...] = x1
    ms = jnp.mean(x1 * x1, axis=-1, keepdims=True)
    h = x1 * lax.rsqrt(ms + NORM_EPS) * g_ref[...]
    h_ref[...] = h

    h_hi = h.astype(BF16)
    h_lo = (h - h_hi.astype(F32)).astype(BF16)
    ne = br_ref.shape[0]
    p_hi = jnp.dot(h_hi, wr_hi_ref[...], preferred_element_type=F32)
    p_lo = jnp.dot(h_lo, wr_lo_ref[...], preferred_element_type=F32)
    by_token = p_hi + pltpu.roll(p_hi, LANES - ne, 1) + p_lo
    logits = by_token.T[:ne] + br_ref[...]
    tm = logits.shape[1]
    eid = lax.broadcasted_iota(jnp.int32, (ne, tm), 0).astype(F32)
    vals, ids = [], []
    work = logits
    for _ in range(TOP_K):
        mx = jnp.max(work, axis=0, keepdims=True)
        sel = jnp.min(jnp.where(work == mx, eid, float(ne)), axis=0, keepdims=True)
        vals.append(mx)
        ids.append(sel)
        work = jnp.where(eid == sel, -jnp.inf, work)
    exps = [jnp.exp(v - vals[0]) for v in vals]
    denom = exps[0] + exps[1] + exps[2] + exps[3]
    onehots = [(eid == sel).astype(F32) for sel in ids]
    assigned = onehots[0] + onehots[1] + onehots[2] + onehots[3]
    before = jnp.dot(assigned.astype(BF16), tri_ref[...], preferred_element_type=F32) + carry_ref[...]
    for r in range(TOP_K):
        idx_ref[r:r + 1, :] = ids[r].astype(jnp.int32)
        gate_ref[r:r + 1, :] = exps[r] / denom
        rank_ref[r:r + 1, :] = jnp.sum(onehots[r] * before, axis=0, keepdims=True).astype(jnp.int32)
    carry = carry_ref[...] + jnp.sum(assigned, axis=1, keepdims=True)
    carry_ref[...] = carry
    cnt_ref[...] = jnp.broadcast_to(carry, cnt_ref.shape)


def _post_attn(oa, ob, ga, gb, x2, wda, wsb, wo, g, wr_hi, wr_lo, br, tm=512):
    t, d = x2.shape
    w = oa.shape[1]
    r = lax.broadcasted_iota(jnp.int32, (tm, tm), 0)
    c = lax.broadcasted_iota(jnp.int32, (tm, tm), 1)
    tri = (r < c).astype(BF16)
    row = lambda i: (i, 0)
    col = lambda i: (0, i)
    const = lambda i: (0, 0)
    full = lambda a: pl.BlockSpec(a.shape, const)
    return pl.pallas_call(
        _post_attn_kernel,
        out_shape=[jax.ShapeDtypeStruct((t, d), F32), jax.ShapeDtypeStruct((t, d), F32),
                   jax.ShapeDtypeStruct((TOP_K, t), jnp.int32), jax.ShapeDtypeStruct((TOP_K, t), F32),
                   jax.ShapeDtypeStruct((TOP_K, t), jnp.int32),
                   jax.ShapeDtypeStruct((N_EXPERTS, LANES), F32)],
        grid=(t // tm,),
        in_specs=[pl.BlockSpec((tm, w), row), pl.BlockSpec((tm, w), row),
                  pl.BlockSpec((tm, d), row), pl.BlockSpec((tm, d), row), pl.BlockSpec((tm, d), row),
                  full(wda), full(wsb), full(wo), full(g), full(wr_hi), full(wr_lo), full(br), full(tri)],
        out_specs=[pl.BlockSpec((tm, d), row), pl.BlockSpec((tm, d), row),
                   pl.BlockSpec((TOP_K, tm), col), pl.BlockSpec((TOP_K, tm), col),
                   pl.BlockSpec((TOP_K, tm), col), pl.BlockSpec((N_EXPERTS, LANES), const)],
        scratch_shapes=[pltpu.VMEM((N_EXPERTS, 1), F32)],
        compiler_params=_params(("arbitrary",)),
        name="post_attn",
    )(oa, ob, ga, gb, x2, wda, wsb, wo, g, wr_hi, wr_lo, br, tri)


def _dispatch_kernel(dest_ref, zero_blk_ref, n_used_ref, h_ref, xs_ref, zeros, sem, zsem, *, tm, t_total):
    i = pl.program_id(0)
    n_blocks = xs_ref.shape[0] // ROW_BLOCK

    def zero_copy(blk):
        row = pl.multiple_of(blk * ROW_BLOCK, ROW_BLOCK)
        return pltpu.make_async_copy(zeros, xs_ref.at[pl.ds(row, ROW_BLOCK), :], zsem)

    @pl.when(i == 0)
    def _():
        zeros[...] = jnp.zeros(zeros.shape, zeros.dtype)
        n_tail = n_blocks - n_used_ref[0]

        def start(n, carry):
            zero_copy(jnp.where(n < N_EXPERTS, zero_blk_ref[jnp.minimum(n, N_EXPERTS - 1)],
                                n_used_ref[0] + n - N_EXPERTS)).start()
            return carry

        def wait(n, carry):
            zero_copy(0).wait()
            return carry

        lax.fori_loop(0, N_EXPERTS + n_tail, start, 0)
        lax.fori_loop(0, N_EXPERTS + n_tail, wait, 0)

    def row_copy(t, k):
        dst = dest_ref[k * t_total + i * tm + t]
        return pltpu.make_async_copy(h_ref.at[pl.ds(t, 1), :], xs_ref.at[pl.ds(dst, 1), :], sem)

    def issue(t, carry):
        for k in range(TOP_K):
            row_copy(t, k).start(priority=k % 2)
        return carry

    lax.fori_loop(0, tm, issue, 0, unroll=8)
    for _ in range(TOP_K):
        pltpu.make_async_copy(h_ref, xs_ref.at[pl.ds(0, tm), :], sem).wait()


def _dispatch(dest_flat, zero_blk, n_used, h, p_rows, tm=256):
    t, d = h.shape
    kern = functools.partial(_dispatch_kernel, tm=tm, t_total=t)
    return pl.pallas_call(
        kern,
        out_shape=jax.ShapeDtypeStruct((p_rows, d), h.dtype),
        grid_spec=pltpu.PrefetchScalarGridSpec(
            num_scalar_prefetch=3,
            grid=(t // tm,),
            in_specs=[pl.BlockSpec((tm, d), lambda i, *_: (i, 0))],
            out_specs=pl.BlockSpec(memory_space=pl.ANY),
            scratch_shapes=[pltpu.VMEM((ROW_BLOCK, d), h.dtype), pltpu.SemaphoreType.DMA,
                            pltpu.SemaphoreType.DMA],
        ),
        compiler_params=_params(("arbitrary",)),
        name="dispatch",
    )(dest_flat, zero_blk, n_used, h)


def _expert_kernel(first_ref, count_ref, n_used_ref, xs_ref, wg_ref, bg_ref, wu_ref, bu_ref, wd_ref, bd_ref,
                   o_ref, wg_bf, wu_bf, wd_bf, xbuf, obuf, in_sem, out_sem):
    e = pl.program_id(0)
    first = first_ref[e]
    count = count_ref[e]

    n_pairs = count // 2
    odd = count % 2

    def rows(blk, nb):
        return pl.ds(pl.multiple_of(blk * ROW_BLOCK, ROW_BLOCK), nb * ROW_BLOCK)

    def fetch(blk, nb, slot):
        return pltpu.make_async_copy(xs_ref.at[rows(blk, nb), :], xbuf.at[slot, pl.ds(0, nb * ROW_BLOCK), :],
                                     in_sem.at[slot])

    def flush(blk, nb, slot):
        return pltpu.make_async_copy(obuf.at[slot, pl.ds(0, nb * ROW_BLOCK), :], o_ref.at[rows(blk, nb), :],
                                     out_sem.at[slot])

    def mlp(x):
        x = x.astype(BF16)
        g = jnp.dot(x, wg_bf[...], preferred_element_type=F32) + bg_ref[0]
        u = jnp.dot(x, wu_bf[...], preferred_element_type=F32) + bu_ref[0]
        g = jnp.minimum(g, SWIGLU_LIMIT)
        u = jnp.clip(u, -SWIGLU_LIMIT, SWIGLU_LIMIT)
        glu = g * jax.nn.sigmoid(SWIGLU_ALPHA * g)
        act = ((u + 1.0) * glu).astype(BF16)
        return jnp.dot(act, wd_bf[...], preferred_element_type=F32) + bd_ref[0]

    @pl.when(count > 0)
    def _():
        @pl.when(n_pairs > 0)
        def _():
            fetch(first, 2, 0).start()

        @pl.when(n_pairs == 0)
        def _():
            fetch(first, 1, 0).start()

        wg_bf[...] = wg_ref[0].astype(BF16)
        wu_bf[...] = wu_ref[0].astype(BF16)
        wd_bf[...] = wd_ref[0].astype(BF16)

    def pair(jj, carry):
        slot = jj % 2
        blk = first + 2 * jj
        fetch(blk, 2, slot).wait()

        @pl.when(jj + 1 < n_pairs)
        def _():
            fetch(blk + 2, 2, 1 - slot).start()

        @pl.when((jj + 1 == n_pairs) & (odd == 1))
        def _():
            fetch(blk + 2, 1, 1 - slot).start()

        y = mlp(xbuf[slot])

        @pl.when(jj >= 2)
        def _():
            flush(blk - 4, 2, slot).wait()

        obuf[slot] = y
        flush(blk, 2, slot).start()
        return carry

    lax.fori_loop(0, n_pairs, pair, 0)
    last = first + 2 * n_pairs
    last_slot = n_pairs % 2

    @pl.when(odd == 1)
    def _():
        fetch(last, 1, last_slot).wait()
        y = mlp(xbuf[last_slot, :ROW_BLOCK, :])

        @pl.when(n_pairs >= 2)
        def _():
            flush(last - 4, 2, last_slot).wait()

        obuf[last_slot, :ROW_BLOCK, :] = y
        flush(last, 1, last_slot).start()

    @pl.when(n_pairs >= 1)
    def _():
        flush(last - 2, 2, 1 - last_slot).wait()

    @pl.when((n_pairs >= 2) & (odd == 0))
    def _():
        flush(last - 4, 2, last_slot).wait()

    @pl.when(odd == 1)
    def _():
        flush(last, 1, last_slot).wait()

    @pl.when(e == pl.num_programs(0) - 1)
    def _():
        obuf[0, :ROW_BLOCK, :] = jnp.zeros((ROW_BLOCK, obuf.shape[2]), obuf.dtype)
        n_used = n_used_ref[0]
        n_total = o_ref.shape[0] // ROW_BLOCK

        def start(blk, carry):
            flush(blk, 1, 0).start()
            return carry

        def wait(blk, carry):
            flush(blk, 1, 0).wait()
            return carry

        lax.fori_loop(n_used, n_total, start, 0)
        lax.fori_loop(n_used, n_total, wait, 0)


def _experts(first_blk, blk_count, n_used, n_out_blocks, xs, wg, bg, wu, bu, wd, bd):
    d = xs.shape[1]
    n_exp, _, f = wg.shape
    wmap = lambda e, *_: (e, 0, 0)
    return pl.pallas_call(
        _expert_kernel,
        out_shape=jax.ShapeDtypeStruct((n_out_blocks * ROW_BLOCK, d), F32),
        grid_spec=pltpu.PrefetchScalarGridSpec(
            num_scalar_prefetch=3,
            grid=(n_exp,),
            in_specs=[pl.BlockSpec(memory_space=pl.ANY),
                      pl.BlockSpec((1, d, f), wmap), pl.BlockSpec((1, 1, f), wmap),
                      pl.BlockSpec((1, d, f), wmap), pl.BlockSpec((1, 1, f), wmap),
                      pl.BlockSpec((1, f, d), wmap), pl.BlockSpec((1, 1, d), wmap)],
            out_specs=pl.BlockSpec(memory_space=pl.ANY),
            scratch_shapes=[pltpu.VMEM((d, f), BF16), pltpu.VMEM((d, f), BF16), pltpu.VMEM((f, d), BF16),
                            pltpu.VMEM((2, 2 * ROW_BLOCK, d), F32), pltpu.VMEM((2, 2 * ROW_BLOCK, d), F32),
                            pltpu.SemaphoreType.DMA((2,)), pltpu.SemaphoreType.DMA((2,))],
        ),
        compiler_params=_params(("arbitrary",)),
        name="experts",
    )(first_blk, blk_count, n_used, xs, wg, bg, wu, bu, wd, bd)


def _combine_kernel(base_ref, npass_ref, rows_ref, idx_ref, lr_ref, gate_ref, x1_ref, g_ref, o_ref,
                    wbuf, sem, *, tm):
    i = pl.program_id(0)
    slot = i % 2
    win = COMBINE_WINDOW
    width = N_EXPERTS * win

    def start_fetch(tile, p, s):
        for e in range(N_EXPERTS):
            start = pl.multiple_of(base_ref[tile * N_EXPERTS + e] + p * win, SUBLANES)
            pltpu.make_async_copy(rows_ref.at[pl.ds(start, win), :],
                                  wbuf.at[s, pl.ds(e * win, win), :], sem.at[s]).start()

    def wait_fetch(s):
        pltpu.make_async_copy(rows_ref.at[pl.ds(0, width), :], wbuf.at[s], sem.at[s]).wait()

    @pl.when(i == 0)
    def _():
        start_fetch(0, 0, 0)

    @pl.when(i + 1 < pl.num_programs(0))
    def _():
        start_fetch(i + 1, 0, 1 - slot)

    idx = idx_ref[...]
    lr = lr_ref[...]
    gates = gate_ref[...]
    col = lax.broadcasted_iota(jnp.int32, (tm, width), 1)

    def weights(p):
        w = jnp.zeros((tm, width), F32)
        for k in range(TOP_K):
            r = lr[:, k:k + 1] - p * win
            tgt = jnp.where((r >= 0) & (r < win), idx[:, k:k + 1] * win + r, -1)
            w = jnp.where(col == tgt, gates[:, k:k + 1], w)
        return w.astype(BF16)

    def gathered(p):
        return jnp.dot(weights(p), wbuf[slot].astype(BF16), preferred_element_type=F32)

    wait_fetch(slot)
    y = x1_ref[...] + gathered(0)

    def extra_pass(p, acc):
        start_fetch(i, p, slot)
        wait_fetch(slot)
        return acc + gathered(p)

    y = lax.fori_loop(1, npass_ref[i], extra_pass, y)
    ms = jnp.mean(y * y, axis=-1, keepdims=True)
    o_ref[...] = y * lax.rsqrt(ms + NORM_EPS) * g_ref[...]


def _combine(base_tbl, npass, rows, idx_t, lr_t, gates_t, x1, g, tm):
    t, d = x1.shape
    kern = functools.partial(_combine_kernel, tm=tm)
    tok = lambda i, *_: (i, 0)
    return pl.pallas_call(
        kern,
        out_shape=jax.ShapeDtypeStruct((t, d), F32),
        grid_spec=pltpu.PrefetchScalarGridSpec(
            num_scalar_prefetch=2,
            grid=(t // tm,),
            in_specs=[pl.BlockSpec(memory_space=pl.ANY),
                      pl.BlockSpec((tm, TOP_K), tok), pl.BlockSpec((tm, TOP_K), tok),
                      pl.BlockSpec((tm, TOP_K), tok), pl.BlockSpec((tm, d), tok),
                      pl.BlockSpec((1, d), lambda i, *_: (0, 0))],
            out_specs=pl.BlockSpec((tm, d), tok),
            scratch_shapes=[pltpu.VMEM((2, N_EXPERTS * COMBINE_WINDOW, d), F32),
                            pltpu.SemaphoreType.DMA((2,))],
        ),
        compiler_params=_params(("arbitrary",)),
        name="combine",
    )(base_tbl, npass, rows, idx_t, lr_t, gates_t, x1, g)


def _rope_tables(seq):
    inv = 1.0 / (ROPE_THETA ** (jnp.arange(0, HEAD_DIM, 2, dtype=F32) / HEAD_DIM))
    ang = jnp.arange(seq, dtype=F32)[:, None] * inv[None, :]
    cos, sin = jnp.cos(ang), jnp.sin(ang)
    return jnp.concatenate([cos] * 4, axis=1), jnp.concatenate([-sin, sin, -sin, sin], axis=1)


def kernel(x, norm_mix_g, w_in, lambda_q1, lambda_k1, lambda_q2, lambda_k2, da_subln_g, w_da_out, w_sb_out, w_o, norm_ffn_g, w_router, b_router, w_gate, b_gate, w_up, b_up, w_down, b_down, norm_final_g):
    b, s, d = x.shape
    depth = w_in.shape[0]
    t = b * s
    cos_t, sin_t = _rope_tables(s)
    n_blocks = (t * TOP_K + ROW_BLOCK - 1) // ROW_BLOCK + N_EXPERTS
    p_rows = n_blocks * ROW_BLOCK
    x2 = x.reshape(t, d)
    for l in range(depth):
        lambda_init = 0.8 - 0.6 * math.exp(-0.3 * l)
        qa, ka, va, qb, kb, vb, ga, gb = _in_proj(
            x2, norm_mix_g[l][None, :], w_in[l].astype(BF16), cos_t, sin_t, s)
        lam_p = jnp.stack([lambda_q1[l], lambda_k1[l], lambda_q2[l], lambda_k2[l]]).astype(F32)
        seq3 = lambda a: a.reshape(b, s, a.shape[1])
        oa = _diff_attn(seq3(qa), seq3(ka), seq3(va), lam_p, da_subln_g[l][None, :].astype(F32), lambda_init)
        ob = _sb_attn(seq3(qb), seq3(kb), seq3(vb))
        wr = w_router[l].astype(F32)
        wr_top = wr.astype(BF16)
        wr_rest = (wr - wr_top.astype(F32)).astype(BF16)
        lane_pad = lambda a: jnp.pad(a, ((0, 0), (0, LANES - a.shape[1])))
        wr_hi = lane_pad(jnp.concatenate([wr_top, wr_rest], axis=1))
        wr_lo = lane_pad(wr_top)
        x1, h, idx, gates, rank, cnt = _post_attn(
            oa.reshape(t, -1), ob.reshape(t, -1), ga, gb, x2,
            w_da_out[l].astype(BF16), w_sb_out[l].astype(BF16), w_o[l].astype(BF16),
            norm_ffn_g[l][None, :], wr_hi, wr_lo, b_router[l][:, None].astype(F32))
        counts = cnt[:, 0].astype(jnp.int32)
        padded = (counts + ROW_BLOCK - 1) // ROW_BLOCK * ROW_BLOCK
        pad_ends = jnp.cumsum(padded)
        pad_starts = pad_ends - padded
        experts = jnp.arange(N_EXPERTS, dtype=jnp.int32)
        chosen = idx[:, :, None] == experts
        base = jnp.sum(jnp.where(chosen, pad_starts, 0), axis=-1)
        dest = (base + rank).reshape(-1)
        n_used = (pad_ends[-1] // ROW_BLOCK).astype(jnp.int32)
        first_blk = (pad_starts // ROW_BLOCK).astype(jnp.int32)
        blk_count = (padded // ROW_BLOCK).astype(jnp.int32)
        zero_blk = jnp.maximum(pad_ends // ROW_BLOCK - 1, 0).astype(jnp.int32)
        n_tiles = t // COMBINE_TILE
        tile_cnt = jnp.sum(chosen.reshape(TOP_K, n_tiles, COMBINE_TILE, N_EXPERTS), axis=(0, 2), dtype=jnp.int32)
        tile_carry = jnp.cumsum(tile_cnt, axis=0) - tile_cnt
        run_start = pad_starts[None, :] + tile_carry
        run_skew = run_start % SUBLANES
        run_base = (run_start - run_skew).reshape(-1).astype(jnp.int32)
        shift_tok = jnp.repeat(tile_carry - run_skew, COMBINE_TILE, axis=0)
        local_rank = rank - jnp.sum(jnp.where(chosen, shift_tok[None], 0), axis=-1)
        n_pass = jnp.maximum((jnp.max(tile_cnt + run_skew, axis=1) + COMBINE_WINDOW - 1) // COMBINE_WINDOW,
                             1).astype(jnp.int32)
        xs = _dispatch(dest, zero_blk, n_used[None], h, p_rows)
        rows = _experts(first_blk, blk_count, n_used[None], n_blocks + 1, xs, w_gate[l], b_gate[l][:, None, :],
                        w_up[l], b_up[l][:, None, :], w_down[l], b_down[l][:, None, :])
        g_next = norm_final_g[None, :] if l == depth - 1 else jnp.ones((1, d), F32)
        x2 = _combine(run_base, n_pass, rows, idx.T, local_rank.T.astype(jnp.int32), gates.T, x1, g_next,
                      COMBINE_TILE)
        if l != depth - 1:
            raise NotImplementedError("only the final layer's norm is fused into the combine kernel")
    return x2.reshape(b, s, d)
```

```python
import functools
import math

import jax
import jax.numpy as jnp
from jax import lax
from jax.experimental import pallas as pl
from jax.experimental.pallas import tpu as pltpu

F32 = jnp.float32
BF16 = jnp.bfloat16

DA_HEADS = 4
HEAD_DIM = 64
N_EXPERTS = 32
TOP_K = 4
ROPE_THETA = 10000.0
SWIGLU_LIMIT = 7.0
SWIGLU_ALPHA = 1.702
NORM_EPS = 1e-5
ROW_BLOCK = 256
COMBINE_TILE = 256
WINDOW_ALIGN = 16
COMBINE_WINDOW = 80
LANES = 128
NEG_BIG = -1e30
LOG2E = math.log2(math.e)
SB_DEAD_LOG2 = -150.0

VMEM_LIMIT = 56 * 1024 * 1024


def _params(sem, vmem=VMEM_LIMIT):
    return pltpu.CompilerParams(dimension_semantics=sem, vmem_limit_bytes=vmem)


def _in_proj_kernel(x_ref, g_ref, w_ref, cos_ref, sin_ref,
                    qa_ref, ka_ref, va_ref, qb_ref, kb_ref, vb_ref, ga_ref, gb_ref):
    x = x_ref[...]
    ms = jnp.mean(x * x, axis=-1, keepdims=True)
    h = (x * lax.rsqrt(ms + NORM_EPS) * g_ref[...]).astype(BF16)
    cos = cos_ref[...]
    sin = sin_ref[...]
    lane = lax.broadcasted_iota(jnp.int32, cos.shape, 1)
    first_half = (lane & (HEAD_DIM - 1)) < HEAD_DIM // 2

    def proj(c0, width):
        return jnp.dot(h, w_ref[:, c0:c0 + width], preferred_element_type=F32)

    def rope(r):
        outs = []
        for g in range(r.shape[1] // LANES):
            xg = r[:, g * LANES:(g + 1) * LANES]
            rot = jnp.where(first_half, pltpu.roll(xg, LANES - HEAD_DIM // 2, 1),
                            pltpu.roll(xg, HEAD_DIM // 2, 1))
            outs.append(xg * cos + rot * sin)
        return jnp.concatenate(outs, axis=1)

    scale = HEAD_DIM ** -0.5 * LOG2E
    w = qa_ref.shape[1]
    d = ga_ref.shape[1]
    qa_ref[...] = (rope(proj(0, w)) * scale).astype(BF16)
    ka_ref[...] = rope(proj(w, w)).astype(BF16)
    va_ref[...] = proj(2 * w, w).astype(BF16)
    qb_ref[...] = (proj(3 * w, w) * scale).astype(BF16)
    kb_ref[...] = proj(4 * w, w).astype(BF16)
    vb_ref[...] = proj(5 * w, w).astype(BF16)
    ga_ref[...] = jax.nn.sigmoid(proj(6 * w, d)).astype(BF16)
    gb_ref[...] = jax.nn.sigmoid(proj(6 * w + d, d)).astype(BF16)


def _in_proj(x2, g, w_in_bf, cos_t, sin_t, seq, tm=512):
    t, d = x2.shape
    w = 512
    nseq = seq // tm
    outs = [jax.ShapeDtypeStruct((t, w), BF16)] * 6 + [jax.ShapeDtypeStruct((t, d), BF16)] * 2
    row = lambda i: (i, 0)
    return pl.pallas_call(
        _in_proj_kernel,
        out_shape=outs,
        grid=(t // tm,),
        in_specs=[
            pl.BlockSpec((tm, d), row),
            pl.BlockSpec((1, d), lambda i: (0, 0)),
            pl.BlockSpec(w_in_bf.shape, lambda i: (0, 0)),
            pl.BlockSpec((tm, LANES), lambda i: (i % nseq, 0)),
            pl.BlockSpec((tm, LANES), lambda i: (i % nseq, 0)),
        ],
        out_specs=[pl.BlockSpec((tm, w), row)] * 6 + [pl.BlockSpec((tm, d), row)] * 2,
        compiler_params=_params(("arbitrary",)),
        name="in_proj",
    )(x2, g, w_in_bf, cos_t, sin_t)


def _lane_groups(x):
    return [x[:, g * LANES:(g + 1) * LANES] for g in range(x.shape[1] // LANES)]


def _da_kernel(q_ref, k_ref, v_ref, lam_ref, subg_ref, o_ref, s_buf, mx_ref, m_ref, acc_ref,
               *, tq, tk, lambda_init):
    i = pl.program_id(2)
    q = q_ref[0]
    lane = lax.broadcasted_iota(jnp.int32, q.shape, 1)
    zero = jnp.zeros_like(q)
    qs = (jnp.where(lane < HEAD_DIM, q, zero), jnp.where(lane >= HEAD_DIM, q, zero))
    mx_ref[...] = jnp.full(mx_ref.shape, NEG_BIG, F32)
    acc_ref[...] = jnp.zeros(acc_ref.shape, F32)

    def score_blocks(js, diag_last):
        maxes = [mx_ref[0], mx_ref[1]]
        for n, j in enumerate(js):
            k = k_ref[0, pl.ds(pl.multiple_of(j * tk, tk), tk), :]
            for h in range(2):
                s = lax.dot_general(qs[h], k, (((1,), (1,)), ((), ())), preferred_element_type=F32)
                if diag_last and n == len(js) - 1:
                    rows = lax.broadcasted_iota(jnp.int32, s.shape, 0)
                    cols = lax.broadcasted_iota(jnp.int32, s.shape, 1)
                    s = jnp.where(rows >= cols, s, NEG_BIG)
                s_buf[h, j] = s
                for sg in _lane_groups(s):
                    maxes[h] = jnp.maximum(maxes[h], sg)
        mx_ref[0], mx_ref[1] = maxes

    def score_body(jj, carry):
        score_blocks([2 * jj, 2 * jj + 1], False)
        return carry

    lax.fori_loop(0, i // 2, score_body, 0)

    @pl.when(i % 2 == 0)
    def _():
        score_blocks([i], True)

    @pl.when(i % 2 == 1)
    def _():
        score_blocks([i - 1, i], True)

    for h in range(2):
        m_ref[h] = jnp.broadcast_to(jnp.max(mx_ref[h], axis=1, keepdims=True), (tq, LANES))

    def pv_blocks(js):
        vs = [v_ref[0, pl.ds(pl.multiple_of(j * tk, tk), tk), :] for j in js]
        v_all = jnp.concatenate(vs, axis=0)
        v_ext = jnp.concatenate([v_all, jnp.ones_like(v_all)], axis=1)
        accs = [acc_ref[0], acc_ref[1]]
        for h in range(2):
            m = m_ref[h]
            p = jnp.concatenate([jnp.exp2(sg - m).astype(BF16)
                                 for j in js for sg in _lane_groups(s_buf[h, j])], axis=1)
            accs[h] = accs[h] + jnp.dot(p, v_ext, preferred_element_type=F32)
        acc_ref[0], acc_ref[1] = accs

    def pv_body(jj, carry):
        pv_blocks([2 * jj, 2 * jj + 1])
        return carry

    lax.fori_loop(0, (i + 1) // 2, pv_body, 0)

    @pl.when(i % 2 == 0)
    def _():
        pv_blocks([i])

    lam_p = lam_ref[...]
    lam = (jnp.exp(jnp.sum(lam_p[0:1] * lam_p[1:2], axis=1, keepdims=True))
           - jnp.exp(jnp.sum(lam_p[2:3] * lam_p[3:4], axis=1, keepdims=True)) + lambda_init)
    a0, a1 = acc_ref[0], acc_ref[1]
    o = a0[:, :LANES] / a0[:, LANES:] - lam * (a1[:, :LANES] / a1[:, LANES:])
    ms = jnp.mean(o * o, axis=-1, keepdims=True)
    y = o * lax.rsqrt(ms + NORM_EPS) * subg_ref[...] * (1.0 - lambda_init)
    o_ref[0] = y.astype(o_ref.dtype)


def _diff_attn(qa, ka, va, lam_p, subg, lambda_init, tq=512):
    b, s, w = qa.shape
    nh = w // LANES
    kern = functools.partial(_da_kernel, tq=tq, tk=tq, lambda_init=lambda_init)
    qspec = pl.BlockSpec((1, tq, LANES), lambda bb, h, i: (bb, i, h))
    kvspec = pl.BlockSpec((1, s, LANES), lambda bb, h, i: (bb, 0, h))
    return pl.pallas_call(
        kern,
        out_shape=jax.ShapeDtypeStruct((b, s, w), BF16),
        grid=(b, nh, s // tq),
        in_specs=[qspec, kvspec, kvspec,
                  pl.BlockSpec(lam_p.shape, lambda bb, h, i: (0, 0)),
                  pl.BlockSpec(subg.shape, lambda bb, h, i: (0, 0))],
        out_specs=qspec,
        scratch_shapes=[pltpu.VMEM((2, s // tq, tq, tq), F32),
                        pltpu.VMEM((2, tq, LANES), F32),
                        pltpu.VMEM((2, tq, LANES), F32),
                        pltpu.VMEM((2, tq, 2 * LANES), F32)],
        compiler_params=_params(("arbitrary",) * 3),
        name="diff_attn",
    )(qa, ka, va, lam_p, subg)


def _sb_kernel(q_ref, k_ref, v_ref, tri_ref, o_ref, c_ref, acc_ref, *, tq, tk):
    i = pl.program_id(2)
    n_tiles = tq // tk
    tri = tri_ref[...]
    lane = lax.broadcasted_iota(jnp.int32, (tk, LANES), 1)
    qs = []
    for r in range(n_tiles):
        q = q_ref[0, r * tk:(r + 1) * tk, :]
        zero = jnp.zeros_like(q)
        qs.append((jnp.where(lane < HEAD_DIM, q, zero), jnp.where(lane >= HEAD_DIM, q, zero)))
    c_ref[...] = jnp.zeros(c_ref.shape, F32)
    acc_ref[...] = jnp.zeros(acc_ref.shape, F32)

    def load_kv(j):
        rows = pl.ds(pl.multiple_of(j * tk, tk), tk)
        return k_ref[0, rows, :], v_ref[0, rows, :]

    def window(jobs):
        keys = [(r, h) for r, _, _ in jobs for h in range(2)]
        state = {key: (c_ref[key], acc_ref[key]) for key in keys}
        new_state = {}
        for r, kvs, diag_first in jobs:
            v_all = jnp.concatenate([v for _, v in kvs], axis=0)
            for h in range(2):
                c_run, acc_prev = state[(r, h)]
                weights = []
                for n, (k, _) in enumerate(kvs):
                    masked = diag_first and n == 0
                    z = lax.dot_general(qs[r][h], k, (((1,), (1,)), ((), ())), preferred_element_type=F32)
                    soft = jnp.maximum(z, 0.0) + jnp.log2(1.0 + jnp.exp2(-jnp.abs(z)))
                    log_b = z - soft
                    if masked:
                        mask = (lax.broadcasted_iota(jnp.int32, z.shape, 1)
                                < lax.broadcasted_iota(jnp.int32, z.shape, 0))
                        soft = jnp.where(mask, soft, 0.0)
                    res = jnp.dot(soft.astype(BF16), tri, preferred_element_type=F32)
                    suffix, total = res[:, :tk], res[:, tk:]
                    a = jnp.concatenate([jnp.exp2(lb + sf + c_run)
                                         for lb, sf in zip(_lane_groups(log_b), _lane_groups(suffix))], axis=1)
                    if masked:
                        a = jnp.where(mask, a, 0.0)
                    weights.append(a.astype(BF16))
                    c_run = c_run + total
                a_all = jnp.concatenate(weights, axis=1)
                new_state[(r, h)] = (c_run, acc_prev + jnp.dot(a_all, v_all, preferred_element_type=F32))
        for key in keys:
            c_ref[key], acc_ref[key] = new_state[key]

    first = i * n_tiles

    @pl.when(i == 0)
    def _():
        blocks = [load_kv(r) for r in range(n_tiles)]
        window([(r, blocks[max(r - 1, 0):r + 1][::-1], True) for r in range(n_tiles)])

    @pl.when(i > 0)
    def _():
        blocks = [load_kv(first - 1 + n) for n in range(n_tiles + 1)]
        window([(r, [blocks[r + 1], blocks[r]], True) for r in range(n_tiles)])

    for r in range(n_tiles):
        def live(r=r):
            return jnp.max(jnp.maximum(c_ref[r, 0], c_ref[r, 1])) > SB_DEAD_LOG2

        def cond(carry):
            j, alive = carry
            return jnp.logical_and(j >= 0, alive)

        def body(carry, r=r, live=live):
            j, _ = carry
            window([(r, [load_kv(j)], False)])
            return j - 1, live()

        lax.while_loop(cond, body, (first + r - 2, live()))
        o_ref[0, r * tk:(r + 1) * tk, :] = jnp.where(lane < HEAD_DIM, acc_ref[r, 0],
                                                     acc_ref[r, 1]).astype(o_ref.dtype)


def _sb_attn(qb, kb, vb, tq=512, tk=256):
    b, s, w = qb.shape
    nh = w // LANES
    r = lax.broadcasted_iota(jnp.int32, (tk, tk), 0)
    c = lax.broadcasted_iota(jnp.int32, (tk, tk), 1)
    tri = -jnp.concatenate([(r > c).astype(BF16), jnp.ones((tk, LANES), BF16)], axis=1)
    kern = functools.partial(_sb_kernel, tq=tq, tk=tk)
    qspec = pl.BlockSpec((1, tq, LANES), lambda bb, h, i: (bb, i, h))
    kvspec = pl.BlockSpec((1, s, LANES), lambda bb, h, i: (bb, 0, h))
    return pl.pallas_call(
        kern,
        out_shape=jax.ShapeDtypeStruct((b, s, w), BF16),
        grid=(b, nh, s // tq),
        in_specs=[qspec, kvspec, kvspec, pl.BlockSpec(tri.shape, lambda bb, h, i: (0, 0))],
        out_specs=qspec,
        scratch_shapes=[pltpu.VMEM((tq // tk, 2, tk, LANES), F32), pltpu.VMEM((tq // tk, 2, tk, LANES), F32)],
        compiler_params=_params(("arbitrary",) * 3),
        name="sb_attn",
    )(qb, kb, vb, tri)


def _post_attn_kernel(oa_ref, ob_ref, ga_ref, gb_ref, x_ref, wda_ref, wsb_ref, wo_ref, g_ref,
                      wr_hi_ref, wr_lo_ref, br_ref, tri_ref,
                      x1_ref, h_ref, idx_ref, gate_ref, rank_ref, cnt_ref, carry_ref):
    i = pl.program_id(0)

    @pl.when(i == 0)
    def _():
        carry_ref[...] = jnp.zeros(carry_ref.shape, F32)

    ya = jnp.dot(oa_ref[...], wda_ref[...], preferred_element_type=F32)
    yb = jnp.dot(ob_ref[...], wsb_ref[...], preferred_element_type=F32)
    mix = ga_ref[...].astype(F32) * ya + gb_ref[...].astype(F32) * yb
    x1 = x_ref[...] + jnp.dot(mix.astype(BF16), wo_ref[...], preferred_element_type=F32)
    x1_ref[...] = x1
    ms = jnp.mean(x1 * x1, axis=-1, keepdims=True)
    h = x1 * lax.rsqrt(ms + NORM_EPS) * g_ref[...]
    h_ref[...] = h

    h_hi = h.astype(BF16)
    h_lo = (h - h_hi.astype(F32)).astype(BF16)
    ne = br_ref.shape[0]
    p_hi = jnp.dot(h_hi, wr_hi_ref[...], preferred_element_type=F32)
    p_lo = jnp.dot(h_lo, wr_lo_ref[...], preferred_element_type=F32)
    by_token = p_hi + pltpu.roll(p_hi, LANES - ne, 1) + p_lo
    logits = by_token.T[:ne] + br_ref[...]
    tm = logits.shape[1]
    eid = lax.broadcasted_iota(jnp.int32, (ne, tm), 0).astype(F32)
    vals, ids = [], []
    work = logits
    for _ in range(TOP_K):
        mx = jnp.max(work, axis=0, keepdims=True)
        sel = jnp.min(jnp.where(work == mx, eid, float(ne)), axis=0, keepdims=True)
        vals.append(mx)
        ids.append(sel)
        work = jnp.where(eid == sel, -jnp.inf, work)
    exps = [jnp.exp(v - vals[0]) for v in vals]
    denom = exps[0] + exps[1] + exps[2] + exps[3]
    onehots = [(eid == sel).astype(F32) for sel in ids]
    assigned = onehots[0] + onehots[1] + onehots[2] + onehots[3]
    before = jnp.dot(assigned.astype(BF16), tri_ref[...], preferred_element_type=F32) + carry_ref[...]
    for r in range(TOP_K):
        idx_ref[r:r + 1, :] = ids[r].astype(jnp.int32)
        gate_ref[r:r + 1, :] = exps[r] / denom
        rank_ref[r:r + 1, :] = jnp.sum(onehots[r] * before, axis=0, keepdims=True).astype(jnp.int32)
    carry = carry_ref[...] + jnp.sum(assigned, axis=1, keepdims=True)
    carry_ref[...] = carry
    cnt_ref[...] = jnp.broadcast_to(carry, cnt_ref.shape)


def _post_attn(oa, ob, ga, gb, x2, wda, wsb, wo, g, wr_hi, wr_lo, br, tm=512):
    t, d = x2.shape
    w = oa.shape[1]
    r = lax.broadcasted_iota(jnp.int32, (tm, tm), 0)
    c = lax.broadcasted_iota(jnp.int32, (tm, tm), 1)
    tri = (r < c).astype(BF16)
    row = lambda i: (i, 0)
    col = lambda i: (0, i)
    const = lambda i: (0, 0)
    full = lambda a: pl.BlockSpec(a.shape, const)
    return pl.pallas_call(
        _post_attn_kernel,
        out_shape=[jax.ShapeDtypeStruct((t, d), F32), jax.ShapeDtypeStruct((t, d), F32),
                   jax.ShapeDtypeStruct((TOP_K, t), jnp.int32), jax.ShapeDtypeStruct((TOP_K, t), F32),
                   jax.ShapeDtypeStruct((TOP_K, t), jnp.int32),
                   jax.ShapeDtypeStruct((N_EXPERTS, LANES), F32)],
        grid=(t // tm,),
        in_specs=[pl.BlockSpec((tm, w), row), pl.BlockSpec((tm, w), row),
                  pl.BlockSpec((tm, d), row), pl.BlockSpec((tm, d), row), pl.BlockSpec((tm, d), row),
                  full(wda), full(wsb), full(wo), full(g), full(wr_hi), full(wr_lo), full(br), full(tri)],
        out_specs=[pl.BlockSpec((tm, d), row), pl.BlockSpec((tm, d), row),
                   pl.BlockSpec((TOP_K, tm), col), pl.BlockSpec((TOP_K, tm), col),
                   pl.BlockSpec((TOP_K, tm), col), pl.BlockSpec((N_EXPERTS, LANES), const)],
        scratch_shapes=[pltpu.VMEM((N_EXPERTS, 1), F32)],
        compiler_params=_params(("arbitrary",)),
        name="post_attn",
    )(oa, ob, ga, gb, x2, wda, wsb, wo, g, wr_hi, wr_lo, br, tri)


def _dispatch_kernel(dest_ref, zero_blk_ref, n_used_ref, h_ref, xs_ref, zeros, sem, zsem, *, tm, t_total):
    i = pl.program_id(0)
    n_blocks = xs_ref.shape[0] // ROW_BLOCK

    def zero_copy(blk):
        row = pl.multiple_of(blk * ROW_BLOCK, ROW_BLOCK)
        return pltpu.make_async_copy(zeros, xs_ref.at[pl.ds(row, ROW_BLOCK), :], zsem)

    @pl.when(i == 0)
    def _():
        zeros[...] = jnp.zeros(zeros.shape, zeros.dtype)
        n_tail = n_blocks - n_used_ref[0]

        def start(n, carry):
            zero_copy(jnp.where(n < N_EXPERTS, zero_blk_ref[jnp.minimum(n, N_EXPERTS - 1)],
                                n_used_ref[0] + n - N_EXPERTS)).start()
            return carry

        def wait(n, carry):
            zero_copy(0).wait()
            return carry

        lax.fori_loop(0, N_EXPERTS + n_tail, start, 0)
        lax.fori_loop(0, N_EXPERTS + n_tail, wait, 0)

    def row_copy(t, k):
        dst = dest_ref[k * t_total + i * tm + t]
        return pltpu.make_async_copy(h_ref.at[pl.ds(t, 1), :], xs_ref.at[pl.ds(dst, 1), :], sem)

    def issue(t, carry):
        for k in range(TOP_K):
            row_copy(t, k).start(priority=k % 2)
        return carry

    lax.fori_loop(0, tm, issue, 0, unroll=8)
    for _ in range(TOP_K):
        pltpu.make_async_copy(h_ref, xs_ref.at[pl.ds(0, tm), :], sem).wait()


def _dispatch(dest_flat, zero_blk, n_used, h, p_rows, tm=256):
    t, d = h.shape
    kern = functools.partial(_dispatch_kernel, tm=tm, t_total=t)
    return pl.pallas_call(
        kern,
        out_shape=jax.ShapeDtypeStruct((p_rows, d), h.dtype),
        grid_spec=pltpu.PrefetchScalarGridSpec(
            num_scalar_prefetch=3,
            grid=(t // tm,),
            in_specs=[pl.BlockSpec((tm, d), lambda i, *_: (i, 0))],
            out_specs=pl.BlockSpec(memory_space=pl.ANY),
            scratch_shapes=[pltpu.VMEM((ROW_BLOCK, d), h.dtype), pltpu.SemaphoreType.DMA,
                            pltpu.SemaphoreType.DMA],
        ),
        compiler_params=_params(("arbitrary",)),
        name="dispatch",
    )(dest_flat, zero_blk, n_used, h)


def _expert_kernel(first_ref, count_ref, n_used_ref, xs_ref, wg_ref, bg_ref, wu_ref, bu_ref, wd_ref, bd_ref,
                   o_ref, wg_bf, wu_bf, wd_bf, xbuf, obuf, in_sem, out_sem):
    e = pl.program_id(0)
    first = first_ref[e]
    count = count_ref[e]

    n_pairs = count // 2
    odd = count % 2

    def rows(blk, nb):
        return pl.ds(pl.multiple_of(blk * ROW_BLOCK, ROW_BLOCK), nb * ROW_BLOCK)

    def fetch(blk, nb, slot):
        return pltpu.make_async_copy(xs_ref.at[rows(blk, nb), :], xbuf.at[slot, pl.ds(0, nb * ROW_BLOCK), :],
                                     in_sem.at[slot])

    def flush(blk, nb, slot):
        return pltpu.make_async_copy(obuf.at[slot, pl.ds(0, nb * ROW_BLOCK), :], o_ref.at[rows(blk, nb), :],
                                     out_sem.at[slot])

    def mlp(x):
        x = x.astype(BF16)
        g = jnp.dot(x, wg_bf[...], preferred_element_type=F32) + bg_ref[0]
        u = jnp.dot(x, wu_bf[...], preferred_element_type=F32) + bu_ref[0]
        g = jnp.minimum(g, SWIGLU_LIMIT)
        u = jnp.clip(u, -SWIGLU_LIMIT, SWIGLU_LIMIT)
        glu = g * jax.nn.sigmoid(SWIGLU_ALPHA * g)
        act = ((u + 1.0) * glu).astype(BF16)
        return (jnp.dot(act, wd_bf[...], preferred_element_type=F32) + bd_ref[0]).astype(obuf.dtype)

    @pl.when(count > 0)
    def _():
        @pl.when(n_pairs > 0)
        def _():
            fetch(first, 2, 0).start()

        @pl.when(n_pairs == 0)
        def _():
            fetch(first, 1, 0).start()

        wg_bf[...] = wg_ref[0].astype(BF16)
        wu_bf[...] = wu_ref[0].astype(BF16)
        wd_bf[...] = wd_ref[0].astype(BF16)

    def pair(jj, carry):
        slot = jj % 2
        blk = first + 2 * jj
        fetch(blk, 2, slot).wait()

        @pl.when(jj + 1 < n_pairs)
        def _():
            fetch(blk + 2, 2, 1 - slot).start()

        @pl.when((jj + 1 == n_pairs) & (odd == 1))
        def _():
            fetch(blk + 2, 1, 1 - slot).start()

        y = mlp(xbuf[slot])

        @pl.when(jj >= 2)
        def _():
            flush(blk - 4, 2, slot).wait()

        obuf[slot] = y
        flush(blk, 2, slot).start()
        return carry

    lax.fori_loop(0, n_pairs, pair, 0)
    last = first + 2 * n_pairs
    last_slot = n_pairs % 2

    @pl.when(odd == 1)
    def _():
        fetch(last, 1, last_slot).wait()
        y = mlp(xbuf[last_slot, :ROW_BLOCK, :])

        @pl.when(n_pairs >= 2)
        def _():
            flush(last - 4, 2, last_slot).wait()

        obuf[last_slot, :ROW_BLOCK, :] = y
        flush(last, 1, last_slot).start()

    @pl.when(n_pairs >= 1)
    def _():
        flush(last - 2, 2, 1 - last_slot).wait()

    @pl.when((n_pairs >= 2) & (odd == 0))
    def _():
        flush(last - 4, 2, last_slot).wait()

    @pl.when(odd == 1)
    def _():
        flush(last, 1, last_slot).wait()

    @pl.when(e == pl.num_programs(0) - 1)
    def _():
        obuf[0, :ROW_BLOCK, :] = jnp.zeros((ROW_BLOCK, obuf.shape[2]), obuf.dtype)
        n_used = n_used_ref[0]
        n_total = o_ref.shape[0] // ROW_BLOCK

        def start(blk, carry):
            flush(blk, 1, 0).start()
            return carry

        def wait(blk, carry):
            flush(blk, 1, 0).wait()
            return carry

        lax.fori_loop(n_used, n_total, start, 0)
        lax.fori_loop(n_used, n_total, wait, 0)


def _experts(first_blk, blk_count, n_used, n_out_blocks, xs, wg, bg, wu, bu, wd, bd):
    d = xs.shape[1]
    n_exp, _, f = wg.shape
    wmap = lambda e, *_: (e, 0, 0)
    return pl.pallas_call(
        _expert_kernel,
        out_shape=jax.ShapeDtypeStruct((n_out_blocks * ROW_BLOCK, d), BF16),
        grid_spec=pltpu.PrefetchScalarGridSpec(
            num_scalar_prefetch=3,
            grid=(n_exp,),
            in_specs=[pl.BlockSpec(memory_space=pl.ANY),
                      pl.BlockSpec((1, d, f), wmap), pl.BlockSpec((1, 1, f), wmap),
                      pl.BlockSpec((1, d, f), wmap), pl.BlockSpec((1, 1, f), wmap),
                      pl.BlockSpec((1, f, d), wmap), pl.BlockSpec((1, 1, d), wmap)],
            out_specs=pl.BlockSpec(memory_space=pl.ANY),
            scratch_shapes=[pltpu.VMEM((d, f), BF16), pltpu.VMEM((d, f), BF16), pltpu.VMEM((f, d), BF16),
                            pltpu.VMEM((2, 2 * ROW_BLOCK, d), F32), pltpu.VMEM((2, 2 * ROW_BLOCK, d), BF16),
                            pltpu.SemaphoreType.DMA((2,)), pltpu.SemaphoreType.DMA((2,))],
        ),
        compiler_params=_params(("arbitrary",)),
        name="experts",
    )(first_blk, blk_count, n_used, xs, wg, bg, wu, bu, wd, bd)


def _combine_kernel(base_ref, npass_ref, rows_ref, idx_ref, lr_ref, gate_ref, x1_ref, g_ref, o_ref,
                    wbuf, sem, *, tm):
    i = pl.program_id(0)
    slot = i % 2
    win = COMBINE_WINDOW
    width = N_EXPERTS * win

    def start_fetch(tile, p, s):
        for e in range(N_EXPERTS):
            start = pl.multiple_of(base_ref[tile * N_EXPERTS + e] + p * win, WINDOW_ALIGN)
            pltpu.make_async_copy(rows_ref.at[pl.ds(start, win), :],
                                  wbuf.at[s, pl.ds(e * win, win), :], sem.at[s]).start()

    def wait_fetch(s):
        pltpu.make_async_copy(rows_ref.at[pl.ds(0, width), :], wbuf.at[s], sem.at[s]).wait()

    @pl.when(i == 0)
    def _():
        start_fetch(0, 0, 0)

    @pl.when(i + 1 < pl.num_programs(0))
    def _():
        start_fetch(i + 1, 0, 1 - slot)

    idx = idx_ref[...]
    lr = lr_ref[...]
    gates = gate_ref[...]
    col = lax.broadcasted_iota(jnp.int32, (tm, width), 1)

    def weights(p):
        w = jnp.zeros((tm, width), F32)
        for k in range(TOP_K):
            r = lr[:, k:k + 1] - p * win
            tgt = jnp.where((r >= 0) & (r < win), idx[:, k:k + 1] * win + r, -1)
            w = jnp.where(col == tgt, gates[:, k:k + 1], w)
        return w.astype(BF16)

    def gathered(p):
        return jnp.dot(weights(p), wbuf[slot], preferred_element_type=F32)

    wait_fetch(slot)
    y = x1_ref[...] + gathered(0)

    def extra_pass(p, acc):
        start_fetch(i, p, slot)
        wait_fetch(slot)
        return acc + gathered(p)

    y = lax.fori_loop(1, npass_ref[i], extra_pass, y)
    ms = jnp.mean(y * y, axis=-1, keepdims=True)
    o_ref[...] = y * lax.rsqrt(ms + NORM_EPS) * g_ref[...]


def _combine(base_tbl, npass, rows, idx_t, lr_t, gates_t, x1, g, tm):
    t, d = x1.shape
    kern = functools.partial(_combine_kernel, tm=tm)
    tok = lambda i, *_: (i, 0)
    return pl.pallas_call(
        kern,
        out_shape=jax.ShapeDtypeStruct((t, d), F32),
        grid_spec=pltpu.PrefetchScalarGridSpec(
            num_scalar_prefetch=2,
            grid=(t // tm,),
            in_specs=[pl.BlockSpec(memory_space=pl.ANY),
                      pl.BlockSpec((tm, TOP_K), tok), pl.BlockSpec((tm, TOP_K), tok),
                      pl.BlockSpec((tm, TOP_K), tok), pl.BlockSpec((tm, d), tok),
                      pl.BlockSpec((1, d), lambda i, *_: (0, 0))],
            out_specs=pl.BlockSpec((tm, d), tok),
            scratch_shapes=[pltpu.VMEM((2, N_EXPERTS * COMBINE_WINDOW, d), rows.dtype),
                            pltpu.SemaphoreType.DMA((2,))],
        ),
        compiler_params=_params(("arbitrary",)),
        name="combine",
    )(base_tbl, npass, rows, idx_t, lr_t, gates_t, x1, g)


def _rope_tables(seq):
    inv = 1.0 / (ROPE_THETA ** (jnp.arange(0, HEAD_DIM, 2, dtype=F32) / HEAD_DIM))
    ang = jnp.arange(seq, dtype=F32)[:, None] * inv[None, :]
    cos, sin = jnp.cos(ang), jnp.sin(ang)
    return jnp.concatenate([cos] * 4, axis=1), jnp.concatenate([-sin, sin, -sin, sin], axis=1)


def kernel(x, norm_mix_g, w_in, lambda_q1, lambda_k1, lambda_q2, lambda_k2, da_subln_g, w_da_out, w_sb_out, w_o, norm_ffn_g, w_router, b_router, w_gate, b_gate, w_up, b_up, w_down, b_down, norm_final_g):
    b, s, d = x.shape
    depth = w_in.shape[0]
    t = b * s
    cos_t, sin_t = _rope_tables(s)
    n_blocks = (t * TOP_K + ROW_BLOCK - 1) // ROW_BLOCK + N_EXPERTS
    p_rows = n_blocks * ROW_BLOCK
    x2 = x.reshape(t, d)
    for l in range(depth):
        lambda_init = 0.8 - 0.6 * math.exp(-0.3 * l)
        qa, ka, va, qb, kb, vb, ga, gb = _in_proj(
            x2, norm_mix_g[l][None, :], w_in[l].astype(BF16), cos_t, sin_t, s)
        lam_p = jnp.stack([lambda_q1[l], lambda_k1[l], lambda_q2[l], lambda_k2[l]]).astype(F32)
        seq3 = lambda a: a.reshape(b, s, a.shape[1])
        oa = _diff_attn(seq3(qa), seq3(ka), seq3(va), lam_p, da_subln_g[l][None, :].astype(F32), lambda_init)
        ob = _sb_attn(seq3(qb), seq3(kb), seq3(vb))
        wr = w_router[l].astype(F32)
        wr_top = wr.astype(BF16)
        wr_rest = (wr - wr_top.astype(F32)).astype(BF16)
        lane_pad = lambda a: jnp.pad(a, ((0, 0), (0, LANES - a.shape[1])))
        wr_hi = lane_pad(jnp.concatenate([wr_top, wr_rest], axis=1))
        wr_lo = lane_pad(wr_top)
        x1, h, idx, gates, rank, cnt = _post_attn(
            oa.reshape(t, -1), ob.reshape(t, -1), ga, gb, x2,
            w_da_out[l].astype(BF16), w_sb_out[l].astype(BF16), w_o[l].astype(BF16),
            norm_ffn_g[l][None, :], wr_hi, wr_lo, b_router[l][:, None].astype(F32))
        counts = cnt[:, 0].astype(jnp.int32)
        padded = (counts + ROW_BLOCK - 1) // ROW_BLOCK * ROW_BLOCK
        pad_ends = jnp.cumsum(padded)
        pad_starts = pad_ends - padded
        experts = jnp.arange(N_EXPERTS, dtype=jnp.int32)
        chosen = idx[:, :, None] == experts
        base = jnp.sum(jnp.where(chosen, pad_starts, 0), axis=-1)
        dest = (base + rank).reshape(-1)
        n_used = (pad_ends[-1] // ROW_BLOCK).astype(jnp.int32)
        first_blk = (pad_starts // ROW_BLOCK).astype(jnp.int32)
        blk_count = (padded // ROW_BLOCK).astype(jnp.int32)
        zero_blk = jnp.maximum(pad_ends // ROW_BLOCK - 1, 0).astype(jnp.int32)
        n_tiles = t // COMBINE_TILE
        tile_cnt = jnp.sum(chosen.reshape(TOP_K, n_tiles, COMBINE_TILE, N_EXPERTS), axis=(0, 2), dtype=jnp.int32)
        tile_carry = jnp.cumsum(tile_cnt, axis=0) - tile_cnt
        run_start = pad_starts[None, :] + tile_carry
        run_skew = run_start % WINDOW_ALIGN
        run_base = (run_start - run_skew).reshape(-1).astype(jnp.int32)
        shift_tok = jnp.repeat(tile_carry - run_skew, COMBINE_TILE, axis=0)
        local_rank = rank - jnp.sum(jnp.where(chosen, shift_tok[None], 0), axis=-1)
        n_pass = jnp.maximum((jnp.max(tile_cnt + run_skew, axis=1) + COMBINE_WINDOW - 1) // COMBINE_WINDOW,
                             1).astype(jnp.int32)
        xs = _dispatch(dest, zero_blk, n_used[None], h, p_rows)
        rows = _experts(first_blk, blk_count, n_used[None], n_blocks + 1, xs, w_gate[l], b_gate[l][:, None, :],
                        w_up[l], b_up[l][:, None, :], w_down[l], b_down[l][:, None, :])
        g_next = norm_final_g[None, :] if l == depth - 1 else jnp.ones((1, d), F32)
        x2 = _combine(run_base, n_pass, rows, idx.T, local_rank.T.astype(jnp.int32), gates.T, x1, g_next,
                      COMBINE_TILE)
        if l != depth - 1:
            raise NotImplementedError("only the final layer's norm is fused into the combine kernel")
    return x2.reshape(b, s, d)
```

```python
import functools
import math

import jax
import jax.numpy as jnp
from jax import lax
from jax.experimental import pallas as pl
from jax.experimental.pallas import tpu as pltpu

F32 = jnp.float32
BF16 = jnp.bfloat16

DA_HEADS = 4
HEAD_DIM = 64
N_EXPERTS = 32
TOP_K = 4
ROPE_THETA = 10000.0
SWIGLU_LIMIT = 7.0
SWIGLU_ALPHA = 1.702
NORM_EPS = 1e-5
ROW_BLOCK = 256
COMBINE_TILE = 256
WINDOW_ALIGN = 16
COMBINE_WINDOW = 80
LANES = 128
NEG_BIG = -1e30
LOG2E = math.log2(math.e)
SB_DEAD_LOG2 = -150.0

VMEM_LIMIT = 56 * 1024 * 1024


def _params(sem, vmem=VMEM_LIMIT):
    return pltpu.CompilerParams(dimension_semantics=sem, vmem_limit_bytes=vmem)


def _in_proj_kernel(x_ref, g_ref, w_ref, cos_ref, sin_ref,
                    qa_ref, ka_ref, va_ref, qb_ref, kb_ref, vb_ref, ga_ref, gb_ref):
    x = x_ref[...]
    ms = jnp.mean(x * x, axis=-1, keepdims=True)
    h = (x * lax.rsqrt(ms + NORM_EPS) * g_ref[...]).astype(BF16)
    cos = cos_ref[...]
    sin = sin_ref[...]
    lane = lax.broadcasted_iota(jnp.int32, cos.shape, 1)
    first_half = (lane & (HEAD_DIM - 1)) < HEAD_DIM // 2

    def proj(c0, width):
        return jnp.dot(h, w_ref[:, c0:c0 + width], preferred_element_type=F32)

    def rope(r):
        outs = []
        for g in range(r.shape[1] // LANES):
            xg = r[:, g * LANES:(g + 1) * LANES]
            rot = jnp.where(first_half, pltpu.roll(xg, LANES - HEAD_DIM // 2, 1),
                            pltpu.roll(xg, HEAD_DIM // 2, 1))
            outs.append(xg * cos + rot * sin)
        return jnp.concatenate(outs, axis=1)

    scale = HEAD_DIM ** -0.5 * LOG2E
    w = qa_ref.shape[1]
    d = ga_ref.shape[1]
    qa_ref[...] = (rope(proj(0, w)) * scale).astype(BF16)
    ka_ref[...] = rope(proj(w, w)).astype(BF16)
    va_ref[...] = proj(2 * w, w).astype(BF16)
    qb_ref[...] = (proj(3 * w, w) * scale).astype(BF16)
    kb_ref[...] = proj(4 * w, w).astype(BF16)
    vb_ref[...] = proj(5 * w, w).astype(BF16)
    ga_ref[...] = jax.nn.sigmoid(proj(6 * w, d)).astype(BF16)
    gb_ref[...] = jax.nn.sigmoid(proj(6 * w + d, d)).astype(BF16)


def _in_proj(x2, g, w_in_bf, cos_t, sin_t, seq, tm=512):
    t, d = x2.shape
    w = 512
    nseq = seq // tm
    outs = [jax.ShapeDtypeStruct((t, w), BF16)] * 6 + [jax.ShapeDtypeStruct((t, d), BF16)] * 2
    row = lambda i: (i, 0)
    return pl.pallas_call(
        _in_proj_kernel,
        out_shape=outs,
        grid=(t // tm,),
        in_specs=[
            pl.BlockSpec((tm, d), row),
            pl.BlockSpec((1, d), lambda i: (0, 0)),
            pl.BlockSpec(w_in_bf.shape, lambda i: (0, 0)),
            pl.BlockSpec((tm, LANES), lambda i: (i % nseq, 0)),
            pl.BlockSpec((tm, LANES), lambda i: (i % nseq, 0)),
        ],
        out_specs=[pl.BlockSpec((tm, w), row)] * 6 + [pl.BlockSpec((tm, d), row)] * 2,
        compiler_params=_params(("arbitrary",)),
        name="in_proj",
    )(x2, g, w_in_bf, cos_t, sin_t)


def _lane_groups(x):
    return [x[:, g * LANES:(g + 1) * LANES] for g in range(x.shape[1] // LANES)]


def _da_kernel(q_ref, k_ref, v_ref, lam_ref, subg_ref, o_ref, s_buf, mx_ref, m_ref, acc_ref,
               *, tq, tk, lambda_init):
    i = pl.program_id(2)
    q = q_ref[0]
    lane = lax.broadcasted_iota(jnp.int32, q.shape, 1)
    zero = jnp.zeros_like(q)
    qs = (jnp.where(lane < HEAD_DIM, q, zero), jnp.where(lane >= HEAD_DIM, q, zero))
    mx_ref[...] = jnp.full(mx_ref.shape, NEG_BIG, F32)
    acc_ref[...] = jnp.zeros(acc_ref.shape, F32)

    def score_blocks(js, diag_last):
        maxes = [mx_ref[0], mx_ref[1]]
        for n, j in enumerate(js):
            k = k_ref[0, pl.ds(pl.multiple_of(j * tk, tk), tk), :]
            for h in range(2):
                s = lax.dot_general(qs[h], k, (((1,), (1,)), ((), ())), preferred_element_type=F32)
                if diag_last and n == len(js) - 1:
                    rows = lax.broadcasted_iota(jnp.int32, s.shape, 0)
                    cols = lax.broadcasted_iota(jnp.int32, s.shape, 1)
                    s = jnp.where(rows >= cols, s, NEG_BIG)
                s_buf[h, j] = s
                for sg in _lane_groups(s):
                    maxes[h] = jnp.maximum(maxes[h], sg)
        mx_ref[0], mx_ref[1] = maxes

    def score_body(jj, carry):
        score_blocks([2 * jj, 2 * jj + 1], False)
        return carry

    lax.fori_loop(0, i // 2, score_body, 0)

    @pl.when(i % 2 == 0)
    def _():
        score_blocks([i], True)

    @pl.when(i % 2 == 1)
    def _():
        score_blocks([i - 1, i], True)

    for h in range(2):
        m_ref[h] = jnp.broadcast_to(jnp.max(mx_ref[h], axis=1, keepdims=True), (tq, LANES))

    def pv_blocks(js):
        vs = [v_ref[0, pl.ds(pl.multiple_of(j * tk, tk), tk), :] for j in js]
        v_all = jnp.concatenate(vs, axis=0)
        v_ext = jnp.concatenate([v_all, jnp.ones_like(v_all)], axis=1)
        accs = [acc_ref[0], acc_ref[1]]
        for h in range(2):
            m = m_ref[h]
            p = jnp.concatenate([jnp.exp2(sg - m).astype(BF16)
                                 for j in js for sg in _lane_groups(s_buf[h, j])], axis=1)
            accs[h] = accs[h] + jnp.dot(p, v_ext, preferred_element_type=F32)
        acc_ref[0], acc_ref[1] = accs

    def pv_body(jj, carry):
        pv_blocks([2 * jj, 2 * jj + 1])
        return carry

    lax.fori_loop(0, (i + 1) // 2, pv_body, 0)

    @pl.when(i % 2 == 0)
    def _():
        pv_blocks([i])

    lam_p = lam_ref[...]
    lam = (jnp.exp(jnp.sum(lam_p[0:1] * lam_p[1:2], axis=1, keepdims=True))
           - jnp.exp(jnp.sum(lam_p[2:3] * lam_p[3:4], axis=1, keepdims=True)) + lambda_init)
    a0, a1 = acc_ref[0], acc_ref[1]
    o = a0[:, :LANES] / a0[:, LANES:] - lam * (a1[:, :LANES] / a1[:, LANES:])
    ms = jnp.mean(o * o, axis=-1, keepdims=True)
    y = o * lax.rsqrt(ms + NORM_EPS) * subg_ref[...] * (1.0 - lambda_init)
    o_ref[0] = y.astype(o_ref.dtype)


def _diff_attn(qa, ka, va, lam_p, subg, lambda_init, tq=512):
    b, s, w = qa.shape
    nh = w // LANES
    kern = functools.partial(_da_kernel, tq=tq, tk=tq, lambda_init=lambda_init)
    qspec = pl.BlockSpec((1, tq, LANES), lambda bb, h, i: (bb, i, h))
    kvspec = pl.BlockSpec((1, s, LANES), lambda bb, h, i: (bb, 0, h))
    return pl.pallas_call(
        kern,
        out_shape=jax.ShapeDtypeStruct((b, s, w), BF16),
        grid=(b, nh, s // tq),
        in_specs=[qspec, kvspec, kvspec,
                  pl.BlockSpec(lam_p.shape, lambda bb, h, i: (0, 0)),
                  pl.BlockSpec(subg.shape, lambda bb, h, i: (0, 0))],
        out_specs=qspec,
        scratch_shapes=[pltpu.VMEM((2, s // tq, tq, tq), F32),
                        pltpu.VMEM((2, tq, LANES), F32),
                        pltpu.VMEM((2, tq, LANES), F32),
                        pltpu.VMEM((2, tq, 2 * LANES), F32)],
        compiler_params=_params(("arbitrary",) * 3),
        name="diff_attn",
    )(qa, ka, va, lam_p, subg)


def _sb_kernel(q_ref, k_ref, v_ref, tri_ref, o_ref, c_ref, acc_ref, *, tq, tk):
    i = pl.program_id(2)
    n_tiles = tq // tk
    tri = tri_ref[...]
    lane = lax.broadcasted_iota(jnp.int32, (tk, LANES), 1)
    qs = []
    for r in range(n_tiles):
        q = q_ref[0, r * tk:(r + 1) * tk, :]
        zero = jnp.zeros_like(q)
        qs.append((jnp.where(lane < HEAD_DIM, q, zero), jnp.where(lane >= HEAD_DIM, q, zero)))
    c_ref[...] = jnp.zeros(c_ref.shape, F32)
    acc_ref[...] = jnp.zeros(acc_ref.shape, F32)

    def load_kv(j):
        rows = pl.ds(pl.multiple_of(j * tk, tk), tk)
        return k_ref[0, rows, :], v_ref[0, rows, :]

    def window(jobs):
        keys = [(r, h) for r, _, _ in jobs for h in range(2)]
        state = {key: (c_ref[key], acc_ref[key]) for key in keys}
        new_state = {}
        for r, kvs, diag_first in jobs:
            v_all = jnp.concatenate([v for _, v in kvs], axis=0)
            for h in range(2):
                c_run, acc_prev = state[(r, h)]
                weights = []
                for n, (k, _) in enumerate(kvs):
                    masked = diag_first and n == 0
                    z = lax.dot_general(qs[r][h], k, (((1,), (1,)), ((), ())), preferred_element_type=F32)
                    soft = jnp.maximum(z, 0.0) + jnp.log2(1.0 + jnp.exp2(-jnp.abs(z)))
                    log_b = z - soft
                    if masked:
                        mask = (lax.broadcasted_iota(jnp.int32, z.shape, 1)
                                < lax.broadcasted_iota(jnp.int32, z.shape, 0))
                        soft = jnp.where(mask, soft, 0.0)
                    res = jnp.dot(soft.astype(BF16), tri, preferred_element_type=F32)
                    suffix, total = res[:, :tk], res[:, tk:]
                    a = jnp.concatenate([jnp.exp2(lb + sf + c_run)
                                         for lb, sf in zip(_lane_groups(log_b), _lane_groups(suffix))], axis=1)
                    if masked:
                        a = jnp.where(mask, a, 0.0)
                    weights.append(a.astype(BF16))
                    c_run = c_run + total
                a_all = jnp.concatenate(weights, axis=1)
                new_state[(r, h)] = (c_run, acc_prev + jnp.dot(a_all, v_all, preferred_element_type=F32))
        for key in keys:
            c_ref[key], acc_ref[key] = new_state[key]

    first = i * n_tiles

    @pl.when(i == 0)
    def _():
        blocks = [load_kv(r) for r in range(n_tiles)]
        window([(r, blocks[max(r - 1, 0):r + 1][::-1], True) for r in range(n_tiles)])

    @pl.when(i > 0)
    def _():
        blocks = [load_kv(first - 1 + n) for n in range(n_tiles + 1)]
        window([(r, [blocks[r + 1], blocks[r]], True) for r in range(n_tiles)])

    for r in range(n_tiles):
        def live(r=r):
            return jnp.max(jnp.maximum(c_ref[r, 0], c_ref[r, 1])) > SB_DEAD_LOG2

        def cond(carry):
            j, alive = carry
            return jnp.logical_and(j >= 0, alive)

        def body(carry, r=r, live=live):
            j, _ = carry
            window([(r, [load_kv(j)], False)])
            return j - 1, live()

        lax.while_loop(cond, body, (first + r - 2, live()))
        o_ref[0, r * tk:(r + 1) * tk, :] = jnp.where(lane < HEAD_DIM, acc_ref[r, 0],
                                                     acc_ref[r, 1]).astype(o_ref.dtype)


def _sb_attn(qb, kb, vb, tq=512, tk=256):
    b, s, w = qb.shape
    nh = w // LANES
    r = lax.broadcasted_iota(jnp.int32, (tk, tk), 0)
    c = lax.broadcasted_iota(jnp.int32, (tk, tk), 1)
    tri = -jnp.concatenate([(r > c).astype(BF16), jnp.ones((tk, LANES), BF16)], axis=1)
    kern = functools.partial(_sb_kernel, tq=tq, tk=tk)
    qspec = pl.BlockSpec((1, tq, LANES), lambda bb, h, i: (bb, i, h))
    kvspec = pl.BlockSpec((1, s, LANES), lambda bb, h, i: (bb, 0, h))
    return pl.pallas_call(
        kern,
        out_shape=jax.ShapeDtypeStruct((b, s, w), BF16),
        grid=(b, nh, s // tq),
        in_specs=[qspec, kvspec, kvspec, pl.BlockSpec(tri.shape, lambda bb, h, i: (0, 0))],
        out_specs=qspec,
        scratch_shapes=[pltpu.VMEM((tq // tk, 2, tk, LANES), F32), pltpu.VMEM((tq // tk, 2, tk, LANES), F32)],
        compiler_params=_params(("arbitrary",) * 3),
        name="sb_attn",
    )(qb, kb, vb, tri)


def _post_attn_kernel(oa_ref, ob_ref, ga_ref, gb_ref, x_ref, wda_ref, wsb_ref, wo_ref, g_ref,
                      wr_hi_ref, wr_lo_ref, br_ref, tri_ref,
                      x1_ref, idx_ref, gate_ref, rank_ref, cnt_ref, carry_ref):
    i = pl.program_id(0)

    @pl.when(i == 0)
    def _():
        carry_ref[...] = jnp.zeros(carry_ref.shape, F32)

    ya = jnp.dot(oa_ref[...], wda_ref[...], preferred_element_type=F32)
    yb = jnp.dot(ob_ref[...], wsb_ref[...], preferred_element_type=F32)
    mix = ga_ref[...].astype(F32) * ya + gb_ref[...].astype(F32) * yb
    x1 = x_ref[...] + jnp.dot(mix.astype(BF16), wo_ref[...], preferred_element_type=F32)
    x1_ref[...] = x1
    ms = jnp.mean(x1 * x1, axis=-1, keepdims=True)
    h = x1 * lax.rsqrt(ms + NORM_EPS) * g_ref[...]

    h_hi = h.astype(BF16)
    h_lo = (h - h_hi.astype(F32)).astype(BF16)
    ne = br_ref.shape[0]
    p_hi = jnp.dot(h_hi, wr_hi_ref[...], preferred_element_type=F32)
    p_lo = jnp.dot(h_lo, wr_lo_ref[...], preferred_element_type=F32)
    by_token = p_hi + pltpu.roll(p_hi, LANES - ne, 1) + p_lo
    logits = by_token.T[:ne] + br_ref[...]
    tm = logits.shape[1]
    eid = lax.broadcasted_iota(jnp.int32, (ne, tm), 0).astype(F32)
    vals, ids = [], []
    work = logits
    for _ in range(TOP_K):
        mx = jnp.max(work, axis=0, keepdims=True)
        sel = jnp.min(jnp.where(work == mx, eid, float(ne)), axis=0, keepdims=True)
        vals.append(mx)
        ids.append(sel)
        work = jnp.where(eid == sel, -jnp.inf, work)
    exps = [jnp.exp(v - vals[0]) for v in vals]
    denom = exps[0] + exps[1] + exps[2] + exps[3]
    onehots = [(eid == sel).astype(F32) for sel in ids]
    assigned = onehots[0] + onehots[1] + onehots[2] + onehots[3]
    before = jnp.dot(assigned.astype(BF16), tri_ref[...], preferred_element_type=F32) + carry_ref[...]
    for r in range(TOP_K):
        idx_ref[r:r + 1, :] = ids[r].astype(jnp.int32)
        gate_ref[r:r + 1, :] = exps[r] / denom
        rank_ref[r:r + 1, :] = jnp.sum(onehots[r] * before, axis=0, keepdims=True).astype(jnp.int32)
    carry = carry_ref[...] + jnp.sum(assigned, axis=1, keepdims=True)
    carry_ref[...] = carry
    cnt_ref[...] = jnp.broadcast_to(carry, cnt_ref.shape)


def _post_attn(oa, ob, ga, gb, x2, wda, wsb, wo, g, wr_hi, wr_lo, br, tm=512):
    t, d = x2.shape
    w = oa.shape[1]
    r = lax.broadcasted_iota(jnp.int32, (tm, tm), 0)
    c = lax.broadcasted_iota(jnp.int32, (tm, tm), 1)
    tri = (r < c).astype(BF16)
    row = lambda i: (i, 0)
    col = lambda i: (0, i)
    const = lambda i: (0, 0)
    full = lambda a: pl.BlockSpec(a.shape, const)
    return pl.pallas_call(
        _post_attn_kernel,
        out_shape=[jax.ShapeDtypeStruct((t, d), F32),
                   jax.ShapeDtypeStruct((TOP_K, t), jnp.int32), jax.ShapeDtypeStruct((TOP_K, t), F32),
                   jax.ShapeDtypeStruct((TOP_K, t), jnp.int32),
                   jax.ShapeDtypeStruct((N_EXPERTS, LANES), F32)],
        grid=(t // tm,),
        in_specs=[pl.BlockSpec((tm, w), row), pl.BlockSpec((tm, w), row),
                  pl.BlockSpec((tm, d), row), pl.BlockSpec((tm, d), row), pl.BlockSpec((tm, d), row),
                  full(wda), full(wsb), full(wo), full(g), full(wr_hi), full(wr_lo), full(br), full(tri)],
        out_specs=[pl.BlockSpec((tm, d), row),
                   pl.BlockSpec((TOP_K, tm), col), pl.BlockSpec((TOP_K, tm), col),
                   pl.BlockSpec((TOP_K, tm), col), pl.BlockSpec((N_EXPERTS, LANES), const)],
        scratch_shapes=[pltpu.VMEM((N_EXPERTS, 1), F32)],
        compiler_params=_params(("arbitrary",)),
        name="post_attn",
    )(oa, ob, ga, gb, x2, wda, wsb, wo, g, wr_hi, wr_lo, br, tri)


def _dispatch_kernel(dest_ref, zero_blk_ref, n_used_ref, h_ref, xs_ref, zeros, sem, zsem, *, tm, t_total):
    i = pl.program_id(0)
    n_blocks = xs_ref.shape[0] // ROW_BLOCK

    def zero_copy(blk):
        row = pl.multiple_of(blk * ROW_BLOCK, ROW_BLOCK)
        return pltpu.make_async_copy(zeros, xs_ref.at[pl.ds(row, ROW_BLOCK), :], zsem)

    @pl.when(i == 0)
    def _():
        zeros[...] = jnp.zeros(zeros.shape, zeros.dtype)
        n_tail = n_blocks - n_used_ref[0]

        def start(n, carry):
            zero_copy(jnp.where(n < N_EXPERTS, zero_blk_ref[jnp.minimum(n, N_EXPERTS - 1)],
                                n_used_ref[0] + n - N_EXPERTS)).start()
            return carry

        def wait(n, carry):
            zero_copy(0).wait()
            return carry

        lax.fori_loop(0, N_EXPERTS + n_tail, start, 0)
        lax.fori_loop(0, N_EXPERTS + n_tail, wait, 0)

    def row_copy(t, k):
        dst = dest_ref[k * t_total + i * tm + t]
        return pltpu.make_async_copy(h_ref.at[pl.ds(t, 1), :], xs_ref.at[pl.ds(dst, 1), :], sem)

    def issue(t, carry):
        for k in range(TOP_K):
            row_copy(t, k).start(priority=k % 2)
        return carry

    lax.fori_loop(0, tm, issue, 0, unroll=8)
    for _ in range(TOP_K):
        pltpu.make_async_copy(h_ref, xs_ref.at[pl.ds(0, tm), :], sem).wait()


def _dispatch(dest_flat, zero_blk, n_used, h, p_rows, tm=256):
    t, d = h.shape
    kern = functools.partial(_dispatch_kernel, tm=tm, t_total=t)
    return pl.pallas_call(
        kern,
        out_shape=jax.ShapeDtypeStruct((p_rows, d), h.dtype),
        grid_spec=pltpu.PrefetchScalarGridSpec(
            num_scalar_prefetch=3,
            grid=(t // tm,),
            in_specs=[pl.BlockSpec((tm, d), lambda i, *_: (i, 0))],
            out_specs=pl.BlockSpec(memory_space=pl.ANY),
            scratch_shapes=[pltpu.VMEM((ROW_BLOCK, d), h.dtype), pltpu.SemaphoreType.DMA,
                            pltpu.SemaphoreType.DMA],
        ),
        compiler_params=_params(("arbitrary",)),
        name="dispatch",
    )(dest_flat, zero_blk, n_used, h)


def _expert_kernel(first_ref, count_ref, n_used_ref, xs_ref, ng_ref, wg_ref, bg_ref, wu_ref, bu_ref, wd_ref,
                   bd_ref, o_ref, wg_bf, wu_bf, wd_bf, xbuf, obuf, in_sem, out_sem):
    e = pl.program_id(0)
    first = first_ref[e]
    count = count_ref[e]

    n_pairs = count // 2
    odd = count % 2

    def rows(blk, nb):
        return pl.ds(pl.multiple_of(blk * ROW_BLOCK, ROW_BLOCK), nb * ROW_BLOCK)

    def fetch(blk, nb, slot):
        return pltpu.make_async_copy(xs_ref.at[rows(blk, nb), :], xbuf.at[slot, pl.ds(0, nb * ROW_BLOCK), :],
                                     in_sem.at[slot])

    def flush(blk, nb, slot):
        return pltpu.make_async_copy(obuf.at[slot, pl.ds(0, nb * ROW_BLOCK), :], o_ref.at[rows(blk, nb), :],
                                     out_sem.at[slot])

    def mlp(x):
        ms = jnp.mean(x * x, axis=-1, keepdims=True)
        x = (x * lax.rsqrt(ms + NORM_EPS) * ng_ref[...]).astype(BF16)
        g = jnp.dot(x, wg_bf[...], preferred_element_type=F32) + bg_ref[0]
        u = jnp.dot(x, wu_bf[...], preferred_element_type=F32) + bu_ref[0]
        g = jnp.minimum(g, SWIGLU_LIMIT)
        u = jnp.clip(u, -SWIGLU_LIMIT, SWIGLU_LIMIT)
        glu = g * jax.nn.sigmoid(SWIGLU_ALPHA * g)
        act = ((u + 1.0) * glu).astype(BF16)
        return (jnp.dot(act, wd_bf[...], preferred_element_type=F32) + bd_ref[0]).astype(obuf.dtype)

    @pl.when(count > 0)
    def _():
        @pl.when(n_pairs > 0)
        def _():
            fetch(first, 2, 0).start()

        @pl.when(n_pairs == 0)
        def _():
            fetch(first, 1, 0).start()

        wg_bf[...] = wg_ref[0].astype(BF16)
        wu_bf[...] = wu_ref[0].astype(BF16)
        wd_bf[...] = wd_ref[0].astype(BF16)

    def pair(jj, carry):
        slot = jj % 2
        blk = first + 2 * jj
        fetch(blk, 2, slot).wait()

        @pl.when(jj + 1 < n_pairs)
        def _():
            fetch(blk + 2, 2, 1 - slot).start()

        @pl.when((jj + 1 == n_pairs) & (odd == 1))
        def _():
            fetch(blk + 2, 1, 1 - slot).start()

        y = mlp(xbuf[slot])

        @pl.when(jj >= 2)
        def _():
            flush(blk - 4, 2, slot).wait()

        obuf[slot] = y
        flush(blk, 2, slot).start()
        return carry

    lax.fori_loop(0, n_pairs, pair, 0)
    last = first + 2 * n_pairs
    last_slot = n_pairs % 2

    @pl.when(odd == 1)
    def _():
        fetch(last, 1, last_slot).wait()
        y = mlp(xbuf[last_slot, :ROW_BLOCK, :])

        @pl.when(n_pairs >= 2)
        def _():
            flush(last - 4, 2, last_slot).wait()

        obuf[last_slot, :ROW_BLOCK, :] = y
        flush(last, 1, last_slot).start()

    @pl.when(n_pairs >= 1)
    def _():
        flush(last - 2, 2, 1 - last_slot).wait()

    @pl.when((n_pairs >= 2) & (odd == 0))
    def _():
        flush(last - 4, 2, last_slot).wait()

    @pl.when(odd == 1)
    def _():
        flush(last, 1, last_slot).wait()

    @pl.when(e == pl.num_programs(0) - 1)
    def _():
        obuf[0, :ROW_BLOCK, :] = jnp.zeros((ROW_BLOCK, obuf.shape[2]), obuf.dtype)
        n_used = n_used_ref[0]
        n_total = o_ref.shape[0] // ROW_BLOCK

        def start(blk, carry):
            flush(blk, 1, 0).start()
            return carry

        def wait(blk, carry):
            flush(blk, 1, 0).wait()
            return carry

        lax.fori_loop(n_used, n_total, start, 0)
        lax.fori_loop(n_used, n_total, wait, 0)


def _experts(first_blk, blk_count, n_used, n_out_blocks, xs, norm_g, wg, bg, wu, bu, wd, bd):
    d = xs.shape[1]
    n_exp, _, f = wg.shape
    wmap = lambda e, *_: (e, 0, 0)
    return pl.pallas_call(
        _expert_kernel,
        out_shape=jax.ShapeDtypeStruct((n_out_blocks * ROW_BLOCK, d), BF16),
        grid_spec=pltpu.PrefetchScalarGridSpec(
            num_scalar_prefetch=3,
            grid=(n_exp,),
            in_specs=[pl.BlockSpec(memory_space=pl.ANY), pl.BlockSpec((1, d), lambda e, *_: (0, 0)),
                      pl.BlockSpec((1, d, f), wmap), pl.BlockSpec((1, 1, f), wmap),
                      pl.BlockSpec((1, d, f), wmap), pl.BlockSpec((1, 1, f), wmap),
                      pl.BlockSpec((1, f, d), wmap), pl.BlockSpec((1, 1, d), wmap)],
            out_specs=pl.BlockSpec(memory_space=pl.ANY),
            scratch_shapes=[pltpu.VMEM((d, f), BF16), pltpu.VMEM((d, f), BF16), pltpu.VMEM((f, d), BF16),
                            pltpu.VMEM((2, 2 * ROW_BLOCK, d), F32), pltpu.VMEM((2, 2 * ROW_BLOCK, d), BF16),
                            pltpu.SemaphoreType.DMA((2,)), pltpu.SemaphoreType.DMA((2,))],
        ),
        compiler_params=_params(("arbitrary",)),
        name="experts",
    )(first_blk, blk_count, n_used, xs, norm_g, wg, bg, wu, bu, wd, bd)


def _combine_kernel(base_ref, npass_ref, rows_ref, idx_ref, lr_ref, gate_ref, x1_ref, g_ref, o_ref,
                    wbuf, sem, *, tm):
    i = pl.program_id(0)
    slot = i % 2
    win = COMBINE_WINDOW
    width = N_EXPERTS * win

    def start_fetch(tile, p, s):
        for e in range(N_EXPERTS):
            start = pl.multiple_of(base_ref[tile * N_EXPERTS + e] + p * win, WINDOW_ALIGN)
            pltpu.make_async_copy(rows_ref.at[pl.ds(start, win), :],
                                  wbuf.at[s, pl.ds(e * win, win), :], sem.at[s]).start()

    def wait_fetch(s):
        pltpu.make_async_copy(rows_ref.at[pl.ds(0, width), :], wbuf.at[s], sem.at[s]).wait()

    @pl.when(i == 0)
    def _():
        start_fetch(0, 0, 0)

    @pl.when(i + 1 < pl.num_programs(0))
    def _():
        start_fetch(i + 1, 0, 1 - slot)

    idx = idx_ref[...]
    lr = lr_ref[...]
    gates = gate_ref[...]
    col = lax.broadcasted_iota(jnp.int32, (tm, width), 1)

    def weights(p):
        w = jnp.zeros((tm, width), F32)
        for k in range(TOP_K):
            r = lr[:, k:k + 1] - p * win
            tgt = jnp.where((r >= 0) & (r < win), idx[:, k:k + 1] * win + r, -1)
            w = jnp.where(col == tgt, gates[:, k:k + 1], w)
        return w.astype(BF16)

    def gathered(p):
        return jnp.dot(weights(p), wbuf[slot], preferred_element_type=F32)

    wait_fetch(slot)
    y = x1_ref[...] + gathered(0)

    def extra_pass(p, acc):
        start_fetch(i, p, slot)
        wait_fetch(slot)
        return acc + gathered(p)

    y = lax.fori_loop(1, npass_ref[i], extra_pass, y)
    ms = jnp.mean(y * y, axis=-1, keepdims=True)
    o_ref[...] = y * lax.rsqrt(ms + NORM_EPS) * g_ref[...]


def _combine(base_tbl, npass, rows, idx_t, lr_t, gates_t, x1, g, tm):
    t, d = x1.shape
    kern = functools.partial(_combine_kernel, tm=tm)
    tok = lambda i, *_: (i, 0)
    return pl.pallas_call(
        kern,
        out_shape=jax.ShapeDtypeStruct((t, d), F32),
        grid_spec=pltpu.PrefetchScalarGridSpec(
            num_scalar_prefetch=2,
            grid=(t // tm,),
            in_specs=[pl.BlockSpec(memory_space=pl.ANY),
                      pl.BlockSpec((tm, TOP_K), tok), pl.BlockSpec((tm, TOP_K), tok),
                      pl.BlockSpec((tm, TOP_K), tok), pl.BlockSpec((tm, d), tok),
                      pl.BlockSpec((1, d), lambda i, *_: (0, 0))],
            out_specs=pl.BlockSpec((tm, d), tok),
            scratch_shapes=[pltpu.VMEM((2, N_EXPERTS * COMBINE_WINDOW, d), rows.dtype),
                            pltpu.SemaphoreType.DMA((2,))],
        ),
        compiler_params=_params(("arbitrary",)),
        name="combine",
    )(base_tbl, npass, rows, idx_t, lr_t, gates_t, x1, g)


def _rope_tables(seq):
    inv = 1.0 / (ROPE_THETA ** (jnp.arange(0, HEAD_DIM, 2, dtype=F32) / HEAD_DIM))
    ang = jnp.arange(seq, dtype=F32)[:, None] * inv[None, :]
    cos, sin = jnp.cos(ang), jnp.sin(ang)
    return jnp.concatenate([cos] * 4, axis=1), jnp.concatenate([-sin, sin, -sin, sin], axis=1)


def kernel(x, norm_mix_g, w_in, lambda_q1, lambda_k1, lambda_q2, lambda_k2, da_subln_g, w_da_out, w_sb_out, w_o, norm_ffn_g, w_router, b_router, w_gate, b_gate, w_up, b_up, w_down, b_down, norm_final_g):
    b, s, d = x.shape
    depth = w_in.shape[0]
    t = b * s
    cos_t, sin_t = _rope_tables(s)
    n_blocks = (t * TOP_K + ROW_BLOCK - 1) // ROW_BLOCK + N_EXPERTS
    p_rows = n_blocks * ROW_BLOCK
    x2 = x.reshape(t, d)
    for l in range(depth):
        lambda_init = 0.8 - 0.6 * math.exp(-0.3 * l)
        qa, ka, va, qb, kb, vb, ga, gb = _in_proj(
            x2, norm_mix_g[l][None, :], w_in[l].astype(BF16), cos_t, sin_t, s)
        lam_p = jnp.stack([lambda_q1[l], lambda_k1[l], lambda_q2[l], lambda_k2[l]]).astype(F32)
        seq3 = lambda a: a.reshape(b, s, a.shape[1])
        oa = _diff_attn(seq3(qa), seq3(ka), seq3(va), lam_p, da_subln_g[l][None, :].astype(F32), lambda_init)
        ob = _sb_attn(seq3(qb), seq3(kb), seq3(vb))
        wr = w_router[l].astype(F32)
        wr_top = wr.astype(BF16)
        wr_rest = (wr - wr_top.astype(F32)).astype(BF16)
        lane_pad = lambda a: jnp.pad(a, ((0, 0), (0, LANES - a.shape[1])))
        wr_hi = lane_pad(jnp.concatenate([wr_top, wr_rest], axis=1))
        wr_lo = lane_pad(wr_top)
        x1, idx, gates, rank, cnt = _post_attn(
            oa.reshape(t, -1), ob.reshape(t, -1), ga, gb, x2,
            w_da_out[l].astype(BF16), w_sb_out[l].astype(BF16), w_o[l].astype(BF16),
            norm_ffn_g[l][None, :], wr_hi, wr_lo, b_router[l][:, None].astype(F32))
        counts = cnt[:, 0].astype(jnp.int32)
        padded = (counts + ROW_BLOCK - 1) // ROW_BLOCK * ROW_BLOCK
        pad_ends = jnp.cumsum(padded)
        pad_starts = pad_ends - padded
        experts = jnp.arange(N_EXPERTS, dtype=jnp.int32)
        chosen = idx[:, :, None] == experts
        base = jnp.sum(jnp.where(chosen, pad_starts, 0), axis=-1)
        dest = (base + rank).reshape(-1)
        n_used = (pad_ends[-1] // ROW_BLOCK).astype(jnp.int32)
        first_blk = (pad_starts // ROW_BLOCK).astype(jnp.int32)
        blk_count = (padded // ROW_BLOCK).astype(jnp.int32)
        zero_blk = jnp.maximum(pad_ends // ROW_BLOCK - 1, 0).astype(jnp.int32)
        n_tiles = t // COMBINE_TILE
        tile_cnt = jnp.sum(chosen.reshape(TOP_K, n_tiles, COMBINE_TILE, N_EXPERTS), axis=(0, 2), dtype=jnp.int32)
        tile_carry = jnp.cumsum(tile_cnt, axis=0) - tile_cnt
        run_start = pad_starts[None, :] + tile_carry
        run_skew = run_start % WINDOW_ALIGN
        run_base = (run_start - run_skew).reshape(-1).astype(jnp.int32)
        shift_tok = jnp.repeat(tile_carry - run_skew, COMBINE_TILE, axis=0)
        local_rank = rank - jnp.sum(jnp.where(chosen, shift_tok[None], 0), axis=-1)
        n_pass = jnp.maximum((jnp.max(tile_cnt + run_skew, axis=1) + COMBINE_WINDOW - 1) // COMBINE_WINDOW,
                             1).astype(jnp.int32)
        xs = _dispatch(dest, zero_blk, n_used[None], x1, p_rows)
        rows = _experts(first_blk, blk_count, n_used[None], n_blocks + 1, xs, norm_ffn_g[l][None, :].astype(F32),
                        w_gate[l], b_gate[l][:, None, :],
                        w_up[l], b_up[l][:, None, :], w_down[l], b_down[l][:, None, :])
        g_next = norm_final_g[None, :] if l == depth - 1 else jnp.ones((1, d), F32)
        x2 = _combine(run_base, n_pass, rows, idx.T, local_rank.T.astype(jnp.int32), gates.T, x1, g_next,
                      COMBINE_TILE)
        if l != depth - 1:
            raise NotImplementedError("only the final layer's norm is fused into the combine kernel")
    return x2.reshape(b, s, d)
```

```python
import functools
import math

import jax
import jax.numpy as jnp
from jax import lax
from jax.experimental import pallas as pl
from jax.experimental.pallas import tpu as pltpu

F32 = jnp.float32
BF16 = jnp.bfloat16

DA_HEADS = 4
HEAD_DIM = 64
N_EXPERTS = 32
TOP_K = 4
ROPE_THETA = 10000.0
SWIGLU_LIMIT = 7.0
SWIGLU_ALPHA = 1.702
NORM_EPS = 1e-5
ROW_BLOCK = 256
COMBINE_TILE = 256
WINDOW_ALIGN = 16
COMBINE_WINDOW = 80
LANES = 128
NEG_BIG = -1e30
LOG2E = math.log2(math.e)
SB_DEAD_LOG2 = -150.0

VMEM_LIMIT = 56 * 1024 * 1024


def _params(sem, vmem=VMEM_LIMIT):
    return pltpu.CompilerParams(dimension_semantics=sem, vmem_limit_bytes=vmem)


def _in_proj_kernel(x_ref, g_ref, w_ref, cos_ref, sin_ref,
                    qa_ref, ka_ref, va_ref, qb_ref, kb_ref, vb_ref, ga_ref, gb_ref):
    x = x_ref[...]
    ms = jnp.mean(x * x, axis=-1, keepdims=True)
    h = (x * lax.rsqrt(ms + NORM_EPS) * g_ref[...]).astype(BF16)
    cos = cos_ref[...]
    sin = sin_ref[...]
    lane = lax.broadcasted_iota(jnp.int32, cos.shape, 1)
    first_half = (lane & (HEAD_DIM - 1)) < HEAD_DIM // 2

    def proj(c0, width):
        return jnp.dot(h, w_ref[:, c0:c0 + width], preferred_element_type=F32)

    def rope(r):
        outs = []
        for g in range(r.shape[1] // LANES):
            xg = r[:, g * LANES:(g + 1) * LANES]
            rot = jnp.where(first_half, pltpu.roll(xg, LANES - HEAD_DIM // 2, 1),
                            pltpu.roll(xg, HEAD_DIM // 2, 1))
            outs.append(xg * cos + rot * sin)
        return jnp.concatenate(outs, axis=1)

    scale = HEAD_DIM ** -0.5 * LOG2E
    w = qa_ref.shape[1]
    d = ga_ref.shape[1]
    qa_ref[...] = (rope(proj(0, w)) * scale).astype(BF16)
    ka_ref[...] = rope(proj(w, w)).astype(BF16)
    va_ref[...] = proj(2 * w, w).astype(BF16)
    qb_ref[...] = (proj(3 * w, w) * scale).astype(BF16)
    kb_ref[...] = proj(4 * w, w).astype(BF16)
    vb_ref[...] = proj(5 * w, w).astype(BF16)
    ga_ref[...] = jax.nn.sigmoid(proj(6 * w, d)).astype(BF16)
    gb_ref[...] = jax.nn.sigmoid(proj(6 * w + d, d)).astype(BF16)


def _in_proj(x2, g, w_in_bf, cos_t, sin_t, seq, tm=512):
    t, d = x2.shape
    w = 512
    nseq = seq // tm
    outs = [jax.ShapeDtypeStruct((t, w), BF16)] * 6 + [jax.ShapeDtypeStruct((t, d), BF16)] * 2
    row = lambda i: (i, 0)
    return pl.pallas_call(
        _in_proj_kernel,
        out_shape=outs,
        grid=(t // tm,),
        in_specs=[
            pl.BlockSpec((tm, d), row),
            pl.BlockSpec((1, d), lambda i: (0, 0)),
            pl.BlockSpec(w_in_bf.shape, lambda i: (0, 0)),
            pl.BlockSpec((tm, LANES), lambda i: (i % nseq, 0)),
            pl.BlockSpec((tm, LANES), lambda i: (i % nseq, 0)),
        ],
        out_specs=[pl.BlockSpec((tm, w), row)] * 6 + [pl.BlockSpec((tm, d), row)] * 2,
        compiler_params=_params(("arbitrary",)),
        name="in_proj",
    )(x2, g, w_in_bf, cos_t, sin_t)


def _lane_groups(x):
    return [x[:, g * LANES:(g + 1) * LANES] for g in range(x.shape[1] // LANES)]


def _da_kernel(q_ref, k_ref, v_ref, lam_ref, subg_ref, o_ref, s_buf, mx_ref, m_ref, acc_ref,
               *, tq, tk, lambda_init):
    i = pl.program_id(2)
    q = q_ref[0]
    lane = lax.broadcasted_iota(jnp.int32, q.shape, 1)
    zero = jnp.zeros_like(q)
    qs = (jnp.where(lane < HEAD_DIM, q, zero), jnp.where(lane >= HEAD_DIM, q, zero))
    mx_ref[...] = jnp.full(mx_ref.shape, NEG_BIG, F32)
    acc_ref[...] = jnp.zeros(acc_ref.shape, F32)

    def score_blocks(js, diag_last):
        maxes = [mx_ref[0], mx_ref[1]]
        for n, j in enumerate(js):
            k = k_ref[0, pl.ds(pl.multiple_of(j * tk, tk), tk), :]
            for h in range(2):
                s = lax.dot_general(qs[h], k, (((1,), (1,)), ((), ())), preferred_element_type=F32)
                if diag_last and n == len(js) - 1:
                    rows = lax.broadcasted_iota(jnp.int32, s.shape, 0)
                    cols = lax.broadcasted_iota(jnp.int32, s.shape, 1)
                    s = jnp.where(rows >= cols, s, NEG_BIG)
                s_buf[h, j] = s
                for sg in _lane_groups(s):
                    maxes[h] = jnp.maximum(maxes[h], sg)
        mx_ref[0], mx_ref[1] = maxes

    def score_body(jj, carry):
        score_blocks([2 * jj, 2 * jj + 1], False)
        return carry

    lax.fori_loop(0, i // 2, score_body, 0)

    @pl.when(i % 2 == 0)
    def _():
        score_blocks([i], True)

    @pl.when(i % 2 == 1)
    def _():
        score_blocks([i - 1, i], True)

    for h in range(2):
        m_ref[h] = jnp.broadcast_to(jnp.max(mx_ref[h], axis=1, keepdims=True), (tq, LANES))

    def pv_blocks(js):
        vs = [v_ref[0, pl.ds(pl.multiple_of(j * tk, tk), tk), :] for j in js]
        v_all = jnp.concatenate(vs, axis=0)
        v_ext = jnp.concatenate([v_all, jnp.ones_like(v_all)], axis=1)
        accs = [acc_ref[0], acc_ref[1]]
        for h in range(2):
            m = m_ref[h]
            p = jnp.concatenate([jnp.exp2(sg - m).astype(BF16)
                                 for j in js for sg in _lane_groups(s_buf[h, j])], axis=1)
            accs[h] = accs[h] + jnp.dot(p, v_ext, preferred_element_type=F32)
        acc_ref[0], acc_ref[1] = accs

    def pv_body(jj, carry):
        pv_blocks([2 * jj, 2 * jj + 1])
        return carry

    lax.fori_loop(0, (i + 1) // 2, pv_body, 0)

    @pl.when(i % 2 == 0)
    def _():
        pv_blocks([i])

    lam_p = lam_ref[...]
    lam = (jnp.exp(jnp.sum(lam_p[0:1] * lam_p[1:2], axis=1, keepdims=True))
           - jnp.exp(jnp.sum(lam_p[2:3] * lam_p[3:4], axis=1, keepdims=True)) + lambda_init)
    a0, a1 = acc_ref[0], acc_ref[1]
    o = a0[:, :LANES] / a0[:, LANES:] - lam * (a1[:, :LANES] / a1[:, LANES:])
    ms = jnp.mean(o * o, axis=-1, keepdims=True)
    y = o * lax.rsqrt(ms + NORM_EPS) * subg_ref[...] * (1.0 - lambda_init)
    o_ref[0] = y.astype(o_ref.dtype)


def _diff_attn(qa, ka, va, lam_p, subg, lambda_init, tq=512):
    b, s, w = qa.shape
    nh = w // LANES
    kern = functools.partial(_da_kernel, tq=tq, tk=tq, lambda_init=lambda_init)
    qspec = pl.BlockSpec((1, tq, LANES), lambda bb, h, i: (bb, i, h))
    kvspec = pl.BlockSpec((1, s, LANES), lambda bb, h, i: (bb, 0, h))
    return pl.pallas_call(
        kern,
        out_shape=jax.ShapeDtypeStruct((b, s, w), BF16),
        grid=(b, nh, s // tq),
        in_specs=[qspec, kvspec, kvspec,
                  pl.BlockSpec(lam_p.shape, lambda bb, h, i: (0, 0)),
                  pl.BlockSpec(subg.shape, lambda bb, h, i: (0, 0))],
        out_specs=qspec,
        scratch_shapes=[pltpu.VMEM((2, s // tq, tq, tq), F32),
                        pltpu.VMEM((2, tq, LANES), F32),
                        pltpu.VMEM((2, tq, LANES), F32),
                        pltpu.VMEM((2, tq, 2 * LANES), F32)],
        compiler_params=_params(("arbitrary",) * 3),
        name="diff_attn",
    )(qa, ka, va, lam_p, subg)


def _sb_kernel(q_ref, k_ref, v_ref, tri_ref, o_ref, c_ref, acc_ref, *, tq, tk):
    i = pl.program_id(2)
    n_tiles = tq // tk
    tri = tri_ref[...]
    lane = lax.broadcasted_iota(jnp.int32, (tk, LANES), 1)
    qs = []
    for r in range(n_tiles):
        q = q_ref[0, r * tk:(r + 1) * tk, :]
        zero = jnp.zeros_like(q)
        qs.append((jnp.where(lane < HEAD_DIM, q, zero), jnp.where(lane >= HEAD_DIM, q, zero)))
    c_ref[...] = jnp.zeros(c_ref.shape, F32)
    acc_ref[...] = jnp.zeros(acc_ref.shape, F32)

    def load_kv(j):
        rows = pl.ds(pl.multiple_of(j * tk, tk), tk)
        return k_ref[0, rows, :], v_ref[0, rows, :]

    def window(jobs):
        keys = [(r, h) for r, _, _ in jobs for h in range(2)]
        state = {key: (c_ref[key], acc_ref[key]) for key in keys}
        new_state = {}
        for r, kvs, diag_first in jobs:
            v_all = jnp.concatenate([v for _, v in kvs], axis=0)
            for h in range(2):
                c_run, acc_prev = state[(r, h)]
                weights = []
                for n, (k, _) in enumerate(kvs):
                    masked = diag_first and n == 0
                    z = lax.dot_general(qs[r][h], k, (((1,), (1,)), ((), ())), preferred_element_type=F32)
                    soft = jnp.maximum(z, 0.0) + jnp.log2(1.0 + jnp.exp2(-jnp.abs(z)))
                    log_b = z - soft
                    if masked:
                        mask = (lax.broadcasted_iota(jnp.int32, z.shape, 1)
                                < lax.broadcasted_iota(jnp.int32, z.shape, 0))
                        soft = jnp.where(mask, soft, 0.0)
                    res = jnp.dot(soft.astype(BF16), tri, preferred_element_type=F32)
                    suffix, total = res[:, :tk], res[:, tk:]
                    a = jnp.concatenate([jnp.exp2(lb + sf + c_run)
                                         for lb, sf in zip(_lane_groups(log_b), _lane_groups(suffix))], axis=1)
                    if masked:
                        a = jnp.where(mask, a, 0.0)
                    weights.append(a.astype(BF16))
                    c_run = c_run + total
                a_all = jnp.concatenate(weights, axis=1)
                new_state[(r, h)] = (c_run, acc_prev + jnp.dot(a_all, v_all, preferred_element_type=F32))
        for key in keys:
            c_ref[key], acc_ref[key] = new_state[key]

    first = i * n_tiles

    @pl.when(i == 0)
    def _():
        blocks = [load_kv(r) for r in range(n_tiles)]
        window([(r, blocks[max(r - 1, 0):r + 1][::-1], True) for r in range(n_tiles)])

    @pl.when(i > 0)
    def _():
        blocks = [load_kv(first - 1 + n) for n in range(n_tiles + 1)]
        window([(r, [blocks[r + 1], blocks[r]], True) for r in range(n_tiles)])

    for r in range(n_tiles):
        def live(r=r):
            return jnp.max(jnp.maximum(c_ref[r, 0], c_ref[r, 1])) > SB_DEAD_LOG2

        def cond(carry):
            j, alive = carry
            return jnp.logical_and(j >= 0, alive)

        def body(carry, r=r, live=live):
            j, _ = carry
            window([(r, [load_kv(j)], False)])
            return j - 1, live()

        lax.while_loop(cond, body, (first + r - 2, live()))
        o_ref[0, r * tk:(r + 1) * tk, :] = jnp.where(lane < HEAD_DIM, acc_ref[r, 0],
                                                     acc_ref[r, 1]).astype(o_ref.dtype)


def _sb_attn(qb, kb, vb, tq=512, tk=256):
    b, s, w = qb.shape
    nh = w // LANES
    r = lax.broadcasted_iota(jnp.int32, (tk, tk), 0)
    c = lax.broadcasted_iota(jnp.int32, (tk, tk), 1)
    tri = -jnp.concatenate([(r > c).astype(BF16), jnp.ones((tk, LANES), BF16)], axis=1)
    kern = functools.partial(_sb_kernel, tq=tq, tk=tk)
    qspec = pl.BlockSpec((1, tq, LANES), lambda bb, h, i: (bb, i, h))
    kvspec = pl.BlockSpec((1, s, LANES), lambda bb, h, i: (bb, 0, h))
    return pl.pallas_call(
        kern,
        out_shape=jax.ShapeDtypeStruct((b, s, w), BF16),
        grid=(b, nh, s // tq),
        in_specs=[qspec, kvspec, kvspec, pl.BlockSpec(tri.shape, lambda bb, h, i: (0, 0))],
        out_specs=qspec,
        scratch_shapes=[pltpu.VMEM((tq // tk, 2, tk, LANES), F32), pltpu.VMEM((tq // tk, 2, tk, LANES), F32)],
        compiler_params=_params(("arbitrary",) * 3),
        name="sb_attn",
    )(qb, kb, vb, tri)


def _post_attn_kernel(oa_ref, ob_ref, ga_ref, gb_ref, x_ref, wda_ref, wsb_ref, wo_ref, g_ref,
                      wr_hi_ref, wr_lo_ref, br_ref, tri_ref,
                      x1_ref, h_ref, idx_ref, gate_ref, rank_ref, cnt_ref, carry_ref):
    i = pl.program_id(0)

    @pl.when(i == 0)
    def _():
        carry_ref[...] = jnp.zeros(carry_ref.shape, F32)

    ya = jnp.dot(oa_ref[...], wda_ref[...], preferred_element_type=F32)
    yb = jnp.dot(ob_ref[...], wsb_ref[...], preferred_element_type=F32)
    mix = ga_ref[...].astype(F32) * ya + gb_ref[...].astype(F32) * yb
    x1 = x_ref[...] + jnp.dot(mix.astype(BF16), wo_ref[...], preferred_element_type=F32)
    x1_ref[...] = x1
    ms = jnp.mean(x1 * x1, axis=-1, keepdims=True)
    h = x1 * lax.rsqrt(ms + NORM_EPS) * g_ref[...]
    h_ref[...] = h

    h_hi = h.astype(BF16)
    h_lo = (h - h_hi.astype(F32)).astype(BF16)
    ne = br_ref.shape[0]
    p_hi = jnp.dot(h_hi, wr_hi_ref[...], preferred_element_type=F32)
    p_lo = jnp.dot(h_lo, wr_lo_ref[...], preferred_element_type=F32)
    by_token = p_hi + pltpu.roll(p_hi, LANES - ne, 1) + p_lo
    logits = by_token.T[:ne] + br_ref[...]
    tm = logits.shape[1]
    eid = lax.broadcasted_iota(jnp.int32, (ne, tm), 0).astype(F32)
    vals, ids = [], []
    work = logits
    for _ in range(TOP_K):
        mx = jnp.max(work, axis=0, keepdims=True)
        sel = jnp.min(jnp.where(work == mx, eid, float(ne)), axis=0, keepdims=True)
        vals.append(mx)
        ids.append(sel)
        work = jnp.where(eid == sel, -jnp.inf, work)
    exps = [jnp.exp(v - vals[0]) for v in vals]
    denom = exps[0] + exps[1] + exps[2] + exps[3]
    onehots = [(eid == sel).astype(F32) for sel in ids]
    assigned = onehots[0] + onehots[1] + onehots[2] + onehots[3]
    before = jnp.dot(assigned.astype(BF16), tri_ref[...], preferred_element_type=F32) + carry_ref[...]
    for r in range(TOP_K):
        idx_ref[r:r + 1, :] = ids[r].astype(jnp.int32)
        gate_ref[r:r + 1, :] = exps[r] / denom
        rank_ref[r:r + 1, :] = jnp.sum(onehots[r] * before, axis=0, keepdims=True).astype(jnp.int32)
    carry = carry_ref[...] + jnp.sum(assigned, axis=1, keepdims=True)
    carry_ref[...] = carry
    cnt_ref[...] = jnp.broadcast_to(carry, cnt_ref.shape)


def _post_attn(oa, ob, ga, gb, x2, wda, wsb, wo, g, wr_hi, wr_lo, br, tm=512):
    t, d = x2.shape
    w = oa.shape[1]
    r = lax.broadcasted_iota(jnp.int32, (tm, tm), 0)
    c = lax.broadcasted_iota(jnp.int32, (tm, tm), 1)
    tri = (r < c).astype(BF16)
    row = lambda i: (i, 0)
    col = lambda i: (0, i)
    const = lambda i: (0, 0)
    full = lambda a: pl.BlockSpec(a.shape, const)
    return pl.pallas_call(
        _post_attn_kernel,
        out_shape=[jax.ShapeDtypeStruct((t, d), F32), jax.ShapeDtypeStruct((t, d), F32),
                   jax.ShapeDtypeStruct((TOP_K, t), jnp.int32), jax.ShapeDtypeStruct((TOP_K, t), F32),
                   jax.ShapeDtypeStruct((TOP_K, t), jnp.int32),
                   jax.ShapeDtypeStruct((N_EXPERTS, LANES), F32)],
        grid=(t // tm,),
        in_specs=[pl.BlockSpec((tm, w), row), pl.BlockSpec((tm, w), row),
                  pl.BlockSpec((tm, d), row), pl.BlockSpec((tm, d), row), pl.BlockSpec((tm, d), row),
                  full(wda), full(wsb), full(wo), full(g), full(wr_hi), full(wr_lo), full(br), full(tri)],
        out_specs=[pl.BlockSpec((tm, d), row), pl.BlockSpec((tm, d), row),
                   pl.BlockSpec((TOP_K, tm), col), pl.BlockSpec((TOP_K, tm), col),
                   pl.BlockSpec((TOP_K, tm), col), pl.BlockSpec((N_EXPERTS, LANES), const)],
        scratch_shapes=[pltpu.VMEM((N_EXPERTS, 1), F32)],
        compiler_params=_params(("arbitrary",)),
        name="post_attn",
    )(oa, ob, ga, gb, x2, wda, wsb, wo, g, wr_hi, wr_lo, br, tri)


def _dispatch_kernel(dest_ref, zero_blk_ref, n_used_ref, h_ref, xs_ref, zeros, sem, zsem, *, tm, t_total):
    i = pl.program_id(0)
    n_blocks = xs_ref.shape[0] // ROW_BLOCK

    def zero_copy(blk):
        row = pl.multiple_of(blk * ROW_BLOCK, ROW_BLOCK)
        return pltpu.make_async_copy(zeros, xs_ref.at[pl.ds(row, ROW_BLOCK), :], zsem)

    @pl.when(i == 0)
    def _():
        zeros[...] = jnp.zeros(zeros.shape, zeros.dtype)
        n_tail = n_blocks - n_used_ref[0]

        def start(n, carry):
            zero_copy(jnp.where(n < N_EXPERTS, zero_blk_ref[jnp.minimum(n, N_EXPERTS - 1)],
                                n_used_ref[0] + n - N_EXPERTS)).start()
            return carry

        def wait(n, carry):
            zero_copy(0).wait()
            return carry

        lax.fori_loop(0, N_EXPERTS + n_tail, start, 0)
        lax.fori_loop(0, N_EXPERTS + n_tail, wait, 0)

    def row_copy(t, k):
        dst = dest_ref[k * t_total + i * tm + t]
        return pltpu.make_async_copy(h_ref.at[pl.ds(t, 1), :], xs_ref.at[pl.ds(dst, 1), :], sem)

    def issue(t, carry):
        for k in range(TOP_K):
            row_copy(t, k).start(priority=k % 2)
        return carry

    lax.fori_loop(0, tm, issue, 0, unroll=8)
    for _ in range(TOP_K):
        pltpu.make_async_copy(h_ref, xs_ref.at[pl.ds(0, tm), :], sem).wait()


def _dispatch(dest_flat, zero_blk, n_used, h, p_rows, tm=256):
    t, d = h.shape
    kern = functools.partial(_dispatch_kernel, tm=tm, t_total=t)
    return pl.pallas_call(
        kern,
        out_shape=jax.ShapeDtypeStruct((p_rows, d), h.dtype),
        grid_spec=pltpu.PrefetchScalarGridSpec(
            num_scalar_prefetch=3,
            grid=(t // tm,),
            in_specs=[pl.BlockSpec((tm, d), lambda i, *_: (i, 0))],
            out_specs=pl.BlockSpec(memory_space=pl.ANY),
            scratch_shapes=[pltpu.VMEM((ROW_BLOCK, d), h.dtype), pltpu.SemaphoreType.DMA,
                            pltpu.SemaphoreType.DMA],
        ),
        compiler_params=_params(("arbitrary",)),
        name="dispatch",
    )(dest_flat, zero_blk, n_used, h)


def _expert_kernel(first_ref, count_ref, n_used_ref, xs_ref, wg_ref, bg_ref, wu_ref, bu_ref, wd_ref, bd_ref,
                   o_ref, wg_bf, wu_bf, wd_bf, xbuf, obuf, in_sem, out_sem):
    e = pl.program_id(0)
    first = first_ref[e]
    count = count_ref[e]

    n_pairs = count // 2
    odd = count % 2

    def rows(blk, nb):
        return pl.ds(pl.multiple_of(blk * ROW_BLOCK, ROW_BLOCK), nb * ROW_BLOCK)

    def fetch(blk, nb, slot):
        return pltpu.make_async_copy(xs_ref.at[rows(blk, nb), :], xbuf.at[slot, pl.ds(0, nb * ROW_BLOCK), :],
                                     in_sem.at[slot])

    def flush(blk, nb, slot):
        return pltpu.make_async_copy(obuf.at[slot, pl.ds(0, nb * ROW_BLOCK), :], o_ref.at[rows(blk, nb), :],
                                     out_sem.at[slot])

    def mlp(x):
        x = x.astype(BF16)
        g = jnp.dot(x, wg_bf[...], preferred_element_type=F32) + bg_ref[0]
        u = jnp.dot(x, wu_bf[...], preferred_element_type=F32) + bu_ref[0]
        g = jnp.minimum(g, SWIGLU_LIMIT)
        u = jnp.clip(u, -SWIGLU_LIMIT, SWIGLU_LIMIT)
        glu = g * jax.nn.sigmoid(SWIGLU_ALPHA * g)
        act = ((u + 1.0) * glu).astype(BF16)
        return (jnp.dot(act, wd_bf[...], preferred_element_type=F32) + bd_ref[0]).astype(obuf.dtype)

    @pl.when(count > 0)
    def _():
        @pl.when(n_pairs > 0)
        def _():
            fetch(first, 2, 0).start()

        @pl.when(n_pairs == 0)
        def _():
            fetch(first, 1, 0).start()

        wg_bf[...] = wg_ref[0].astype(BF16)
        wu_bf[...] = wu_ref[0].astype(BF16)
        wd_bf[...] = wd_ref[0].astype(BF16)

    def pair(jj, carry):
        slot = jj % 2
        blk = first + 2 * jj
        fetch(blk, 2, slot).wait()

        @pl.when(jj + 1 < n_pairs)
        def _():
            fetch(blk + 2, 2, 1 - slot).start()

        @pl.when((jj + 1 == n_pairs) & (odd == 1))
        def _():
            fetch(blk + 2, 1, 1 - slot).start()

        y = mlp(xbuf[slot])

        @pl.when(jj >= 2)
        def _():
            flush(blk - 4, 2, slot).wait()

        obuf[slot] = y
        flush(blk, 2, slot).start()
        return carry

    lax.fori_loop(0, n_pairs, pair, 0)
    last = first + 2 * n_pairs
    last_slot = n_pairs % 2

    @pl.when(odd == 1)
    def _():
        fetch(last, 1, last_slot).wait()
        y = mlp(xbuf[last_slot, :ROW_BLOCK, :])

        @pl.when(n_pairs >= 2)
        def _():
            flush(last - 4, 2, last_slot).wait()

        obuf[last_slot, :ROW_BLOCK, :] = y
        flush(last, 1, last_slot).start()

    @pl.when(n_pairs >= 1)
    def _():
        flush(last - 2, 2, 1 - last_slot).wait()

    @pl.when((n_pairs >= 2) & (odd == 0))
    def _():
        flush(last - 4, 2, last_slot).wait()

    @pl.when(odd == 1)
    def _():
        flush(last, 1, last_slot).wait()

    @pl.when(e == pl.num_programs(0) - 1)
    def _():
        obuf[0, :ROW_BLOCK, :] = jnp.zeros((ROW_BLOCK, obuf.shape[2]), obuf.dtype)
        n_used = n_used_ref[0]
        n_total = o_ref.shape[0] // ROW_BLOCK

        def start(blk, carry):
            flush(blk, 1, 0).start()
            return carry

        def wait(blk, carry):
            flush(blk, 1, 0).wait()
            return carry

        lax.fori_loop(n_used, n_total, start, 0)
        lax.fori_loop(n_used, n_total, wait, 0)


def _experts(first_blk, blk_count, n_used, n_out_blocks, xs, wg, bg, wu, bu, wd, bd):
    d = xs.shape[1]
    n_exp, _, f = wg.shape
    wmap = lambda e, *_: (e, 0, 0)
    return pl.pallas_call(
        _expert_kernel,
        out_shape=jax.ShapeDtypeStruct((n_out_blocks * ROW_BLOCK, d), BF16),
        grid_spec=pltpu.PrefetchScalarGridSpec(
            num_scalar_prefetch=3,
            grid=(n_exp,),
            in_specs=[pl.BlockSpec(memory_space=pl.ANY),
                      pl.BlockSpec((1, d, f), wmap), pl.BlockSpec((1, 1, f), wmap),
                      pl.BlockSpec((1, d, f), wmap), pl.BlockSpec((1, 1, f), wmap),
                      pl.BlockSpec((1, f, d), wmap), pl.BlockSpec((1, 1, d), wmap)],
            out_specs=pl.BlockSpec(memory_space=pl.ANY),
            scratch_shapes=[pltpu.VMEM((d, f), BF16), pltpu.VMEM((d, f), BF16), pltpu.VMEM((f, d), BF16),
                            pltpu.VMEM((2, 2 * ROW_BLOCK, d), F32), pltpu.VMEM((2, 2 * ROW_BLOCK, d), BF16),
                            pltpu.SemaphoreType.DMA((2,)), pltpu.SemaphoreType.DMA((2,))],
        ),
        compiler_params=_params(("arbitrary",)),
        name="experts",
    )(first_blk, blk_count, n_used, xs, wg, bg, wu, bu, wd, bd)


def _combine_kernel(base_ref, npass_ref, rows_ref, idx_ref, lr_ref, gate_ref, x1_ref, g_ref, o_ref,
                    wbuf, sem, *, tm):
    i = pl.program_id(0)
    slot = i % 2
    win = COMBINE_WINDOW
    width = N_EXPERTS * win

    def start_fetch(tile, p, s):
        last_start = rows_ref.shape[0] - win
        for e in range(N_EXPERTS):
            start = jnp.minimum(base_ref[tile * N_EXPERTS + e] + p * win, last_start)
            start = pl.multiple_of(start, WINDOW_ALIGN)
            pltpu.make_async_copy(rows_ref.at[pl.ds(start, win), :],
                                  wbuf.at[s, pl.ds(e * win, win), :], sem.at[s]).start()

    def wait_fetch(s):
        pltpu.make_async_copy(rows_ref.at[pl.ds(0, width), :], wbuf.at[s], sem.at[s]).wait()

    @pl.when(i == 0)
    def _():
        start_fetch(0, 0, 0)

    @pl.when(i + 1 < pl.num_programs(0))
    def _():
        start_fetch(i + 1, 0, 1 - slot)

    idx = idx_ref[...]
    lr = lr_ref[...]
    gates = gate_ref[...]
    col = lax.broadcasted_iota(jnp.int32, (tm, width), 1)

    def weights(p):
        w = jnp.zeros((tm, width), F32)
        for k in range(TOP_K):
            r = lr[:, k:k + 1] - p * win
            tgt = jnp.where((r >= 0) & (r < win), idx[:, k:k + 1] * win + r, -1)
            w = jnp.where(col == tgt, gates[:, k:k + 1], w)
        return w.astype(BF16)

    def gathered(p):
        return jnp.dot(weights(p), wbuf[slot], preferred_element_type=F32)

    wait_fetch(slot)
    y = x1_ref[...] + gathered(0)

    def extra_pass(p, acc):
        start_fetch(i, p, slot)
        wait_fetch(slot)
        return acc + gathered(p)

    y = lax.fori_loop(1, npass_ref[i], extra_pass, y)
    ms = jnp.mean(y * y, axis=-1, keepdims=True)
    o_ref[...] = y * lax.rsqrt(ms + NORM_EPS) * g_ref[...]


def _combine(base_tbl, npass, rows, idx_t, lr_t, gates_t, x1, g, tm):
    t, d = x1.shape
    kern = functools.partial(_combine_kernel, tm=tm)
    tok = lambda i, *_: (i, 0)
    return pl.pallas_call(
        kern,
        out_shape=jax.ShapeDtypeStruct((t, d), F32),
        grid_spec=pltpu.PrefetchScalarGridSpec(
            num_scalar_prefetch=2,
            grid=(t // tm,),
            in_specs=[pl.BlockSpec(memory_space=pl.ANY),
                      pl.BlockSpec((tm, TOP_K), tok), pl.BlockSpec((tm, TOP_K), tok),
                      pl.BlockSpec((tm, TOP_K), tok), pl.BlockSpec((tm, d), tok),
                      pl.BlockSpec((1, d), lambda i, *_: (0, 0))],
            out_specs=pl.BlockSpec((tm, d), tok),
            scratch_shapes=[pltpu.VMEM((2, N_EXPERTS * COMBINE_WINDOW, d), rows.dtype),
                            pltpu.SemaphoreType.DMA((2,))],
        ),
        compiler_params=_params(("arbitrary",)),
        name="combine",
    )(base_tbl, npass, rows, idx_t, lr_t, gates_t, x1, g)


def _rope_tables(seq):
    inv = 1.0 / (ROPE_THETA ** (jnp.arange(0, HEAD_DIM, 2, dtype=F32) / HEAD_DIM))
    ang = jnp.arange(seq, dtype=F32)[:, None] * inv[None, :]
    cos, sin = jnp.cos(ang), jnp.sin(ang)
    return jnp.concatenate([cos] * 4, axis=1), jnp.concatenate([-sin, sin, -sin, sin], axis=1)


def kernel(x, norm_mix_g, w_in, lambda_q1, lambda_k1, lambda_q2, lambda_k2, da_subln_g, w_da_out, w_sb_out, w_o, norm_ffn_g, w_router, b_router, w_gate, b_gate, w_up, b_up, w_down, b_down, norm_final_g):
    b, s, d = x.shape
    depth = w_in.shape[0]
    t = b * s
    cos_t, sin_t = _rope_tables(s)
    n_blocks = (t * TOP_K + ROW_BLOCK - 1) // ROW_BLOCK + N_EXPERTS
    p_rows = n_blocks * ROW_BLOCK
    x2 = x.reshape(t, d)
    for l in range(depth):
        lambda_init = 0.8 - 0.6 * math.exp(-0.3 * l)
        qa, ka, va, qb, kb, vb, ga, gb = _in_proj(
            x2, norm_mix_g[l][None, :], w_in[l].astype(BF16), cos_t, sin_t, s)
        lam_p = jnp.stack([lambda_q1[l], lambda_k1[l], lambda_q2[l], lambda_k2[l]]).astype(F32)
        seq3 = lambda a: a.reshape(b, s, a.shape[1])
        oa = _diff_attn(seq3(qa), seq3(ka), seq3(va), lam_p, da_subln_g[l][None, :].astype(F32), lambda_init)
        ob = _sb_attn(seq3(qb), seq3(kb), seq3(vb))
        wr = w_router[l].astype(F32)
        wr_top = wr.astype(BF16)
        wr_rest = (wr - wr_top.astype(F32)).astype(BF16)
        lane_pad = lambda a: jnp.pad(a, ((0, 0), (0, LANES - a.shape[1])))
        wr_hi = lane_pad(jnp.concatenate([wr_top, wr_rest], axis=1))
        wr_lo = lane_pad(wr_top)
        x1, h, idx, gates, rank, cnt = _post_attn(
            oa.reshape(t, -1), ob.reshape(t, -1), ga, gb, x2,
            w_da_out[l].astype(BF16), w_sb_out[l].astype(BF16), w_o[l].astype(BF16),
            norm_ffn_g[l][None, :], wr_hi, wr_lo, b_router[l][:, None].astype(F32))
        counts = cnt[:, 0].astype(jnp.int32)
        padded = (counts + ROW_BLOCK - 1) // ROW_BLOCK * ROW_BLOCK
        pad_ends = jnp.cumsum(padded)
        pad_starts = pad_ends - padded
        experts = jnp.arange(N_EXPERTS, dtype=jnp.int32)
        chosen = idx[:, :, None] == experts
        base = jnp.sum(jnp.where(chosen, pad_starts, 0), axis=-1)
        dest = (base + rank).reshape(-1)
        n_used = (pad_ends[-1] // ROW_BLOCK).astype(jnp.int32)
        first_blk = (pad_starts // ROW_BLOCK).astype(jnp.int32)
        blk_count = (padded // ROW_BLOCK).astype(jnp.int32)
        zero_blk = jnp.maximum(pad_ends // ROW_BLOCK - 1, 0).astype(jnp.int32)
        n_tiles = t // COMBINE_TILE
        tile_cnt = jnp.sum(chosen.reshape(TOP_K, n_tiles, COMBINE_TILE, N_EXPERTS), axis=(0, 2), dtype=jnp.int32)
        tile_carry = jnp.cumsum(tile_cnt, axis=0) - tile_cnt
        run_start = pad_starts[None, :] + tile_carry
        run_skew = run_start % WINDOW_ALIGN
        run_base = (run_start - run_skew).reshape(-1).astype(jnp.int32)
        shift_tok = jnp.repeat(tile_carry - run_skew, COMBINE_TILE, axis=0)
        local_rank = rank - jnp.sum(jnp.where(chosen, shift_tok[None], 0), axis=-1)
        n_pass = jnp.maximum((jnp.max(tile_cnt + run_skew, axis=1) + COMBINE_WINDOW - 1) // COMBINE_WINDOW,
                             1).astype(jnp.int32)
        xs = _dispatch(dest, zero_blk, n_used[None], h, p_rows)
        rows = _experts(first_blk, blk_count, n_used[None], n_blocks + 1, xs, w_gate[l], b_gate[l][:, None, :],
                        w_up[l], b_up[l][:, None, :], w_down[l], b_down[l][:, None, :])
        g_next = norm_final_g[None, :] if l == depth - 1 else jnp.ones((1, d), F32)
        x2 = _combine(run_base, n_pass, rows, idx.T, local_rank.T.astype(jnp.int32), gates.T, x1, g_next,
                      COMBINE_TILE)
        if l != depth - 1:
            raise NotImplementedError("only the final layer's norm is fused into the combine kernel")
    return x2.reshape(b, s, d)
```

```python
import functools
import math

import jax
import jax.numpy as jnp
from jax import lax
from jax.experimental import pallas as pl
from jax.experimental.pallas import tpu as pltpu

F32 = jnp.float32
BF16 = jnp.bfloat16

DA_HEADS = 4
HEAD_DIM = 64
N_EXPERTS = 32
TOP_K = 4
ROPE_THETA = 10000.0
SWIGLU_LIMIT = 7.0
SWIGLU_ALPHA = 1.702
NORM_EPS = 1e-5
ROW_BLOCK = 256
COMBINE_TILE = 256
WINDOW_ALIGN = 16
COMBINE_WINDOW = 80
LANES = 128
NEG_BIG = -1e30
LOG2E = math.log2(math.e)
SB_DEAD_LOG2 = -150.0

VMEM_LIMIT = 56 * 1024 * 1024


def _params(sem, vmem=VMEM_LIMIT):
    return pltpu.CompilerParams(dimension_semantics=sem, vmem_limit_bytes=vmem)


def _in_proj_kernel(x_ref, g_ref, w_ref, cos_ref, sin_ref,
                    qa_ref, ka_ref, va_ref, qb_ref, kb_ref, vb_ref, ga_ref, gb_ref):
    x = x_ref[...]
    ms = jnp.mean(x * x, axis=-1, keepdims=True)
    h = (x * lax.rsqrt(ms + NORM_EPS) * g_ref[...]).astype(BF16)
    cos = cos_ref[...]
    sin = sin_ref[...]
    lane = lax.broadcasted_iota(jnp.int32, cos.shape, 1)
    first_half = (lane & (HEAD_DIM - 1)) < HEAD_DIM // 2

    def proj(c0, width):
        return jnp.dot(h, w_ref[:, c0:c0 + width], preferred_element_type=F32)

    def rope(r):
        outs = []
        for g in range(r.shape[1] // LANES):
            xg = r[:, g * LANES:(g + 1) * LANES]
            rot = jnp.where(first_half, pltpu.roll(xg, LANES - HEAD_DIM // 2, 1),
                            pltpu.roll(xg, HEAD_DIM // 2, 1))
            outs.append(xg * cos + rot * sin)
        return jnp.concatenate(outs, axis=1)

    scale = HEAD_DIM ** -0.5 * LOG2E
    w = qa_ref.shape[1]
    d = ga_ref.shape[1]
    qa_ref[...] = (rope(proj(0, w)) * scale).astype(BF16)
    ka_ref[...] = rope(proj(w, w)).astype(BF16)
    va_ref[...] = proj(2 * w, w).astype(BF16)
    qb_ref[...] = (proj(3 * w, w) * scale).astype(BF16)
    kb_ref[...] = proj(4 * w, w).astype(BF16)
    vb_ref[...] = proj(5 * w, w).astype(BF16)
    ga_ref[...] = jax.nn.sigmoid(proj(6 * w, d)).astype(BF16)
    gb_ref[...] = jax.nn.sigmoid(proj(6 * w + d, d)).astype(BF16)


def _in_proj(x2, g, w_in_bf, cos_t, sin_t, seq, tm=512):
    t, d = x2.shape
    w = 512
    nseq = seq // tm
    outs = [jax.ShapeDtypeStruct((t, w), BF16)] * 6 + [jax.ShapeDtypeStruct((t, d), BF16)] * 2
    row = lambda i: (i, 0)
    return pl.pallas_call(
        _in_proj_kernel,
        out_shape=outs,
        grid=(t // tm,),
        in_specs=[
            pl.BlockSpec((tm, d), row),
            pl.BlockSpec((1, d), lambda i: (0, 0)),
            pl.BlockSpec(w_in_bf.shape, lambda i: (0, 0)),
            pl.BlockSpec((tm, LANES), lambda i: (i % nseq, 0)),
            pl.BlockSpec((tm, LANES), lambda i: (i % nseq, 0)),
        ],
        out_specs=[pl.BlockSpec((tm, w), row)] * 6 + [pl.BlockSpec((tm, d), row)] * 2,
        compiler_params=_params(("arbitrary",)),
        name="in_proj",
    )(x2, g, w_in_bf, cos_t, sin_t)


def _lane_groups(x):
    return [x[:, g * LANES:(g + 1) * LANES] for g in range(x.shape[1] // LANES)]


def _da_kernel(q_ref, k_ref, v_ref, lam_ref, subg_ref, o_ref, s_buf, mx_ref, m_ref, acc_ref,
               *, tq, tk, lambda_init):
    i = pl.program_id(2)
    q = q_ref[0]
    lane = lax.broadcasted_iota(jnp.int32, q.shape, 1)
    zero = jnp.zeros_like(q)
    qs = (jnp.where(lane < HEAD_DIM, q, zero), jnp.where(lane >= HEAD_DIM, q, zero))
    mx_ref[...] = jnp.full(mx_ref.shape, NEG_BIG, F32)
    acc_ref[...] = jnp.zeros(acc_ref.shape, F32)

    def score_blocks(js, diag_last):
        maxes = [mx_ref[0], mx_ref[1]]
        for n, j in enumerate(js):
            k = k_ref[0, pl.ds(pl.multiple_of(j * tk, tk), tk), :]
            for h in range(2):
                s = lax.dot_general(qs[h], k, (((1,), (1,)), ((), ())), preferred_element_type=F32)
                if diag_last and n == len(js) - 1:
                    rows = lax.broadcasted_iota(jnp.int32, s.shape, 0)
                    cols = lax.broadcasted_iota(jnp.int32, s.shape, 1)
                    s = jnp.where(rows >= cols, s, NEG_BIG)
                s_buf[h, j] = s
                for sg in _lane_groups(s):
                    maxes[h] = jnp.maximum(maxes[h], sg)
        mx_ref[0], mx_ref[1] = maxes

    def score_body(jj, carry):
        score_blocks([2 * jj, 2 * jj + 1], False)
        return carry

    lax.fori_loop(0, i // 2, score_body, 0)

    @pl.when(i % 2 == 0)
    def _():
        score_blocks([i], True)

    @pl.when(i % 2 == 1)
    def _():
        score_blocks([i - 1, i], True)

    for h in range(2):
        m_ref[h] = jnp.broadcast_to(jnp.max(mx_ref[h], axis=1, keepdims=True), (tq, LANES))

    def pv_blocks(js):
        vs = [v_ref[0, pl.ds(pl.multiple_of(j * tk, tk), tk), :] for j in js]
        v_all = jnp.concatenate(vs, axis=0)
        v_ext = jnp.concatenate([v_all, jnp.ones_like(v_all)], axis=1)
        accs = [acc_ref[0], acc_ref[1]]
        for h in range(2):
            m = m_ref[h]
            p = jnp.concatenate([jnp.exp2(sg - m).astype(BF16)
                                 for j in js for sg in _lane_groups(s_buf[h, j])], axis=1)
            accs[h] = accs[h] + jnp.dot(p, v_ext, preferred_element_type=F32)
        acc_ref[0], acc_ref[1] = accs

    def pv_body(jj, carry):
        pv_blocks([2 * jj, 2 * jj + 1])
        return carry

    lax.fori_loop(0, (i + 1) // 2, pv_body, 0)

    @pl.when(i % 2 == 0)
    def _():
        pv_blocks([i])

    lam_p = lam_ref[...]
    lam = (jnp.exp(jnp.sum(lam_p[0:1] * lam_p[1:2], axis=1, keepdims=True))
           - jnp.exp(jnp.sum(lam_p[2:3] * lam_p[3:4], axis=1, keepdims=True)) + lambda_init)
    a0, a1 = acc_ref[0], acc_ref[1]
    o = a0[:, :LANES] / a0[:, LANES:] - lam * (a1[:, :LANES] / a1[:, LANES:])
    ms = jnp.mean(o * o, axis=-1, keepdims=True)
    y = o * lax.rsqrt(ms + NORM_EPS) * subg_ref[...] * (1.0 - lambda_init)
    o_ref[0] = y.astype(o_ref.dtype)


def _diff_attn(qa, ka, va, lam_p, subg, lambda_init, tq=512):
    b, s, w = qa.shape
    nh = w // LANES
    kern = functools.partial(_da_kernel, tq=tq, tk=tq, lambda_init=lambda_init)
    qspec = pl.BlockSpec((1, tq, LANES), lambda bb, h, i: (bb, i, h))
    kvspec = pl.BlockSpec((1, s, LANES), lambda bb, h, i: (bb, 0, h))
    return pl.pallas_call(
        kern,
        out_shape=jax.ShapeDtypeStruct((b, s, w), BF16),
        grid=(b, nh, s // tq),
        in_specs=[qspec, kvspec, kvspec,
                  pl.BlockSpec(lam_p.shape, lambda bb, h, i: (0, 0)),
                  pl.BlockSpec(subg.shape, lambda bb, h, i: (0, 0))],
        out_specs=qspec,
        scratch_shapes=[pltpu.VMEM((2, s // tq, tq, tq), F32),
                        pltpu.VMEM((2, tq, LANES), F32),
                        pltpu.VMEM((2, tq, LANES), F32),
                        pltpu.VMEM((2, tq, 2 * LANES), F32)],
        compiler_params=_params(("arbitrary",) * 3),
        name="diff_attn",
    )(qa, ka, va, lam_p, subg)


def _sb_kernel(q_ref, k_ref, v_ref, tri_ref, o_ref, c_ref, acc_ref, *, tq, tk):
    i = pl.program_id(2)
    n_tiles = tq // tk
    tri = tri_ref[...]
    lane = lax.broadcasted_iota(jnp.int32, (tk, LANES), 1)
    qs = []
    for r in range(n_tiles):
        q = q_ref[0, r * tk:(r + 1) * tk, :]
        zero = jnp.zeros_like(q)
        qs.append((jnp.where(lane < HEAD_DIM, q, zero), jnp.where(lane >= HEAD_DIM, q, zero)))
    c_ref[...] = jnp.zeros(c_ref.shape, F32)
    acc_ref[...] = jnp.zeros(acc_ref.shape, F32)

    def load_kv(j):
        rows = pl.ds(pl.multiple_of(j * tk, tk), tk)
        return k_ref[0, rows, :], v_ref[0, rows, :]

    def window(jobs):
        keys = [(r, h) for r, _, _ in jobs for h in range(2)]
        state = {key: (c_ref[key], acc_ref[key]) for key in keys}
        new_state = {}
        for r, kvs, diag_first in jobs:
            v_all = jnp.concatenate([v for _, v in kvs], axis=0)
            for h in range(2):
                c_run, acc_prev = state[(r, h)]
                weights = []
                for n, (k, _) in enumerate(kvs):
                    masked = diag_first and n == 0
                    z = lax.dot_general(qs[r][h], k, (((1,), (1,)), ((), ())), preferred_element_type=F32)
                    soft = jnp.maximum(z, 0.0) + jnp.log2(1.0 + jnp.exp2(-jnp.abs(z)))
                    log_b = z - soft
                    if masked:
                        mask = (lax.broadcasted_iota(jnp.int32, z.shape, 1)
                                < lax.broadcasted_iota(jnp.int32, z.shape, 0))
                        soft = jnp.where(mask, soft, 0.0)
                    res = jnp.dot(soft.astype(BF16), tri, preferred_element_type=F32)
                    suffix, total = res[:, :tk], res[:, tk:]
                    a = jnp.concatenate([jnp.exp2(lb + sf + c_run)
                                         for lb, sf in zip(_lane_groups(log_b), _lane_groups(suffix))], axis=1)
                    if masked:
                        a = jnp.where(mask, a, 0.0)
                    weights.append(a.astype(BF16))
                    c_run = c_run + total
                a_all = jnp.concatenate(weights, axis=1)
                new_state[(r, h)] = (c_run, acc_prev + jnp.dot(a_all, v_all, preferred_element_type=F32))
        for key in keys:
            c_ref[key], acc_ref[key] = new_state[key]

    first = i * n_tiles

    @pl.when(i == 0)
    def _():
        blocks = [load_kv(r) for r in range(n_tiles)]
        window([(r, blocks[max(r - 1, 0):r + 1][::-1], True) for r in range(n_tiles)])

    @pl.when(i > 0)
    def _():
        blocks = [load_kv(first - 1 + n) for n in range(n_tiles + 1)]
        window([(r, [blocks[r + 1], blocks[r]], True) for r in range(n_tiles)])

    for r in range(n_tiles):
        def live(r=r):
            return jnp.max(jnp.maximum(c_ref[r, 0], c_ref[r, 1])) > SB_DEAD_LOG2

        def cond(carry):
            j, alive = carry
            return jnp.logical_and(j >= 0, alive)

        def body(carry, r=r, live=live):
            j, _ = carry
            window([(r, [load_kv(j)], False)])
            return j - 1, live()

        lax.while_loop(cond, body, (first + r - 2, live()))
        o_ref[0, r * tk:(r + 1) * tk, :] = jnp.where(lane < HEAD_DIM, acc_ref[r, 0],
                                                     acc_ref[r, 1]).astype(o_ref.dtype)


def _sb_attn(qb, kb, vb, tq=512, tk=256):
    b, s, w = qb.shape
    nh = w // LANES
    r = lax.broadcasted_iota(jnp.int32, (tk, tk), 0)
    c = lax.broadcasted_iota(jnp.int32, (tk, tk), 1)
    tri = -jnp.concatenate([(r > c).astype(BF16), jnp.ones((tk, LANES), BF16)], axis=1)
    kern = functools.partial(_sb_kernel, tq=tq, tk=tk)
    qspec = pl.BlockSpec((1, tq, LANES), lambda bb, h, i: (bb, i, h))
    kvspec = pl.BlockSpec((1, s, LANES), lambda bb, h, i: (bb, 0, h))
    return pl.pallas_call(
        kern,
        out_shape=jax.ShapeDtypeStruct((b, s, w), BF16),
        grid=(b, nh, s // tq),
        in_specs=[qspec, kvspec, kvspec, pl.BlockSpec(tri.shape, lambda bb, h, i: (0, 0))],
        out_specs=qspec,
        scratch_shapes=[pltpu.VMEM((tq // tk, 2, tk, LANES), F32), pltpu.VMEM((tq // tk, 2, tk, LANES), F32)],
        compiler_params=_params(("arbitrary",) * 3),
        name="sb_attn",
    )(qb, kb, vb, tri)


def _post_attn_kernel(oa_ref, ob_ref, ga_ref, gb_ref, x_ref, wda_ref, wsb_ref, wo_ref, g_ref,
                      wr_hi_ref, wr_lo_ref, br_ref, tri_ref,
                      x1_ref, h_ref, idx_ref, gate_ref, rank_ref, cnt_ref, carry_ref):
    i = pl.program_id(0)

    @pl.when(i == 0)
    def _():
        carry_ref[...] = jnp.zeros(carry_ref.shape, F32)

    ya = jnp.dot(oa_ref[...], wda_ref[...], preferred_element_type=F32)
    yb = jnp.dot(ob_ref[...], wsb_ref[...], preferred_element_type=F32)
    mix = ga_ref[...].astype(F32) * ya + gb_ref[...].astype(F32) * yb
    x1 = x_ref[...] + jnp.dot(mix.astype(BF16), wo_ref[...], preferred_element_type=F32)
    x1_ref[...] = x1
    ms = jnp.mean(x1 * x1, axis=-1, keepdims=True)
    h = x1 * lax.rsqrt(ms + NORM_EPS) * g_ref[...]
    h_ref[...] = h

    h_hi = h.astype(BF16)
    h_lo = (h - h_hi.astype(F32)).astype(BF16)
    ne = br_ref.shape[0]
    p_hi = jnp.dot(h_hi, wr_hi_ref[...], preferred_element_type=F32)
    p_lo = jnp.dot(h_lo, wr_lo_ref[...], preferred_element_type=F32)
    by_token = p_hi + pltpu.roll(p_hi, LANES - ne, 1) + p_lo
    logits = by_token.T[:ne] + br_ref[...]
    tm = logits.shape[1]
    eid = lax.broadcasted_iota(jnp.int32, (ne, tm), 0).astype(F32)
    vals, ids = [], []
    work = logits
    for _ in range(TOP_K):
        mx = jnp.max(work, axis=0, keepdims=True)
        sel = jnp.min(jnp.where(work == mx, eid, float(ne)), axis=0, keepdims=True)
        vals.append(mx)
        ids.append(sel)
        work = jnp.where(eid == sel, -jnp.inf, work)
    exps = [jnp.exp(v - vals[0]) for v in vals]
    denom = exps[0] + exps[1] + exps[2] + exps[3]
    onehots = [(eid == sel).astype(F32) for sel in ids]
    assigned = onehots[0] + onehots[1] + onehots[2] + onehots[3]
    before = jnp.dot(assigned.astype(BF16), tri_ref[...], preferred_element_type=F32) + carry_ref[...]
    for r in range(TOP_K):
        idx_ref[r:r + 1, :] = ids[r].astype(jnp.int32)
        gate_ref[r:r + 1, :] = exps[r] / denom
        rank_ref[r:r + 1, :] = jnp.sum(onehots[r] * before, axis=0, keepdims=True).astype(jnp.int32)
    carry = carry_ref[...] + jnp.sum(assigned, axis=1, keepdims=True)
    carry_ref[...] = carry
    cnt_ref[...] = jnp.broadcast_to(carry, cnt_ref.shape)


def _post_attn(oa, ob, ga, gb, x2, wda, wsb, wo, g, wr_hi, wr_lo, br, tm=512):
    t, d = x2.shape
    w = oa.shape[1]
    r = lax.broadcasted_iota(jnp.int32, (tm, tm), 0)
    c = lax.broadcasted_iota(jnp.int32, (tm, tm), 1)
    tri = (r < c).astype(BF16)
    row = lambda i: (i, 0)
    col = lambda i: (0, i)
    const = lambda i: (0, 0)
    full = lambda a: pl.BlockSpec(a.shape, const)
    return pl.pallas_call(
        _post_attn_kernel,
        out_shape=[jax.ShapeDtypeStruct((t, d), F32), jax.ShapeDtypeStruct((t, d), F32),
                   jax.ShapeDtypeStruct((TOP_K, t), jnp.int32), jax.ShapeDtypeStruct((TOP_K, t), F32),
                   jax.ShapeDtypeStruct((TOP_K, t), jnp.int32),
                   jax.ShapeDtypeStruct((N_EXPERTS, LANES), F32)],
        grid=(t // tm,),
        in_specs=[pl.BlockSpec((tm, w), row), pl.BlockSpec((tm, w), row),
                  pl.BlockSpec((tm, d), row), pl.BlockSpec((tm, d), row), pl.BlockSpec((tm, d), row),
                  full(wda), full(wsb), full(wo), full(g), full(wr_hi), full(wr_lo), full(br), full(tri)],
        out_specs=[pl.BlockSpec((tm, d), row), pl.BlockSpec((tm, d), row),
                   pl.BlockSpec((TOP_K, tm), col), pl.BlockSpec((TOP_K, tm), col),
                   pl.BlockSpec((TOP_K, tm), col), pl.BlockSpec((N_EXPERTS, LANES), const)],
        scratch_shapes=[pltpu.VMEM((N_EXPERTS, 1), F32)],
        compiler_params=_params(("arbitrary",)),
        name="post_attn",
    )(oa, ob, ga, gb, x2, wda, wsb, wo, g, wr_hi, wr_lo, br, tri)


def _dispatch_kernel(dest_ref, zero_blk_ref, n_used_ref, h_ref, xs_ref, zeros, sem, zsem, *, tm, t_total):
    i = pl.program_id(0)
    n_blocks = xs_ref.shape[0] // ROW_BLOCK

    def zero_copy(blk):
        row = pl.multiple_of(blk * ROW_BLOCK, ROW_BLOCK)
        return pltpu.make_async_copy(zeros, xs_ref.at[pl.ds(row, ROW_BLOCK), :], zsem)

    @pl.when(i == 0)
    def _():
        zeros[...] = jnp.zeros(zeros.shape, zeros.dtype)
        n_tail = n_blocks - n_used_ref[0]

        def start(n, carry):
            zero_copy(jnp.where(n < N_EXPERTS, zero_blk_ref[jnp.minimum(n, N_EXPERTS - 1)],
                                n_used_ref[0] + n - N_EXPERTS)).start()
            return carry

        def wait(n, carry):
            zero_copy(0).wait()
            return carry

        lax.fori_loop(0, N_EXPERTS + n_tail, start, 0)
        lax.fori_loop(0, N_EXPERTS + n_tail, wait, 0)

    def row_copy(t, k):
        dst = dest_ref[k * t_total + i * tm + t]
        return pltpu.make_async_copy(h_ref.at[pl.ds(t, 1), :], xs_ref.at[pl.ds(dst, 1), :], sem)

    def issue(t, carry):
        for k in range(TOP_K):
            row_copy(t, k).start(priority=k % 2)
        return carry

    lax.fori_loop(0, tm, issue, 0, unroll=8)
    for _ in range(TOP_K):
        pltpu.make_async_copy(h_ref, xs_ref.at[pl.ds(0, tm), :], sem).wait()


def _dispatch(dest_flat, zero_blk, n_used, h, p_rows, tm=256):
    t, d = h.shape
    kern = functools.partial(_dispatch_kernel, tm=tm, t_total=t)
    return pl.pallas_call(
        kern,
        out_shape=jax.ShapeDtypeStruct((p_rows, d), h.dtype),
        grid_spec=pltpu.PrefetchScalarGridSpec(
            num_scalar_prefetch=3,
            grid=(t // tm,),
            in_specs=[pl.BlockSpec((tm, d), lambda i, *_: (i, 0))],
            out_specs=pl.BlockSpec(memory_space=pl.ANY),
            scratch_shapes=[pltpu.VMEM((ROW_BLOCK, d), h.dtype), pltpu.SemaphoreType.DMA,
                            pltpu.SemaphoreType.DMA],
        ),
        compiler_params=_params(("arbitrary",)),
        name="dispatch",
    )(dest_flat, zero_blk, n_used, h)


def _expert_kernel(first_ref, count_ref, n_used_ref, xs_ref, wg_ref, bg_ref, wu_ref, bu_ref, wd_ref, bd_ref,
                   o_ref, wg_bf, wu_bf, wd_bf, xbuf, obuf, in_sem, out_sem):
    e = pl.program_id(0)
    first = first_ref[e]
    count = count_ref[e]

    n_pairs = count // 2
    odd = count % 2

    def rows(blk, nb):
        return pl.ds(pl.multiple_of(blk * ROW_BLOCK, ROW_BLOCK), nb * ROW_BLOCK)

    def fetch(blk, nb, slot):
        return pltpu.make_async_copy(xs_ref.at[rows(blk, nb), :], xbuf.at[slot, pl.ds(0, nb * ROW_BLOCK), :],
                                     in_sem.at[slot])

    def flush(blk, nb, slot):
        return pltpu.make_async_copy(obuf.at[slot, pl.ds(0, nb * ROW_BLOCK), :], o_ref.at[rows(blk, nb), :],
                                     out_sem.at[slot])

    def mlp(x):
        x = x.astype(BF16)
        g = jnp.dot(x, wg_bf[...], preferred_element_type=F32) + bg_ref[0]
        u = jnp.dot(x, wu_bf[...], preferred_element_type=F32) + bu_ref[0]
        g = jnp.minimum(g, SWIGLU_LIMIT)
        u = jnp.clip(u, -SWIGLU_LIMIT, SWIGLU_LIMIT)
        glu = g * jax.nn.sigmoid(SWIGLU_ALPHA * g)
        act = ((u + 1.0) * glu).astype(BF16)
        return (jnp.dot(act, wd_bf[...], preferred_element_type=F32) + bd_ref[0]).astype(obuf.dtype)

    def start_item(first_x, count_x, k):
        @pl.when(2 * k + 1 < count_x)
        def _():
            fetch(first_x + 2 * k, 2, k % 2).start()

        @pl.when(2 * k + 1 == count_x)
        def _():
            fetch(first_x + 2 * k, 1, k % 2).start()

    @pl.when(e == 0)
    def _():
        start_item(first, count, 0)
        start_item(first, count, 1)

    @pl.when(count > 0)
    def _():
        wg_bf[...] = wg_ref[0].astype(BF16)
        wu_bf[...] = wu_ref[0].astype(BF16)
        wd_bf[...] = wd_ref[0].astype(BF16)

    def pair(jj, carry):
        slot = jj % 2
        blk = first + 2 * jj
        fetch(blk, 2, slot).wait()
        y = mlp(xbuf[slot])
        start_item(first, count, jj + 2)

        @pl.when(jj >= 2)
        def _():
            flush(blk - 4, 2, slot).wait()

        obuf[slot] = y
        flush(blk, 2, slot).start()
        return carry

    lax.fori_loop(0, n_pairs, pair, 0)
    last = first + 2 * n_pairs
    last_slot = n_pairs % 2

    @pl.when(odd == 1)
    def _():
        fetch(last, 1, last_slot).wait()
        y = mlp(xbuf[last_slot, :ROW_BLOCK, :])

        @pl.when(n_pairs >= 2)
        def _():
            flush(last - 4, 2, last_slot).wait()

        obuf[last_slot, :ROW_BLOCK, :] = y
        flush(last, 1, last_slot).start()

    @pl.when(n_pairs >= 1)
    def _():
        flush(last - 2, 2, 1 - last_slot).wait()

    @pl.when((n_pairs >= 2) & (odd == 0))
    def _():
        flush(last - 4, 2, last_slot).wait()

    @pl.when(odd == 1)
    def _():
        flush(last, 1, last_slot).wait()

    @pl.when(e + 1 < pl.num_programs(0))
    def _():
        start_item(first_ref[e + 1], count_ref[e + 1], 0)
        start_item(first_ref[e + 1], count_ref[e + 1], 1)

    @pl.when(e == pl.num_programs(0) - 1)
    def _():
        obuf[0, :ROW_BLOCK, :] = jnp.zeros((ROW_BLOCK, obuf.shape[2]), obuf.dtype)
        n_used = n_used_ref[0]
        n_total = o_ref.shape[0] // ROW_BLOCK

        def start(blk, carry):
            flush(blk, 1, 0).start()
            return carry

        def wait(blk, carry):
            flush(blk, 1, 0).wait()
            return carry

        lax.fori_loop(n_used, n_total, start, 0)
        lax.fori_loop(n_used, n_total, wait, 0)


def _experts(first_blk, blk_count, n_used, n_out_blocks, xs, wg, bg, wu, bu, wd, bd):
    d = xs.shape[1]
    n_exp, _, f = wg.shape
    wmap = lambda e, *_: (e, 0, 0)
    return pl.pallas_call(
        _expert_kernel,
        out_shape=jax.ShapeDtypeStruct((n_out_blocks * ROW_BLOCK, d), BF16),
        grid_spec=pltpu.PrefetchScalarGridSpec(
            num_scalar_prefetch=3,
            grid=(n_exp,),
            in_specs=[pl.BlockSpec(memory_space=pl.ANY),
                      pl.BlockSpec((1, d, f), wmap), pl.BlockSpec((1, 1, f), wmap),
                      pl.BlockSpec((1, d, f), wmap), pl.BlockSpec((1, 1, f), wmap),
                      pl.BlockSpec((1, f, d), wmap), pl.BlockSpec((1, 1, d), wmap)],
            out_specs=pl.BlockSpec(memory_space=pl.ANY),
            scratch_shapes=[pltpu.VMEM((d, f), BF16), pltpu.VMEM((d, f), BF16), pltpu.VMEM((f, d), BF16),
                            pltpu.VMEM((2, 2 * ROW_BLOCK, d), F32), pltpu.VMEM((2, 2 * ROW_BLOCK, d), BF16),
                            pltpu.SemaphoreType.DMA((2,)), pltpu.SemaphoreType.DMA((2,))],
        ),
        compiler_params=_params(("arbitrary",)),
        name="experts",
    )(first_blk, blk_count, n_used, xs, wg, bg, wu, bu, wd, bd)


def _combine_kernel(base_ref, npass_ref, rows_ref, idx_ref, lr_ref, gate_ref, x1_ref, g_ref, o_ref,
                    wbuf, sem, *, tm):
    i = pl.program_id(0)
    slot = i % 2
    win = COMBINE_WINDOW
    width = N_EXPERTS * win

    def start_fetch(tile, p, s):
        last_start = rows_ref.shape[0] - win
        for e in range(N_EXPERTS):
            start = jnp.minimum(base_ref[tile * N_EXPERTS + e] + p * win, last_start)
            start = pl.multiple_of(start, WINDOW_ALIGN)
            pltpu.make_async_copy(rows_ref.at[pl.ds(start, win), :],
                                  wbuf.at[s, pl.ds(e * win, win), :], sem.at[s]).start()

    def wait_fetch(s):
        pltpu.make_async_copy(rows_ref.at[pl.ds(0, width), :], wbuf.at[s], sem.at[s]).wait()

    @pl.when(i == 0)
    def _():
        start_fetch(0, 0, 0)

    @pl.when(i + 1 < pl.num_programs(0))
    def _():
        start_fetch(i + 1, 0, 1 - slot)

    idx = idx_ref[...]
    lr = lr_ref[...]
    gates = gate_ref[...]
    col = lax.broadcasted_iota(jnp.int32, (tm, width), 1)

    def weights(p):
        w = jnp.zeros((tm, width), F32)
        for k in range(TOP_K):
            r = lr[:, k:k + 1] - p * win
            tgt = jnp.where((r >= 0) & (r < win), idx[:, k:k + 1] * win + r, -1)
            w = jnp.where(col == tgt, gates[:, k:k + 1], w)
        return w.astype(BF16)

    def gathered(p):
        return jnp.dot(weights(p), wbuf[slot], preferred_element_type=F32)

    wait_fetch(slot)
    y = x1_ref[...] + gathered(0)

    def extra_pass(p, acc):
        start_fetch(i, p, slot)
        wait_fetch(slot)
        return acc + gathered(p)

    y = lax.fori_loop(1, npass_ref[i], extra_pass, y)
    ms = jnp.mean(y * y, axis=-1, keepdims=True)
    o_ref[...] = y * lax.rsqrt(ms + NORM_EPS) * g_ref[...]


def _combine(base_tbl, npass, rows, idx_t, lr_t, gates_t, x1, g, tm):
    t, d = x1.shape
    kern = functools.partial(_combine_kernel, tm=tm)
    tok = lambda i, *_: (i, 0)
    return pl.pallas_call(
        kern,
        out_shape=jax.ShapeDtypeStruct((t, d), F32),
        grid_spec=pltpu.PrefetchScalarGridSpec(
            num_scalar_prefetch=2,
            grid=(t // tm,),
            in_specs=[pl.BlockSpec(memory_space=pl.ANY),
                      pl.BlockSpec((tm, TOP_K), tok), pl.BlockSpec((tm, TOP_K), tok),
                      pl.BlockSpec((tm, TOP_K), tok), pl.BlockSpec((tm, d), tok),
                      pl.BlockSpec((1, d), lambda i, *_: (0, 0))],
            out_specs=pl.BlockSpec((tm, d), tok),
            scratch_shapes=[pltpu.VMEM((2, N_EXPERTS * COMBINE_WINDOW, d), rows.dtype),
                            pltpu.SemaphoreType.DMA((2,))],
        ),
        compiler_params=_params(("arbitrary",)),
        name="combine",
    )(base_tbl, npass, rows, idx_t, lr_t, gates_t, x1, g)


def _rope_tables(seq):
    inv = 1.0 / (ROPE_THETA ** (jnp.arange(0, HEAD_DIM, 2, dtype=F32) / HEAD_DIM))
    ang = jnp.arange(seq, dtype=F32)[:, None] * inv[None, :]
    cos, sin = jnp.cos(ang), jnp.sin(ang)
    return jnp.concatenate([cos] * 4, axis=1), jnp.concatenate([-sin, sin, -sin, sin], axis=1)


def kernel(x, norm_mix_g, w_in, lambda_q1, lambda_k1, lambda_q2, lambda_k2, da_subln_g, w_da_out, w_sb_out, w_o, norm_ffn_g, w_router, b_router, w_gate, b_gate, w_up, b_up, w_down, b_down, norm_final_g):
    b, s, d = x.shape
    depth = w_in.shape[0]
    t = b * s
    cos_t, sin_t = _rope_tables(s)
    n_blocks = (t * TOP_K + ROW_BLOCK - 1) // ROW_BLOCK + N_EXPERTS
    p_rows = n_blocks * ROW_BLOCK
    x2 = x.reshape(t, d)
    for l in range(depth):
        lambda_init = 0.8 - 0.6 * math.exp(-0.3 * l)
        qa, ka, va, qb, kb, vb, ga, gb = _in_proj(
            x2, norm_mix_g[l][None, :], w_in[l].astype(BF16), cos_t, sin_t, s)
        lam_p = jnp.stack([lambda_q1[l], lambda_k1[l], lambda_q2[l], lambda_k2[l]]).astype(F32)
        seq3 = lambda a: a.reshape(b, s, a.shape[1])
        oa = _diff_attn(seq3(qa), seq3(ka), seq3(va), lam_p, da_subln_g[l][None, :].astype(F32), lambda_init)
        ob = _sb_attn(seq3(qb), seq3(kb), seq3(vb))
        wr = w_router[l].astype(F32)
        wr_top = wr.astype(BF16)
        wr_rest = (wr - wr_top.astype(F32)).astype(BF16)
        lane_pad = lambda a: jnp.pad(a, ((0, 0), (0, LANES - a.shape[1])))
        wr_hi = lane_pad(jnp.concatenate([wr_top, wr_rest], axis=1))
        wr_lo = lane_pad(wr_top)
        x1, h, idx, gates, rank, cnt = _post_attn(
            oa.reshape(t, -1), ob.reshape(t, -1), ga, gb, x2,
            w_da_out[l].astype(BF16), w_sb_out[l].astype(BF16), w_o[l].astype(BF16),
            norm_ffn_g[l][None, :], wr_hi, wr_lo, b_router[l][:, None].astype(F32))
        counts = cnt[:, 0].astype(jnp.int32)
        padded = (counts + ROW_BLOCK - 1) // ROW_BLOCK * ROW_BLOCK
        pad_ends = jnp.cumsum(padded)
        pad_starts = pad_ends - padded
        experts = jnp.arange(N_EXPERTS, dtype=jnp.int32)
        chosen = idx[:, :, None] == experts
        base = jnp.sum(jnp.where(chosen, pad_starts, 0), axis=-1)
        dest = (base + rank).reshape(-1)
        n_used = (pad_ends[-1] // ROW_BLOCK).astype(jnp.int32)
        first_blk = (pad_starts // ROW_BLOCK).astype(jnp.int32)
        blk_count = (padded // ROW_BLOCK).astype(jnp.int32)
        zero_blk = jnp.maximum(pad_ends // ROW_BLOCK - 1, 0).astype(jnp.int32)
        n_tiles = t // COMBINE_TILE
        tile_cnt = jnp.sum(chosen.reshape(TOP_K, n_tiles, COMBINE_TILE, N_EXPERTS), axis=(0, 2), dtype=jnp.int32)
        tile_carry = jnp.cumsum(tile_cnt, axis=0) - tile_cnt
        run_start = pad_starts[None, :] + tile_carry
        run_skew = run_start % WINDOW_ALIGN
        run_base = (run_start - run_skew).reshape(-1).astype(jnp.int32)
        shift_tok = jnp.repeat(tile_carry - run_skew, COMBINE_TILE, axis=0)
        local_rank = rank - jnp.sum(jnp.where(chosen, shift_tok[None], 0), axis=-1)
        n_pass = jnp.maximum((jnp.max(tile_cnt + run_skew, axis=1) + COMBINE_WINDOW - 1) // COMBINE_WINDOW,
                             1).astype(jnp.int32)
        xs = _dispatch(dest, zero_blk, n_used[None], h, p_rows)
        rows = _experts(first_blk, blk_count, n_used[None], n_blocks + 1, xs, w_gate[l], b_gate[l][:, None, :],
                        w_up[l], b_up[l][:, None, :], w_down[l], b_down[l][:, None, :])
        g_next = norm_final_g[None, :] if l == depth - 1 else jnp.ones((1, d), F32)
        x2 = _combine(run_base, n_pass, rows, idx.T, local_rank.T.astype(jnp.int32), gates.T, x1, g_next,
                      COMBINE_TILE)
        if l != depth - 1:
            raise NotImplementedError("only the final layer's norm is fused into the combine kernel")
    return x2.reshape(b, s, d)
```

```python
import functools
import math

import jax
import jax.numpy as jnp
from jax import lax
from jax.experimental import pallas as pl
from jax.experimental.pallas import tpu as pltpu

F32 = jnp.float32
BF16 = jnp.bfloat16

DA_HEADS = 4
HEAD_DIM = 64
N_EXPERTS = 32
TOP_K = 4
ROPE_THETA = 10000.0
SWIGLU_LIMIT = 7.0
SWIGLU_ALPHA = 1.702
NORM_EPS = 1e-5
ROW_BLOCK = 256
COMBINE_TILE = 256
WINDOW_ALIGN = 16
COMBINE_WINDOW = 80
LANES = 128
NEG_BIG = -1e30
LOG2E = math.log2(math.e)
SB_DEAD_LOG2 = -150.0

VMEM_LIMIT = 56 * 1024 * 1024


def _params(sem, vmem=VMEM_LIMIT):
    return pltpu.CompilerParams(dimension_semantics=sem, vmem_limit_bytes=vmem)


def _in_proj_kernel(x_ref, g_ref, w_ref, cos_ref, sin_ref,
                    qa_ref, ka_ref, va_ref, qb_ref, kb_ref, vb_ref, ga_ref, gb_ref):
    x = x_ref[...]
    ms = jnp.mean(x * x, axis=-1, keepdims=True)
    h = (x * lax.rsqrt(ms + NORM_EPS) * g_ref[...]).astype(BF16)
    cos = cos_ref[...]
    sin = sin_ref[...]
    lane = lax.broadcasted_iota(jnp.int32, cos.shape, 1)
    first_half = (lane & (HEAD_DIM - 1)) < HEAD_DIM // 2

    def proj(c0, width):
        return jnp.dot(h, w_ref[:, c0:c0 + width], preferred_element_type=F32)

    def rope(r):
        outs = []
        for g in range(r.shape[1] // LANES):
            xg = r[:, g * LANES:(g + 1) * LANES]
            rot = jnp.where(first_half, pltpu.roll(xg, LANES - HEAD_DIM // 2, 1),
                            pltpu.roll(xg, HEAD_DIM // 2, 1))
            outs.append(xg * cos + rot * sin)
        return jnp.concatenate(outs, axis=1)

    scale = HEAD_DIM ** -0.5 * LOG2E
    w = qa_ref.shape[1]
    d = ga_ref.shape[1]
    qa_ref[...] = (rope(proj(0, w)) * scale).astype(BF16)
    ka_ref[...] = rope(proj(w, w)).astype(BF16)
    va_ref[...] = proj(2 * w, w).astype(BF16)
    qb_ref[...] = (proj(3 * w, w) * scale).astype(BF16)
    kb_ref[...] = proj(4 * w, w).astype(BF16)
    vb_ref[...] = proj(5 * w, w).astype(BF16)
    ga_ref[...] = jax.nn.sigmoid(proj(6 * w, d)).astype(BF16)
    gb_ref[...] = jax.nn.sigmoid(proj(6 * w + d, d)).astype(BF16)


def _in_proj(x2, g, w_in_bf, cos_t, sin_t, seq, tm=512):
    t, d = x2.shape
    w = 512
    nseq = seq // tm
    outs = [jax.ShapeDtypeStruct((t, w), BF16)] * 6 + [jax.ShapeDtypeStruct((t, d), BF16)] * 2
    row = lambda i: (i, 0)
    return pl.pallas_call(
        _in_proj_kernel,
        out_shape=outs,
        grid=(t // tm,),
        in_specs=[
            pl.BlockSpec((tm, d), row),
            pl.BlockSpec((1, d), lambda i: (0, 0)),
            pl.BlockSpec(w_in_bf.shape, lambda i: (0, 0)),
            pl.BlockSpec((tm, LANES), lambda i: (i % nseq, 0)),
            pl.BlockSpec((tm, LANES), lambda i: (i % nseq, 0)),
        ],
        out_specs=[pl.BlockSpec((tm, w), row)] * 6 + [pl.BlockSpec((tm, d), row)] * 2,
        compiler_params=_params(("arbitrary",)),
        name="in_proj",
    )(x2, g, w_in_bf, cos_t, sin_t)


def _lane_groups(x):
    return [x[:, g * LANES:(g + 1) * LANES] for g in range(x.shape[1] // LANES)]


def _da_kernel(q_ref, k_ref, v_ref, lam_ref, subg_ref, o_ref, s_buf, mx_ref, m_ref, acc_ref,
               *, tq, tk, lambda_init):
    i = pl.program_id(2)
    q = q_ref[0]
    lane = lax.broadcasted_iota(jnp.int32, q.shape, 1)
    zero = jnp.zeros_like(q)
    qs = (jnp.where(lane < HEAD_DIM, q, zero), jnp.where(lane >= HEAD_DIM, q, zero))
    mx_ref[...] = jnp.full(mx_ref.shape, NEG_BIG, F32)
    acc_ref[...] = jnp.zeros(acc_ref.shape, F32)

    def score_blocks(js, diag_last):
        maxes = [mx_ref[0], mx_ref[1]]
        for n, j in enumerate(js):
            k = k_ref[0, pl.ds(pl.multiple_of(j * tk, tk), tk), :]
            for h in range(2):
                s = lax.dot_general(qs[h], k, (((1,), (1,)), ((), ())), preferred_element_type=F32)
                if diag_last and n == len(js) - 1:
                    rows = lax.broadcasted_iota(jnp.int32, s.shape, 0)
                    cols = lax.broadcasted_iota(jnp.int32, s.shape, 1)
                    s = jnp.where(rows >= cols, s, NEG_BIG)
                s_buf[h, j] = s
                for sg in _lane_groups(s):
                    maxes[h] = jnp.maximum(maxes[h], sg)
        if diag_last:
            for h in range(2):
                m_ref[h] = jnp.broadcast_to(jnp.max(maxes[h], axis=1, keepdims=True), (tq, LANES))
        else:
            mx_ref[0], mx_ref[1] = maxes

    def score_body(jj, carry):
        score_blocks([2 * jj, 2 * jj + 1], False)
        return carry

    lax.fori_loop(0, i // 2, score_body, 0)

    @pl.when(i % 2 == 0)
    def _():
        score_blocks([i], True)

    @pl.when(i % 2 == 1)
    def _():
        score_blocks([i - 1, i], True)

    def pv_blocks(js):
        vs = [v_ref[0, pl.ds(pl.multiple_of(j * tk, tk), tk), :] for j in js]
        v_all = jnp.concatenate(vs, axis=0)
        v_ext = jnp.concatenate([v_all, jnp.ones_like(v_all)], axis=1)
        accs = [acc_ref[0], acc_ref[1]]
        for h in range(2):
            m = m_ref[h]
            p = jnp.concatenate([jnp.exp2(sg - m).astype(BF16)
                                 for j in js for sg in _lane_groups(s_buf[h, j])], axis=1)
            accs[h] = accs[h] + jnp.dot(p, v_ext, preferred_element_type=F32)
        acc_ref[0], acc_ref[1] = accs

    def pv_body(jj, carry):
        pv_blocks([2 * jj, 2 * jj + 1])
        return carry

    lax.fori_loop(0, (i + 1) // 2, pv_body, 0)

    @pl.when(i % 2 == 0)
    def _():
        pv_blocks([i])

    lam_p = lam_ref[...]
    lam = (jnp.exp(jnp.sum(lam_p[0:1] * lam_p[1:2], axis=1, keepdims=True))
           - jnp.exp(jnp.sum(lam_p[2:3] * lam_p[3:4], axis=1, keepdims=True)) + lambda_init)
    a0, a1 = acc_ref[0], acc_ref[1]
    o = a0[:, :LANES] / a0[:, LANES:] - lam * (a1[:, :LANES] / a1[:, LANES:])
    ms = jnp.mean(o * o, axis=-1, keepdims=True)
    y = o * lax.rsqrt(ms + NORM_EPS) * subg_ref[...] * (1.0 - lambda_init)
    o_ref[0] = y.astype(o_ref.dtype)


def _diff_attn(qa, ka, va, lam_p, subg, lambda_init, tq=512):
    b, s, w = qa.shape
    nh = w // LANES
    kern = functools.partial(_da_kernel, tq=tq, tk=tq, lambda_init=lambda_init)
    qspec = pl.BlockSpec((1, tq, LANES), lambda bb, h, i: (bb, i, h))
    kvspec = pl.BlockSpec((1, s, LANES), lambda bb, h, i: (bb, 0, h))
    return pl.pallas_call(
        kern,
        out_shape=jax.ShapeDtypeStruct((b, s, w), BF16),
        grid=(b, nh, s // tq),
        in_specs=[qspec, kvspec, kvspec,
                  pl.BlockSpec(lam_p.shape, lambda bb, h, i: (0, 0)),
                  pl.BlockSpec(subg.shape, lambda bb, h, i: (0, 0))],
        out_specs=qspec,
        scratch_shapes=[pltpu.VMEM((2, s // tq, tq, tq), F32),
                        pltpu.VMEM((2, tq, LANES), F32),
                        pltpu.VMEM((2, tq, LANES), F32),
                        pltpu.VMEM((2, tq, 2 * LANES), F32)],
        compiler_params=_params(("arbitrary",) * 3),
        name="diff_attn",
    )(qa, ka, va, lam_p, subg)


def _sb_kernel(q_ref, k_ref, v_ref, tri_ref, o_ref, c_ref, acc_ref, *, tq, tk):
    i = pl.program_id(2)
    n_tiles = tq // tk
    tri = tri_ref[...]
    lane = lax.broadcasted_iota(jnp.int32, (tk, LANES), 1)
    qs = []
    for r in range(n_tiles):
        q = q_ref[0, r * tk:(r + 1) * tk, :]
        zero = jnp.zeros_like(q)
        qs.append((jnp.where(lane < HEAD_DIM, q, zero), jnp.where(lane >= HEAD_DIM, q, zero)))

    def load_kv(j):
        rows = pl.ds(pl.multiple_of(j * tk, tk), tk)
        return k_ref[0, rows, :], v_ref[0, rows, :]

    def window(jobs):
        keys = [(r, h) for r, _, _ in jobs for h in range(2)]
        fresh = jnp.zeros((tk, LANES), F32)
        state = {(r, h): (fresh, fresh) if diag_first else (c_ref[r, h], acc_ref[r, h])
                 for r, _, diag_first in jobs for h in range(2)}
        new_state = {}
        for r, kvs, diag_first in jobs:
            v_all = jnp.concatenate([v for _, v in kvs], axis=0)
            for h in range(2):
                c_run, acc_prev = state[(r, h)]
                weights = []
                for n, (k, _) in enumerate(kvs):
                    masked = diag_first and n == 0
                    z = lax.dot_general(qs[r][h], k, (((1,), (1,)), ((), ())), preferred_element_type=F32)
                    soft = jnp.maximum(z, 0.0) + jnp.log2(1.0 + jnp.exp2(-jnp.abs(z)))
                    log_b = z - soft
                    if masked:
                        mask = (lax.broadcasted_iota(jnp.int32, z.shape, 1)
                                < lax.broadcasted_iota(jnp.int32, z.shape, 0))
                        soft = jnp.where(mask, soft, 0.0)
                    res = jnp.dot(soft.astype(BF16), tri, preferred_element_type=F32)
                    suffix, total = res[:, :tk], res[:, tk:]
                    a = jnp.concatenate([jnp.exp2(lb + sf + c_run)
                                         for lb, sf in zip(_lane_groups(log_b), _lane_groups(suffix))], axis=1)
                    if masked:
                        a = jnp.where(mask, a, 0.0)
                    weights.append(a.astype(BF16))
                    c_run = c_run + total
                a_all = jnp.concatenate(weights, axis=1)
                new_state[(r, h)] = (c_run, acc_prev + jnp.dot(a_all, v_all, preferred_element_type=F32))
        for key in keys:
            c_ref[key], acc_ref[key] = new_state[key]

    first = i * n_tiles

    @pl.when(i == 0)
    def _():
        blocks = [load_kv(r) for r in range(n_tiles)]
        window([(r, blocks[max(r - 1, 0):r + 1][::-1], True) for r in range(n_tiles)])

    @pl.when(i > 0)
    def _():
        blocks = [load_kv(first - 1 + n) for n in range(n_tiles + 1)]
        window([(r, [blocks[r + 1], blocks[r]], True) for r in range(n_tiles)])

    for r in range(n_tiles):
        def live(r=r):
            return jnp.max(jnp.maximum(c_ref[r, 0], c_ref[r, 1])) > SB_DEAD_LOG2

        def cond(carry):
            j, alive = carry
            return jnp.logical_and(j >= 0, alive)

        def body(carry, r=r, live=live):
            j, _ = carry
            window([(r, [load_kv(j)], False)])
            return j - 1, live()

        lax.while_loop(cond, body, (first + r - 2, live()))
        o_ref[0, r * tk:(r + 1) * tk, :] = jnp.where(lane < HEAD_DIM, acc_ref[r, 0],
                                                     acc_ref[r, 1]).astype(o_ref.dtype)


def _sb_attn(qb, kb, vb, tq=512, tk=256):
    b, s, w = qb.shape
    nh = w // LANES
    r = lax.broadcasted_iota(jnp.int32, (tk, tk), 0)
    c = lax.broadcasted_iota(jnp.int32, (tk, tk), 1)
    tri = -jnp.concatenate([(r > c).astype(BF16), jnp.ones((tk, LANES), BF16)], axis=1)
    kern = functools.partial(_sb_kernel, tq=tq, tk=tk)
    qspec = pl.BlockSpec((1, tq, LANES), lambda bb, h, i: (bb, i, h))
    kvspec = pl.BlockSpec((1, s, LANES), lambda bb, h, i: (bb, 0, h))
    return pl.pallas_call(
        kern,
        out_shape=jax.ShapeDtypeStruct((b, s, w), BF16),
        grid=(b, nh, s // tq),
        in_specs=[qspec, kvspec, kvspec, pl.BlockSpec(tri.shape, lambda bb, h, i: (0, 0))],
        out_specs=qspec,
        scratch_shapes=[pltpu.VMEM((tq // tk, 2, tk, LANES), F32), pltpu.VMEM((tq // tk, 2, tk, LANES), F32)],
        compiler_params=_params(("arbitrary",) * 3),
        name="sb_attn",
    )(qb, kb, vb, tri)


def _post_attn_kernel(oa_ref, ob_ref, ga_ref, gb_ref, x_ref, wda_ref, wsb_ref, wo_ref, g_ref,
                      wr_hi_ref, wr_lo_ref, br_ref, tri_ref,
                      x1_ref, h_ref, idx_ref, gate_ref, rank_ref, cnt_ref, carry_ref):
    i = pl.program_id(0)

    @pl.when(i == 0)
    def _():
        carry_ref[...] = jnp.zeros(carry_ref.shape, F32)

    ya = jnp.dot(oa_ref[...], wda_ref[...], preferred_element_type=F32)
    yb = jnp.dot(ob_ref[...], wsb_ref[...], preferred_element_type=F32)
    mix = ga_ref[...].astype(F32) * ya + gb_ref[...].astype(F32) * yb
    x1 = x_ref[...] + jnp.dot(mix.astype(BF16), wo_ref[...], preferred_element_type=F32)
    x1_ref[...] = x1
    ms = jnp.mean(x1 * x1, axis=-1, keepdims=True)
    h = x1 * lax.rsqrt(ms + NORM_EPS) * g_ref[...]
    h_ref[...] = h

    h_hi = h.astype(BF16)
    h_lo = (h - h_hi.astype(F32)).astype(BF16)
    ne = br_ref.shape[0]
    p_hi = jnp.dot(h_hi, wr_hi_ref[...], preferred_element_type=F32)
    p_lo = jnp.dot(h_lo, wr_lo_ref[...], preferred_element_type=F32)
    by_token = p_hi + pltpu.roll(p_hi, LANES - ne, 1) + p_lo
    logits = by_token.T[:ne] + br_ref[...]
    tm = logits.shape[1]
    eid = lax.broadcasted_iota(jnp.int32, (ne, tm), 0).astype(F32)
    vals, ids = [], []
    work = logits
    for _ in range(TOP_K):
        mx = jnp.max(work, axis=0, keepdims=True)
        sel = jnp.min(jnp.where(work == mx, eid, float(ne)), axis=0, keepdims=True)
        vals.append(mx)
        ids.append(sel)
        work = jnp.where(eid == sel, -jnp.inf, work)
    exps = [jnp.exp(v - vals[0]) for v in vals]
    denom = exps[0] + exps[1] + exps[2] + exps[3]
    onehots = [(eid == sel).astype(F32) for sel in ids]
    assigned = onehots[0] + onehots[1] + onehots[2] + onehots[3]
    before = jnp.dot(assigned.astype(BF16), tri_ref[...], preferred_element_type=F32) + carry_ref[...]
    for r in range(TOP_K):
        idx_ref[r:r + 1, :] = ids[r].astype(jnp.int32)
        gate_ref[r:r + 1, :] = exps[r] / denom
        rank_ref[r:r + 1, :] = jnp.sum(onehots[r] * before, axis=0, keepdims=True).astype(jnp.int32)
    carry = carry_ref[...] + jnp.sum(assigned, axis=1, keepdims=True)
    carry_ref[...] = carry
    cnt_ref[...] = jnp.broadcast_to(carry, cnt_ref.shape)


def _post_attn(oa, ob, ga, gb, x2, wda, wsb, wo, g, wr_hi, wr_lo, br, tm=512):
    t, d = x2.shape
    w = oa.shape[1]
    r = lax.broadcasted_iota(jnp.int32, (tm, tm), 0)
    c = lax.broadcasted_iota(jnp.int32, (tm, tm), 1)
    tri = (r < c).astype(BF16)
    row = lambda i: (i, 0)
    col = lambda i: (0, i)
    const = lambda i: (0, 0)
    full = lambda a: pl.BlockSpec(a.shape, const)
    return pl.pallas_call(
        _post_attn_kernel,
        out_shape=[jax.ShapeDtypeStruct((t, d), F32), jax.ShapeDtypeStruct((t, d), F32),
                   jax.ShapeDtypeStruct((TOP_K, t), jnp.int32), jax.ShapeDtypeStruct((TOP_K, t), F32),
                   jax.ShapeDtypeStruct((TOP_K, t), jnp.int32),
                   jax.ShapeDtypeStruct((N_EXPERTS, LANES), F32)],
        grid=(t // tm,),
        in_specs=[pl.BlockSpec((tm, w), row), pl.BlockSpec((tm, w), row),
                  pl.BlockSpec((tm, d), row), pl.BlockSpec((tm, d), row), pl.BlockSpec((tm, d), row),
                  full(wda), full(wsb), full(wo), full(g), full(wr_hi), full(wr_lo), full(br), full(tri)],
        out_specs=[pl.BlockSpec((tm, d), row), pl.BlockSpec((tm, d), row),
                   pl.BlockSpec((TOP_K, tm), col), pl.BlockSpec((TOP_K, tm), col),
                   pl.BlockSpec((TOP_K, tm), col), pl.BlockSpec((N_EXPERTS, LANES), const)],
        scratch_shapes=[pltpu.VMEM((N_EXPERTS, 1), F32)],
        compiler_params=_params(("arbitrary",)),
        name="post_attn",
    )(oa, ob, ga, gb, x2, wda, wsb, wo, g, wr_hi, wr_lo, br, tri)


def _dispatch_kernel(dest_ref, zero_blk_ref, n_used_ref, h_ref, xs_ref, zeros, sem, zsem, *, tm, t_total):
    i = pl.program_id(0)
    n_blocks = xs_ref.shape[0] // ROW_BLOCK

    def zero_copy(blk):
        row = pl.multiple_of(blk * ROW_BLOCK, ROW_BLOCK)
        return pltpu.make_async_copy(zeros, xs_ref.at[pl.ds(row, ROW_BLOCK), :], zsem)

    @pl.when(i == 0)
    def _():
        zeros[...] = jnp.zeros(zeros.shape, zeros.dtype)
        n_tail = n_blocks - n_used_ref[0]

        def start(n, carry):
            zero_copy(jnp.where(n < N_EXPERTS, zero_blk_ref[jnp.minimum(n, N_EXPERTS - 1)],
                                n_used_ref[0] + n - N_EXPERTS)).start()
            return carry

        def wait(n, carry):
            zero_copy(0).wait()
            return carry

        lax.fori_loop(0, N_EXPERTS + n_tail, start, 0)
        lax.fori_loop(0, N_EXPERTS + n_tail, wait, 0)

    def row_copy(t, k):
        dst = dest_ref[k * t_total + i * tm + t]
        return pltpu.make_async_copy(h_ref.at[pl.ds(t, 1), :], xs_ref.at[pl.ds(dst, 1), :], sem)

    def issue(t, carry):
        for k in range(TOP_K):
            row_copy(t, k).start(priority=k % 2)
        return carry

    lax.fori_loop(0, tm, issue, 0, unroll=8)
    for _ in range(TOP_K):
        pltpu.make_async_copy(h_ref, xs_ref.at[pl.ds(0, tm), :], sem).wait()


def _dispatch(dest_flat, zero_blk, n_used, h, p_rows, tm=256):
    t, d = h.shape
    kern = functools.partial(_dispatch_kernel, tm=tm, t_total=t)
    return pl.pallas_call(
        kern,
        out_shape=jax.ShapeDtypeStruct((p_rows, d), h.dtype),
        grid_spec=pltpu.PrefetchScalarGridSpec(
            num_scalar_prefetch=3,
            grid=(t // tm,),
            in_specs=[pl.BlockSpec((tm, d), lambda i, *_: (i, 0))],
            out_specs=pl.BlockSpec(memory_space=pl.ANY),
            scratch_shapes=[pltpu.VMEM((ROW_BLOCK, d), h.dtype), pltpu.SemaphoreType.DMA,
                            pltpu.SemaphoreType.DMA],
        ),
        compiler_params=_params(("arbitrary",)),
        name="dispatch",
    )(dest_flat, zero_blk, n_used, h)


def _expert_kernel(first_ref, count_ref, n_used_ref, xs_ref, wg_ref, bg_ref, wu_ref, bu_ref, wd_ref, bd_ref,
                   o_ref, wg_bf, wu_bf, wd_bf, xbuf, obuf, in_sem, out_sem, pending_ref):
    e = pl.program_id(0)
    first = first_ref[e]
    count = count_ref[e]

    n_pairs = count // 2
    odd = count % 2

    def rows(blk, nb):
        return pl.ds(pl.multiple_of(blk * ROW_BLOCK, ROW_BLOCK), nb * ROW_BLOCK)

    def fetch(blk, nb, slot):
        return pltpu.make_async_copy(xs_ref.at[rows(blk, nb), :], xbuf.at[slot, pl.ds(0, nb * ROW_BLOCK), :],
                                     in_sem.at[slot])

    def flush(blk, nb, slot):
        return pltpu.make_async_copy(obuf.at[slot, pl.ds(0, nb * ROW_BLOCK), :], o_ref.at[rows(blk, nb), :],
                                     out_sem.at[slot])

    def mlp(x):
        x = x.astype(BF16)
        g = jnp.dot(x, wg_bf[...], preferred_element_type=F32) + bg_ref[0]
        u = jnp.dot(x, wu_bf[...], preferred_element_type=F32) + bu_ref[0]
        g = jnp.minimum(g, SWIGLU_LIMIT)
        u = jnp.clip(u, -SWIGLU_LIMIT, SWIGLU_LIMIT)
        glu = g * jax.nn.sigmoid(SWIGLU_ALPHA * g)
        act = ((u + 1.0) * glu).astype(BF16)
        return (jnp.dot(act, wd_bf[...], preferred_element_type=F32) + bd_ref[0]).astype(obuf.dtype)

    def start_item(first_x, count_x, k):
        @pl.when(2 * k + 1 < count_x)
        def _():
            fetch(first_x + 2 * k, 2, k % 2).start()

        @pl.when(2 * k + 1 == count_x)
        def _():
            fetch(first_x + 2 * k, 1, k % 2).start()

    @pl.when(e == 0)
    def _():
        start_item(first, count, 0)
        start_item(first, count, 1)

    @pl.when(count > 0)
    def _():
        wg_bf[...] = wg_ref[0].astype(BF16)
        wu_bf[...] = wu_ref[0].astype(BF16)
        wd_bf[...] = wd_ref[0].astype(BF16)

    @pl.when(e == 0)
    def _():
        pending_ref[0] = 0
        pending_ref[1] = 0

    def drain(slot):
        for nb in (1, 2):
            @pl.when(pending_ref[slot] == nb)
            def _():
                flush(0, nb, slot).wait()
        pending_ref[slot] = 0

    def write_back(blk, nb, slot):
        flush(blk, nb, slot).start()
        pending_ref[slot] = nb

    def pair(jj, carry):
        slot = jj % 2
        blk = first + 2 * jj
        fetch(blk, 2, slot).wait()
        y = mlp(xbuf[slot])
        start_item(first, count, jj + 2)
        drain(slot)
        obuf[slot] = y
        write_back(blk, 2, slot)
        return carry

    lax.fori_loop(0, n_pairs, pair, 0)
    last = first + 2 * n_pairs
    last_slot = n_pairs % 2

    @pl.when(odd == 1)
    def _():
        fetch(last, 1, last_slot).wait()
        y = mlp(xbuf[last_slot, :ROW_BLOCK, :])
        drain(last_slot)
        obuf[last_slot, :ROW_BLOCK, :] = y
        write_back(last, 1, last_slot)

    @pl.when(e + 1 < pl.num_programs(0))
    def _():
        start_item(first_ref[e + 1], count_ref[e + 1], 0)
        start_item(first_ref[e + 1], count_ref[e + 1], 1)

    @pl.when(e == pl.num_programs(0) - 1)
    def _():
        drain(0)
        drain(1)
        obuf[0, :ROW_BLOCK, :] = jnp.zeros((ROW_BLOCK, obuf.shape[2]), obuf.dtype)
        n_used = n_used_ref[0]
        n_total = o_ref.shape[0] // ROW_BLOCK

        def start(blk, carry):
            flush(blk, 1, 0).start()
            return carry

        def wait(blk, carry):
            flush(blk, 1, 0).wait()
            return carry

        lax.fori_loop(n_used, n_total, start, 0)
        lax.fori_loop(n_used, n_total, wait, 0)


def _experts(first_blk, blk_count, n_used, n_out_blocks, xs, wg, bg, wu, bu, wd, bd):
    d = xs.shape[1]
    n_exp, _, f = wg.shape
    wmap = lambda e, *_: (e, 0, 0)
    return pl.pallas_call(
        _expert_kernel,
        out_shape=jax.ShapeDtypeStruct((n_out_blocks * ROW_BLOCK, d), BF16),
        grid_spec=pltpu.PrefetchScalarGridSpec(
            num_scalar_prefetch=3,
            grid=(n_exp,),
            in_specs=[pl.BlockSpec(memory_space=pl.ANY),
                      pl.BlockSpec((1, d, f), wmap), pl.BlockSpec((1, 1, f), wmap),
                      pl.BlockSpec((1, d, f), wmap), pl.BlockSpec((1, 1, f), wmap),
                      pl.BlockSpec((1, f, d), wmap), pl.BlockSpec((1, 1, d), wmap)],
            out_specs=pl.BlockSpec(memory_space=pl.ANY),
            scratch_shapes=[pltpu.VMEM((d, f), BF16), pltpu.VMEM((d, f), BF16), pltpu.VMEM((f, d), BF16),
                            pltpu.VMEM((2, 2 * ROW_BLOCK, d), F32), pltpu.VMEM((2, 2 * ROW_BLOCK, d), BF16),
                            pltpu.SemaphoreType.DMA((2,)), pltpu.SemaphoreType.DMA((2,)),
                            pltpu.SMEM((2,), jnp.int32)],
        ),
        compiler_params=_params(("arbitrary",)),
        name="experts",
    )(first_blk, blk_count, n_used, xs, wg, bg, wu, bu, wd, bd)


def _combine_kernel(base_ref, npass_ref, rows_ref, idx_ref, lr_ref, gate_ref, x1_ref, g_ref, o_ref,
                    wbuf, sem, *, tm):
    i = pl.program_id(0)
    slot = i % 2
    win = COMBINE_WINDOW
    width = N_EXPERTS * win

    def start_fetch(tile, p, s):
        last_start = rows_ref.shape[0] - win
        for e in range(N_EXPERTS):
            start = jnp.minimum(base_ref[tile * N_EXPERTS + e] + p * win, last_start)
            start = pl.multiple_of(start, WINDOW_ALIGN)
            pltpu.make_async_copy(rows_ref.at[pl.ds(start, win), :],
                                  wbuf.at[s, pl.ds(e * win, win), :], sem.at[s]).start()

    def wait_fetch(s):
        pltpu.make_async_copy(rows_ref.at[pl.ds(0, width), :], wbuf.at[s], sem.at[s]).wait()

    @pl.when(i == 0)
    def _():
        start_fetch(0, 0, 0)

    @pl.when(i + 1 < pl.num_programs(0))
    def _():
        start_fetch(i + 1, 0, 1 - slot)

    idx = idx_ref[...]
    lr = lr_ref[...]
    gates = gate_ref[...]
    col = lax.broadcasted_iota(jnp.int32, (tm, width), 1)

    def weights(p):
        w = jnp.zeros((tm, width), F32)
        for k in range(TOP_K):
            r = lr[:, k:k + 1] - p * win
            tgt = jnp.where((r >= 0) & (r < win), idx[:, k:k + 1] * win + r, -1)
            w = jnp.where(col == tgt, gates[:, k:k + 1], w)
        return w.astype(BF16)

    def gathered(p):
        return jnp.dot(weights(p), wbuf[slot], preferred_element_type=F32)

    wait_fetch(slot)
    y = x1_ref[...] + gathered(0)

    def extra_pass(p, acc):
        start_fetch(i, p, slot)
        wait_fetch(slot)
        return acc + gathered(p)

    y = lax.fori_loop(1, npass_ref[i], extra_pass, y)
    ms = jnp.mean(y * y, axis=-1, keepdims=True)
    o_ref[...] = y * lax.rsqrt(ms + NORM_EPS) * g_ref[...]


def _combine(base_tbl, npass, rows, idx_t, lr_t, gates_t, x1, g, tm):
    t, d = x1.shape
    kern = functools.partial(_combine_kernel, tm=tm)
    tok = lambda i, *_: (i, 0)
    return pl.pallas_call(
        kern,
        out_shape=jax.ShapeDtypeStruct((t, d), F32),
        grid_spec=pltpu.PrefetchScalarGridSpec(
            num_scalar_prefetch=2,
            grid=(t // tm,),
            in_specs=[pl.BlockSpec(memory_space=pl.ANY),
                      pl.BlockSpec((tm, TOP_K), tok), pl.BlockSpec((tm, TOP_K), tok),
                      pl.BlockSpec((tm, TOP_K), tok), pl.BlockSpec((tm, d), tok),
                      pl.BlockSpec((1, d), lambda i, *_: (0, 0))],
            out_specs=pl.BlockSpec((tm, d), tok),
            scratch_shapes=[pltpu.VMEM((2, N_EXPERTS * COMBINE_WINDOW, d), rows.dtype),
                            pltpu.SemaphoreType.DMA((2,))],
        ),
        compiler_params=_params(("arbitrary",)),
        name="combine",
    )(base_tbl, npass, rows, idx_t, lr_t, gates_t, x1, g)


def _rope_tables(seq):
    inv = 1.0 / (ROPE_THETA ** (jnp.arange(0, HEAD_DIM, 2, dtype=F32) / HEAD_DIM))
    ang = jnp.arange(seq, dtype=F32)[:, None] * inv[None, :]
    cos, sin = jnp.cos(ang), jnp.sin(ang)
    return jnp.concatenate([cos] * 4, axis=1), jnp.concatenate([-sin, sin, -sin, sin], axis=1)


def kernel(x, norm_mix_g, w_in, lambda_q1, lambda_k1, lambda_q2, lambda_k2, da_subln_g, w_da_out, w_sb_out, w_o, norm_ffn_g, w_router, b_router, w_gate, b_gate, w_up, b_up, w_down, b_down, norm_final_g):
    b, s, d = x.shape
    depth = w_in.shape[0]
    t = b * s
    cos_t, sin_t = _rope_tables(s)
    n_blocks = (t * TOP_K + ROW_BLOCK - 1) // ROW_BLOCK + N_EXPERTS
    p_rows = n_blocks * ROW_BLOCK
    x2 = x.reshape(t, d)
    for l in range(depth):
        lambda_init = 0.8 - 0.6 * math.exp(-0.3 * l)
        qa, ka, va, qb, kb, vb, ga, gb = _in_proj(
            x2, norm_mix_g[l][None, :], w_in[l].astype(BF16), cos_t, sin_t, s)
        lam_p = jnp.stack([lambda_q1[l], lambda_k1[l], lambda_q2[l], lambda_k2[l]]).astype(F32)
        seq3 = lambda a: a.reshape(b, s, a.shape[1])
        oa = _diff_attn(seq3(qa), seq3(ka), seq3(va), lam_p, da_subln_g[l][None, :].astype(F32), lambda_init)
        ob = _sb_attn(seq3(qb), seq3(kb), seq3(vb))
        wr = w_router[l].astype(F32)
        wr_top = wr.astype(BF16)
        wr_rest = (wr - wr_top.astype(F32)).astype(BF16)
        lane_pad = lambda a: jnp.pad(a, ((0, 0), (0, LANES - a.shape[1])))
        wr_hi = lane_pad(jnp.concatenate([wr_top, wr_rest], axis=1))
        wr_lo = lane_pad(wr_top)
        x1, h, idx, gates, rank, cnt = _post_attn(
            oa.reshape(t, -1), ob.reshape(t, -1), ga, gb, x2,
            w_da_out[l].astype(BF16), w_sb_out[l].astype(BF16), w_o[l].astype(BF16),
            norm_ffn_g[l][None, :], wr_hi, wr_lo, b_router[l][:, None].astype(F32))
        counts = cnt[:, 0].astype(jnp.int32)
        padded = (counts + ROW_BLOCK - 1) // ROW_BLOCK * ROW_BLOCK
        pad_ends = jnp.cumsum(padded)
        pad_starts = pad_ends - padded
        experts = jnp.arange(N_EXPERTS, dtype=jnp.int32)
        chosen = idx[:, :, None] == experts
        base = jnp.sum(jnp.where(chosen, pad_starts, 0), axis=-1)
        dest = (base + rank).reshape(-1)
        n_used = (pad_ends[-1] // ROW_BLOCK).astype(jnp.int32)
        first_blk = (pad_starts // ROW_BLOCK).astype(jnp.int32)
        blk_count = (padded // ROW_BLOCK).astype(jnp.int32)
        zero_blk = jnp.maximum(pad_ends // ROW_BLOCK - 1, 0).astype(jnp.int32)
        n_tiles = t // COMBINE_TILE
        tile_cnt = jnp.sum(chosen.reshape(TOP_K, n_tiles, COMBINE_TILE, N_EXPERTS), axis=(0, 2), dtype=jnp.int32)
        tile_carry = jnp.cumsum(tile_cnt, axis=0) - tile_cnt
        run_start = pad_starts[None, :] + tile_carry
        run_skew = run_start % WINDOW_ALIGN
        run_base = (run_start - run_skew).reshape(-1).astype(jnp.int32)
        shift_tok = jnp.repeat(tile_carry - run_skew, COMBINE_TILE, axis=0)
        local_rank = rank - jnp.sum(jnp.where(chosen, shift_tok[None], 0), axis=-1)
        n_pass = jnp.maximum((jnp.max(tile_cnt + run_skew, axis=1) + COMBINE_WINDOW - 1) // COMBINE_WINDOW,
                             1).astype(jnp.int32)
        xs = _dispatch(dest, zero_blk, n_used[None], h, p_rows)
        rows = _experts(first_blk, blk_count, n_used[None], n_blocks + 1, xs, w_gate[l], b_gate[l][:, None, :],
                        w_up[l], b_up[l][:, None, :], w_down[l], b_down[l][:, None, :])
        g_next = norm_final_g[None, :] if l == depth - 1 else jnp.ones((1, d), F32)
        x2 = _combine(run_base, n_pass, rows, idx.T, local_rank.T.astype(jnp.int32), gates.T, x1, g_next,
                      COMBINE_TILE)
        if l != depth - 1:
            raise NotImplementedError("only the final layer's norm is fused into the combine kernel")
    return x2.reshape(b, s, d)
```

```python
import functools
import math

import jax
import jax.numpy as jnp
from jax import lax
from jax.experimental import pallas as pl
from jax.experimental.pallas import tpu as pltpu

F32 = jnp.float32
BF16 = jnp.bfloat16

DA_HEADS = 4
HEAD_DIM = 64
N_EXPERTS = 32
TOP_K = 4
ROPE_THETA = 10000.0
SWIGLU_LIMIT = 7.0
SWIGLU_ALPHA = 1.702
NORM_EPS = 1e-5
ROW_BLOCK = 256
COMBINE_TILE = 256
WINDOW_ALIGN = 16
COMBINE_WINDOW = 80
LANES = 128
NEG_BIG = -1e30
LOG2E = math.log2(math.e)
SB_DEAD_LOG2 = -150.0

VMEM_LIMIT = 56 * 1024 * 1024


def _params(sem, vmem=VMEM_LIMIT):
    return pltpu.CompilerParams(dimension_semantics=sem, vmem_limit_bytes=vmem)


def _in_proj_kernel(x_ref, g_ref, w_ref, cos_ref, sin_ref,
                    qa_ref, ka_ref, va_ref, qb_ref, kb_ref, vb_ref, ga_ref, gb_ref):
    x = x_ref[...]
    ms = jnp.mean(x * x, axis=-1, keepdims=True)
    h = (x * lax.rsqrt(ms + NORM_EPS) * g_ref[...]).astype(BF16)
    cos = cos_ref[...]
    sin = sin_ref[...]
    lane = lax.broadcasted_iota(jnp.int32, cos.shape, 1)
    first_half = (lane & (HEAD_DIM - 1)) < HEAD_DIM // 2

    def proj(c0, width):
        return jnp.dot(h, w_ref[:, c0:c0 + width], preferred_element_type=F32)

    def rope(r):
        outs = []
        for g in range(r.shape[1] // LANES):
            xg = r[:, g * LANES:(g + 1) * LANES]
            rot = jnp.where(first_half, pltpu.roll(xg, LANES - HEAD_DIM // 2, 1),
                            pltpu.roll(xg, HEAD_DIM // 2, 1))
            outs.append(xg * cos + rot * sin)
        return jnp.concatenate(outs, axis=1)

    scale = HEAD_DIM ** -0.5 * LOG2E
    w = qa_ref.shape[1]
    d = ga_ref.shape[1]
    qa_ref[...] = (rope(proj(0, w)) * scale).astype(BF16)
    ka_ref[...] = rope(proj(w, w)).astype(BF16)
    va_ref[...] = proj(2 * w, w).astype(BF16)
    qb_ref[...] = (proj(3 * w, w) * scale).astype(BF16)
    kb_ref[...] = proj(4 * w, w).astype(BF16)
    vb_ref[...] = proj(5 * w, w).astype(BF16)
    ga_ref[...] = jax.nn.sigmoid(proj(6 * w, d)).astype(BF16)
    gb_ref[...] = jax.nn.sigmoid(proj(6 * w + d, d)).astype(BF16)


def _in_proj(x2, g, w_in_bf, cos_t, sin_t, seq, tm=512):
    t, d = x2.shape
    w = 512
    nseq = seq // tm
    outs = [jax.ShapeDtypeStruct((t, w), BF16)] * 6 + [jax.ShapeDtypeStruct((t, d), BF16)] * 2
    row = lambda i: (i, 0)
    return pl.pallas_call(
        _in_proj_kernel,
        out_shape=outs,
        grid=(t // tm,),
        in_specs=[
            pl.BlockSpec((tm, d), row),
            pl.BlockSpec((1, d), lambda i: (0, 0)),
            pl.BlockSpec(w_in_bf.shape, lambda i: (0, 0)),
            pl.BlockSpec((tm, LANES), lambda i: (i % nseq, 0)),
            pl.BlockSpec((tm, LANES), lambda i: (i % nseq, 0)),
        ],
        out_specs=[pl.BlockSpec((tm, w), row)] * 6 + [pl.BlockSpec((tm, d), row)] * 2,
        compiler_params=_params(("arbitrary",)),
        name="in_proj",
    )(x2, g, w_in_bf, cos_t, sin_t)


def _lane_groups(x):
    return [x[:, g * LANES:(g + 1) * LANES] for g in range(x.shape[1] // LANES)]


def _da_kernel(q_ref, k_ref, v_ref, lam_ref, subg_ref, o_ref, s_buf, mx_ref, m_ref, acc_ref,
               *, tq, tk, lambda_init):
    i = pl.program_id(2)
    q = q_ref[0]
    lane = lax.broadcasted_iota(jnp.int32, q.shape, 1)
    zero = jnp.zeros_like(q)
    qs = (jnp.where(lane < HEAD_DIM, q, zero), jnp.where(lane >= HEAD_DIM, q, zero))
    mx_ref[...] = jnp.full(mx_ref.shape, NEG_BIG, F32)
    acc_ref[...] = jnp.zeros(acc_ref.shape, F32)

    def score_blocks(js, diag_last):
        maxes = [mx_ref[0], mx_ref[1]]
        for n, j in enumerate(js):
            k = k_ref[0, pl.ds(pl.multiple_of(j * tk, tk), tk), :]
            for h in range(2):
                s = lax.dot_general(qs[h], k, (((1,), (1,)), ((), ())), preferred_element_type=F32)
                if diag_last and n == len(js) - 1:
                    rows = lax.broadcasted_iota(jnp.int32, s.shape, 0)
                    cols = lax.broadcasted_iota(jnp.int32, s.shape, 1)
                    s = jnp.where(rows >= cols, s, NEG_BIG)
                s_buf[h, j] = s
                for sg in _lane_groups(s):
                    maxes[h] = jnp.maximum(maxes[h], sg)
        if diag_last:
            for h in range(2):
                m_ref[h] = jnp.broadcast_to(jnp.max(maxes[h], axis=1, keepdims=True), (tq, LANES))
        else:
            mx_ref[0], mx_ref[1] = maxes

    def score_body(jj, carry):
        score_blocks([2 * jj, 2 * jj + 1], False)
        return carry

    lax.fori_loop(0, i // 2, score_body, 0)

    @pl.when(i % 2 == 0)
    def _():
        score_blocks([i], True)

    @pl.when(i % 2 == 1)
    def _():
        score_blocks([i - 1, i], True)

    def pv_blocks(js):
        vs = [v_ref[0, pl.ds(pl.multiple_of(j * tk, tk), tk), :] for j in js]
        v_all = jnp.concatenate(vs, axis=0)
        v_ext = jnp.concatenate([v_all, jnp.ones_like(v_all)], axis=1)
        accs = [acc_ref[0], acc_ref[1]]
        for h in range(2):
            m = m_ref[h]
            p = jnp.concatenate([jnp.exp2(sg - m).astype(BF16)
                                 for j in js for sg in _lane_groups(s_buf[h, j])], axis=1)
            accs[h] = accs[h] + jnp.dot(p, v_ext, preferred_element_type=F32)
        acc_ref[0], acc_ref[1] = accs

    def pv_body(jj, carry):
        pv_blocks([2 * jj, 2 * jj + 1])
        return carry

    lax.fori_loop(0, (i + 1) // 2, pv_body, 0)

    @pl.when(i % 2 == 0)
    def _():
        pv_blocks([i])

    lam_p = lam_ref[...]
    lam = (jnp.exp(jnp.sum(lam_p[0:1] * lam_p[1:2], axis=1, keepdims=True))
           - jnp.exp(jnp.sum(lam_p[2:3] * lam_p[3:4], axis=1, keepdims=True)) + lambda_init)
    a0, a1 = acc_ref[0], acc_ref[1]
    o = a0[:, :LANES] / a0[:, LANES:] - lam * (a1[:, :LANES] / a1[:, LANES:])
    ms = jnp.mean(o * o, axis=-1, keepdims=True)
    y = o * lax.rsqrt(ms + NORM_EPS) * subg_ref[...] * (1.0 - lambda_init)
    o_ref[0] = y.astype(o_ref.dtype)


def _diff_attn(qa, ka, va, lam_p, subg, lambda_init, tq=512):
    b, s, w = qa.shape
    nh = w // LANES
    kern = functools.partial(_da_kernel, tq=tq, tk=tq, lambda_init=lambda_init)
    qspec = pl.BlockSpec((1, tq, LANES), lambda bb, h, i: (bb, i, h))
    kvspec = pl.BlockSpec((1, s, LANES), lambda bb, h, i: (bb, 0, h))
    return pl.pallas_call(
        kern,
        out_shape=jax.ShapeDtypeStruct((b, s, w), BF16),
        grid=(b, nh, s // tq),
        in_specs=[qspec, kvspec, kvspec,
                  pl.BlockSpec(lam_p.shape, lambda bb, h, i: (0, 0)),
                  pl.BlockSpec(subg.shape, lambda bb, h, i: (0, 0))],
        out_specs=qspec,
        scratch_shapes=[pltpu.VMEM((2, s // tq, tq, tq), F32),
                        pltpu.VMEM((2, tq, LANES), F32),
                        pltpu.VMEM((2, tq, LANES), F32),
                        pltpu.VMEM((2, tq, 2 * LANES), F32)],
        compiler_params=_params(("arbitrary",) * 3),
        name="diff_attn",
    )(qa, ka, va, lam_p, subg)


def _sb_kernel(q_ref, k_ref, v_ref, tri_ref, o_ref, c_ref, acc_ref, *, tq, tk):
    i = pl.program_id(2)
    n_tiles = tq // tk
    tri = tri_ref[...]
    lane = lax.broadcasted_iota(jnp.int32, (tk, LANES), 1)
    qs = []
    for r in range(n_tiles):
        q = q_ref[0, r * tk:(r + 1) * tk, :]
        zero = jnp.zeros_like(q)
        qs.append((jnp.where(lane < HEAD_DIM, q, zero), jnp.where(lane >= HEAD_DIM, q, zero)))

    def load_kv(j):
        rows = pl.ds(pl.multiple_of(j * tk, tk), tk)
        return k_ref[0, rows, :], v_ref[0, rows, :]

    def window(jobs):
        keys = [(r, h) for r, _, _ in jobs for h in range(2)]
        fresh = jnp.zeros((tk, LANES), F32)
        state = {(r, h): (fresh, fresh) if diag_first else (c_ref[r, h], acc_ref[r, h])
                 for r, _, diag_first in jobs for h in range(2)}
        new_state = {}
        for r, kvs, diag_first in jobs:
            v_all = jnp.concatenate([v for _, v in kvs], axis=0)
            for h in range(2):
                c_run, acc_prev = state[(r, h)]
                weights = []
                for n, (k, _) in enumerate(kvs):
                    masked = diag_first and n == 0
                    z = lax.dot_general(qs[r][h], k, (((1,), (1,)), ((), ())), preferred_element_type=F32)
                    soft = jnp.maximum(z, 0.0) + jnp.log2(1.0 + jnp.exp2(-jnp.abs(z)))
                    log_b = z - soft
                    if masked:
                        mask = (lax.broadcasted_iota(jnp.int32, z.shape, 1)
                                < lax.broadcasted_iota(jnp.int32, z.shape, 0))
                        soft = jnp.where(mask, soft, 0.0)
                    res = jnp.dot(soft.astype(BF16), tri, preferred_element_type=F32)
                    suffix, total = res[:, :tk], res[:, tk:]
                    a = jnp.concatenate([jnp.exp2(lb + sf + c_run)
                                         for lb, sf in zip(_lane_groups(log_b), _lane_groups(suffix))], axis=1)
                    if masked:
                        a = jnp.where(mask, a, 0.0)
                    weights.append(a.astype(BF16))
                    c_run = c_run + total
                a_all = jnp.concatenate(weights, axis=1)
                new_state[(r, h)] = (c_run, acc_prev + jnp.dot(a_all, v_all, preferred_element_type=F32))
        for key in keys:
            c_ref[key], acc_ref[key] = new_state[key]

    first = i * n_tiles

    @pl.when(i == 0)
    def _():
        blocks = [load_kv(r) for r in range(n_tiles)]
        window([(r, blocks[max(r - 1, 0):r + 1][::-1], True) for r in range(n_tiles)])

    @pl.when(i > 0)
    def _():
        blocks = [load_kv(first - 1 + n) for n in range(n_tiles + 1)]
        window([(r, [blocks[r + 1], blocks[r]], True) for r in range(n_tiles)])

    for r in range(n_tiles):
        def live(r=r):
            return jnp.max(jnp.maximum(c_ref[r, 0], c_ref[r, 1])) > SB_DEAD_LOG2

        def cond(carry):
            j, alive = carry
            return jnp.logical_and(j >= 0, alive)

        def body(carry, r=r, live=live):
            j, _ = carry
            window([(r, [load_kv(j)], False)])
            return j - 1, live()

        lax.while_loop(cond, body, (first + r - 2, live()))
        o_ref[0, r * tk:(r + 1) * tk, :] = jnp.where(lane < HEAD_DIM, acc_ref[r, 0],
                                                     acc_ref[r, 1]).astype(o_ref.dtype)


def _sb_attn(qb, kb, vb, tq=512, tk=256):
    b, s, w = qb.shape
    nh = w // LANES
    r = lax.broadcasted_iota(jnp.int32, (tk, tk), 0)
    c = lax.broadcasted_iota(jnp.int32, (tk, tk), 1)
    tri = -jnp.concatenate([(r > c).astype(BF16), jnp.ones((tk, LANES), BF16)], axis=1)
    kern = functools.partial(_sb_kernel, tq=tq, tk=tk)
    qspec = pl.BlockSpec((1, tq, LANES), lambda bb, h, i: (bb, i, h))
    kvspec = pl.BlockSpec((1, s, LANES), lambda bb, h, i: (bb, 0, h))
    return pl.pallas_call(
        kern,
        out_shape=jax.ShapeDtypeStruct((b, s, w), BF16),
        grid=(b, nh, s // tq),
        in_specs=[qspec, kvspec, kvspec, pl.BlockSpec(tri.shape, lambda bb, h, i: (0, 0))],
        out_specs=qspec,
        scratch_shapes=[pltpu.VMEM((tq // tk, 2, tk, LANES), F32), pltpu.VMEM((tq // tk, 2, tk, LANES), F32)],
        compiler_params=_params(("arbitrary",) * 3),
        name="sb_attn",
    )(qb, kb, vb, tri)


def _post_attn_kernel(oa_ref, ob_ref, ga_ref, gb_ref, x_ref, wda_ref, wsb_ref, wo_ref, g_ref,
                      wr_hi_ref, wr_lo_ref, br_ref, tri_ref,
                      x1_ref, h_ref, idx_ref, gate_ref, rank_ref, cnt_ref, carry_ref):
    i = pl.program_id(0)

    @pl.when(i == 0)
    def _():
        carry_ref[...] = jnp.zeros(carry_ref.shape, F32)

    ya = jnp.dot(oa_ref[...], wda_ref[...], preferred_element_type=F32)
    yb = jnp.dot(ob_ref[...], wsb_ref[...], preferred_element_type=F32)
    mix = ga_ref[...].astype(F32) * ya + gb_ref[...].astype(F32) * yb
    x1 = x_ref[...] + jnp.dot(mix.astype(BF16), wo_ref[...], preferred_element_type=F32)
    x1_ref[...] = x1
    ms = jnp.mean(x1 * x1, axis=-1, keepdims=True)
    h = x1 * lax.rsqrt(ms + NORM_EPS) * g_ref[...]
    h_ref[...] = h

    h_hi = h.astype(BF16)
    h_lo = (h - h_hi.astype(F32)).astype(BF16)
    ne = br_ref.shape[0]
    p_hi = jnp.dot(h_hi, wr_hi_ref[...], preferred_element_type=F32)
    p_lo = jnp.dot(h_lo, wr_lo_ref[...], preferred_element_type=F32)
    by_token = p_hi + pltpu.roll(p_hi, LANES - ne, 1) + p_lo
    logits = by_token.T[:ne] + br_ref[...]
    tm = logits.shape[1]
    eid = lax.broadcasted_iota(jnp.int32, (ne, tm), 0).astype(F32)
    vals, ids = [], []
    work = logits
    for _ in range(TOP_K):
        mx = jnp.max(work, axis=0, keepdims=True)
        sel = jnp.min(jnp.where(work == mx, eid, float(ne)), axis=0, keepdims=True)
        vals.append(mx)
        ids.append(sel)
        work = jnp.where(eid == sel, -jnp.inf, work)
    exps = [jnp.exp(v - vals[0]) for v in vals]
    denom = exps[0] + exps[1] + exps[2] + exps[3]
    onehots = [(eid == sel).astype(F32) for sel in ids]
    assigned = onehots[0] + onehots[1] + onehots[2] + onehots[3]
    before = jnp.dot(assigned.astype(BF16), tri_ref[...], preferred_element_type=F32) + carry_ref[...]
    for r in range(TOP_K):
        idx_ref[r:r + 1, :] = ids[r].astype(jnp.int32)
        gate_ref[r:r + 1, :] = exps[r] / denom
        rank_ref[r:r + 1, :] = jnp.sum(onehots[r] * before, axis=0, keepdims=True).astype(jnp.int32)
    carry = carry_ref[...] + jnp.sum(assigned, axis=1, keepdims=True)
    carry_ref[...] = carry
    cnt_ref[...] = jnp.broadcast_to(carry, cnt_ref.shape)


def _post_attn(oa, ob, ga, gb, x2, wda, wsb, wo, g, wr_hi, wr_lo, br, tm=512):
    t, d = x2.shape
    w = oa.shape[1]
    r = lax.broadcasted_iota(jnp.int32, (tm, tm), 0)
    c = lax.broadcasted_iota(jnp.int32, (tm, tm), 1)
    tri = (r < c).astype(BF16)
    row = lambda i: (i, 0)
    col = lambda i: (0, i)
    const = lambda i: (0, 0)
    full = lambda a: pl.BlockSpec(a.shape, const)
    return pl.pallas_call(
        _post_attn_kernel,
        out_shape=[jax.ShapeDtypeStruct((t, d), F32), jax.ShapeDtypeStruct((t, d), F32),
                   jax.ShapeDtypeStruct((TOP_K, t), jnp.int32), jax.ShapeDtypeStruct((TOP_K, t), F32),
                   jax.ShapeDtypeStruct((TOP_K, t), jnp.int32),
                   jax.ShapeDtypeStruct((N_EXPERTS, LANES), F32)],
        grid=(t // tm,),
        in_specs=[pl.BlockSpec((tm, w), row), pl.BlockSpec((tm, w), row),
                  pl.BlockSpec((tm, d), row), pl.BlockSpec((tm, d), row), pl.BlockSpec((tm, d), row),
                  full(wda), full(wsb), full(wo), full(g), full(wr_hi), full(wr_lo), full(br), full(tri)],
        out_specs=[pl.BlockSpec((tm, d), row), pl.BlockSpec((tm, d), row),
                   pl.BlockSpec((TOP_K, tm), col), pl.BlockSpec((TOP_K, tm), col),
                   pl.BlockSpec((TOP_K, tm), col), pl.BlockSpec((N_EXPERTS, LANES), const)],
        scratch_shapes=[pltpu.VMEM((N_EXPERTS, 1), F32)],
        compiler_params=_params(("arbitrary",)),
        name="post_attn",
    )(oa, ob, ga, gb, x2, wda, wsb, wo, g, wr_hi, wr_lo, br, tri)


def _dispatch_kernel(dest_ref, zero_blk_ref, n_used_ref, h_ref, xs_ref, zeros, sem, zsem, *, tm, t_total):
    i = pl.program_id(0)
    n_blocks = xs_ref.shape[0] // ROW_BLOCK

    def zero_copy(blk):
        row = pl.multiple_of(blk * ROW_BLOCK, ROW_BLOCK)
        return pltpu.make_async_copy(zeros, xs_ref.at[pl.ds(row, ROW_BLOCK), :], zsem)

    @pl.when(i == 0)
    def _():
        zeros[...] = jnp.zeros(zeros.shape, zeros.dtype)
        n_tail = n_blocks - n_used_ref[0]

        def start(n, carry):
            zero_copy(jnp.where(n < N_EXPERTS, zero_blk_ref[jnp.minimum(n, N_EXPERTS - 1)],
                                n_used_ref[0] + n - N_EXPERTS)).start()
            return carry

        def wait(n, carry):
            zero_copy(0).wait()
            return carry

        lax.fori_loop(0, N_EXPERTS + n_tail, start, 0)
        lax.fori_loop(0, N_EXPERTS + n_tail, wait, 0)

    def row_copy(t, k):
        dst = dest_ref[k * t_total + i * tm + t]
        return pltpu.make_async_copy(h_ref.at[pl.ds(t, 1), :], xs_ref.at[pl.ds(dst, 1), :], sem)

    def issue(t, carry):
        for k in range(TOP_K):
            row_copy(t, k).start(priority=k % 2)
        return carry

    lax.fori_loop(0, tm, issue, 0, unroll=8)
    for _ in range(TOP_K):
        pltpu.make_async_copy(h_ref, xs_ref.at[pl.ds(0, tm), :], sem).wait()


def _dispatch(dest_flat, zero_blk, n_used, h, p_rows, tm=1024):
    t, d = h.shape
    kern = functools.partial(_dispatch_kernel, tm=tm, t_total=t)
    return pl.pallas_call(
        kern,
        out_shape=jax.ShapeDtypeStruct((p_rows, d), h.dtype),
        grid_spec=pltpu.PrefetchScalarGridSpec(
            num_scalar_prefetch=3,
            grid=(t // tm,),
            in_specs=[pl.BlockSpec((tm, d), lambda i, *_: (i, 0))],
            out_specs=pl.BlockSpec(memory_space=pl.ANY),
            scratch_shapes=[pltpu.VMEM((ROW_BLOCK, d), h.dtype), pltpu.SemaphoreType.DMA,
                            pltpu.SemaphoreType.DMA],
        ),
        compiler_params=_params(("arbitrary",)),
        name="dispatch",
    )(dest_flat, zero_blk, n_used, h)


def _expert_kernel(first_ref, count_ref, n_used_ref, xs_ref, wg_ref, bg_ref, wu_ref, bu_ref, wd_ref, bd_ref,
                   o_ref, wg_bf, wu_bf, wd_bf, xbuf, obuf, in_sem, out_sem, pending_ref):
    e = pl.program_id(0)
    first = first_ref[e]
    count = count_ref[e]

    n_pairs = count // 2
    odd = count % 2

    def rows(blk, nb):
        return pl.ds(pl.multiple_of(blk * ROW_BLOCK, ROW_BLOCK), nb * ROW_BLOCK)

    def fetch(blk, nb, slot):
        return pltpu.make_async_copy(xs_ref.at[rows(blk, nb), :], xbuf.at[slot, pl.ds(0, nb * ROW_BLOCK), :],
                                     in_sem.at[slot])

    def flush(blk, nb, slot):
        return pltpu.make_async_copy(obuf.at[slot, pl.ds(0, nb * ROW_BLOCK), :], o_ref.at[rows(blk, nb), :],
                                     out_sem.at[slot])

    def mlp(x):
        x = x.astype(BF16)
        g = jnp.dot(x, wg_bf[...], preferred_element_type=F32) + bg_ref[0]
        u = jnp.dot(x, wu_bf[...], preferred_element_type=F32) + bu_ref[0]
        g = jnp.minimum(g, SWIGLU_LIMIT)
        u = jnp.clip(u, -SWIGLU_LIMIT, SWIGLU_LIMIT)
        glu = g * jax.nn.sigmoid(SWIGLU_ALPHA * g)
        act = ((u + 1.0) * glu).astype(BF16)
        return (jnp.dot(act, wd_bf[...], preferred_element_type=F32) + bd_ref[0]).astype(obuf.dtype)

    def start_item(first_x, count_x, k):
        @pl.when(2 * k + 1 < count_x)
        def _():
            fetch(first_x + 2 * k, 2, k % 2).start()

        @pl.when(2 * k + 1 == count_x)
        def _():
            fetch(first_x + 2 * k, 1, k % 2).start()

    @pl.when(e == 0)
    def _():
        start_item(first, count, 0)
        start_item(first, count, 1)

    @pl.when(count > 0)
    def _():
        wg_bf[...] = wg_ref[0].astype(BF16)
        wu_bf[...] = wu_ref[0].astype(BF16)
        wd_bf[...] = wd_ref[0].astype(BF16)

    @pl.when(e == 0)
    def _():
        pending_ref[0] = 0
        pending_ref[1] = 0

    def drain(slot):
        for nb in (1, 2):
            @pl.when(pending_ref[slot] == nb)
            def _():
                flush(0, nb, slot).wait()
        pending_ref[slot] = 0

    def write_back(blk, nb, slot):
        flush(blk, nb, slot).start()
        pending_ref[slot] = nb

    def pair(jj, carry):
        slot = jj % 2
        blk = first + 2 * jj
        fetch(blk, 2, slot).wait()
        y = mlp(xbuf[slot])
        start_item(first, count, jj + 2)
        drain(slot)
        obuf[slot] = y
        write_back(blk, 2, slot)
        return carry

    lax.fori_loop(0, n_pairs, pair, 0)
    last = first + 2 * n_pairs
    last_slot = n_pairs % 2

    @pl.when(odd == 1)
    def _():
        fetch(last, 1, last_slot).wait()
        y = mlp(xbuf[last_slot, :ROW_BLOCK, :])
        drain(last_slot)
        obuf[last_slot, :ROW_BLOCK, :] = y
        write_back(last, 1, last_slot)

    @pl.when(e + 1 < pl.num_programs(0))
    def _():
        start_item(first_ref[e + 1], count_ref[e + 1], 0)
        start_item(first_ref[e + 1], count_ref[e + 1], 1)

    @pl.when(e == pl.num_programs(0) - 1)
    def _():
        drain(0)
        drain(1)
        obuf[0, :ROW_BLOCK, :] = jnp.zeros((ROW_BLOCK, obuf.shape[2]), obuf.dtype)
        n_used = n_used_ref[0]
        n_total = o_ref.shape[0] // ROW_BLOCK

        def start(blk, carry):
            flush(blk, 1, 0).start()
            return carry

        def wait(blk, carry):
            flush(blk, 1, 0).wait()
            return carry

        lax.fori_loop(n_used, n_total, start, 0)
        lax.fori_loop(n_used, n_total, wait, 0)


def _experts(first_blk, blk_count, n_used, n_out_blocks, xs, wg, bg, wu, bu, wd, bd):
    d = xs.shape[1]
    n_exp, _, f = wg.shape
    wmap = lambda e, *_: (e, 0, 0)
    return pl.pallas_call(
        _expert_kernel,
        out_shape=jax.ShapeDtypeStruct((n_out_blocks * ROW_BLOCK, d), BF16),
        grid_spec=pltpu.PrefetchScalarGridSpec(
            num_scalar_prefetch=3,
            grid=(n_exp,),
            in_specs=[pl.BlockSpec(memory_space=pl.ANY),
                      pl.BlockSpec((1, d, f), wmap), pl.BlockSpec((1, 1, f), wmap),
                      pl.BlockSpec((1, d, f), wmap), pl.BlockSpec((1, 1, f), wmap),
                      pl.BlockSpec((1, f, d), wmap), pl.BlockSpec((1, 1, d), wmap)],
            out_specs=pl.BlockSpec(memory_space=pl.ANY),
            scratch_shapes=[pltpu.VMEM((d, f), BF16), pltpu.VMEM((d, f), BF16), pltpu.VMEM((f, d), BF16),
                            pltpu.VMEM((2, 2 * ROW_BLOCK, d), F32), pltpu.VMEM((2, 2 * ROW_BLOCK, d), BF16),
                            pltpu.SemaphoreType.DMA((2,)), pltpu.SemaphoreType.DMA((2,)),
                            pltpu.SMEM((2,), jnp.int32)],
        ),
        compiler_params=_params(("arbitrary",)),
        name="experts",
    )(first_blk, blk_count, n_used, xs, wg, bg, wu, bu, wd, bd)


def _combine_kernel(base_ref, npass_ref, rows_ref, idx_ref, lr_ref, gate_ref, x1_ref, g_ref, o_ref,
                    wbuf, sem, *, tm):
    i = pl.program_id(0)
    slot = i % 2
    win = COMBINE_WINDOW
    width = N_EXPERTS * win

    def start_fetch(tile, p, s):
        last_start = rows_ref.shape[0] - win
        for e in range(N_EXPERTS):
            start = jnp.minimum(base_ref[tile * N_EXPERTS + e] + p * win, last_start)
            start = pl.multiple_of(start, WINDOW_ALIGN)
            pltpu.make_async_copy(rows_ref.at[pl.ds(start, win), :],
                                  wbuf.at[s, pl.ds(e * win, win), :], sem.at[s]).start()

    def wait_fetch(s):
        pltpu.make_async_copy(rows_ref.at[pl.ds(0, width), :], wbuf.at[s], sem.at[s]).wait()

    @pl.when(i == 0)
    def _():
        start_fetch(0, 0, 0)

    @pl.when(i + 1 < pl.num_programs(0))
    def _():
        start_fetch(i + 1, 0, 1 - slot)

    idx = idx_ref[...]
    lr = lr_ref[...]
    gates = gate_ref[...]
    col = lax.broadcasted_iota(jnp.int32, (tm, width), 1)

    def weights(p):
        w = jnp.zeros((tm, width), F32)
        for k in range(TOP_K):
            r = lr[:, k:k + 1] - p * win
            tgt = jnp.where((r >= 0) & (r < win), idx[:, k:k + 1] * win + r, -1)
            w = jnp.where(col == tgt, gates[:, k:k + 1], w)
        return w.astype(BF16)

    def gathered(p):
        return jnp.dot(weights(p), wbuf[slot], preferred_element_type=F32)

    wait_fetch(slot)
    y = x1_ref[...] + gathered(0)

    def extra_pass(p, acc):
        start_fetch(i, p, slot)
        wait_fetch(slot)
        return acc + gathered(p)

    y = lax.fori_loop(1, npass_ref[i], extra_pass, y)
    ms = jnp.mean(y * y, axis=-1, keepdims=True)
    o_ref[...] = y * lax.rsqrt(ms + NORM_EPS) * g_ref[...]


def _combine(base_tbl, npass, rows, idx_t, lr_t, gates_t, x1, g, tm):
    t, d = x1.shape
    kern = functools.partial(_combine_kernel, tm=tm)
    tok = lambda i, *_: (i, 0)
    return pl.pallas_call(
        kern,
        out_shape=jax.ShapeDtypeStruct((t, d), F32),
        grid_spec=pltpu.PrefetchScalarGridSpec(
            num_scalar_prefetch=2,
            grid=(t // tm,),
            in_specs=[pl.BlockSpec(memory_space=pl.ANY),
                      pl.BlockSpec((tm, TOP_K), tok), pl.BlockSpec((tm, TOP_K), tok),
                      pl.BlockSpec((tm, TOP_K), tok), pl.BlockSpec((tm, d), tok),
                      pl.BlockSpec((1, d), lambda i, *_: (0, 0))],
            out_specs=pl.BlockSpec((tm, d), tok),
            scratch_shapes=[pltpu.VMEM((2, N_EXPERTS * COMBINE_WINDOW, d), rows.dtype),
                            pltpu.SemaphoreType.DMA((2,))],
        ),
        compiler_params=_params(("arbitrary",)),
        name="combine",
    )(base_tbl, npass, rows, idx_t, lr_t, gates_t, x1, g)


def _rope_tables(seq):
    inv = 1.0 / (ROPE_THETA ** (jnp.arange(0, HEAD_DIM, 2, dtype=F32) / HEAD_DIM))
    ang = jnp.arange(seq, dtype=F32)[:, None] * inv[None, :]
    cos, sin = jnp.cos(ang), jnp.sin(ang)
    return jnp.concatenate([cos] * 4, axis=1), jnp.concatenate([-sin, sin, -sin, sin], axis=1)


def kernel(x, norm_mix_g, w_in, lambda_q1, lambda_k1, lambda_q2, lambda_k2, da_subln_g, w_da_out, w_sb_out, w_o, norm_ffn_g, w_router, b_router, w_gate, b_gate, w_up, b_up, w_down, b_down, norm_final_g):
    b, s, d = x.shape
    depth = w_in.shape[0]
    t = b * s
    cos_t, sin_t = _rope_tables(s)
    n_blocks = (t * TOP_K + ROW_BLOCK - 1) // ROW_BLOCK + N_EXPERTS
    p_rows = n_blocks * ROW_BLOCK
    x2 = x.reshape(t, d)
    for l in range(depth):
        lambda_init = 0.8 - 0.6 * math.exp(-0.3 * l)
        qa, ka, va, qb, kb, vb, ga, gb = _in_proj(
            x2, norm_mix_g[l][None, :], w_in[l].astype(BF16), cos_t, sin_t, s)
        lam_p = jnp.stack([lambda_q1[l], lambda_k1[l], lambda_q2[l], lambda_k2[l]]).astype(F32)
        seq3 = lambda a: a.reshape(b, s, a.shape[1])
        oa = _diff_attn(seq3(qa), seq3(ka), seq3(va), lam_p, da_subln_g[l][None, :].astype(F32), lambda_init)
        ob = _sb_attn(seq3(qb), seq3(kb), seq3(vb))
        wr = w_router[l].astype(F32)
        wr_top = wr.astype(BF16)
        wr_rest = (wr - wr_top.astype(F32)).astype(BF16)
        lane_pad = lambda a: jnp.pad(a, ((0, 0), (0, LANES - a.shape[1])))
        wr_hi = lane_pad(jnp.concatenate([wr_top, wr_rest], axis=1))
        wr_lo = lane_pad(wr_top)
        x1, h, idx, gates, rank, cnt = _post_attn(
            oa.reshape(t, -1), ob.reshape(t, -1), ga, gb, x2,
            w_da_out[l].astype(BF16), w_sb_out[l].astype(BF16), w_o[l].astype(BF16),
            norm_ffn_g[l][None, :], wr_hi, wr_lo, b_router[l][:, None].astype(F32))
        counts = cnt[:, 0].astype(jnp.int32)
        padded = (counts + ROW_BLOCK - 1) // ROW_BLOCK * ROW_BLOCK
        pad_ends = jnp.cumsum(padded)
        pad_starts = pad_ends - padded
        experts = jnp.arange(N_EXPERTS, dtype=jnp.int32)
        chosen = idx[:, :, None] == experts
        base = jnp.sum(jnp.where(chosen, pad_starts, 0), axis=-1)
        dest = (base + rank).reshape(-1)
        n_used = (pad_ends[-1] // ROW_BLOCK).astype(jnp.int32)
        first_blk = (pad_starts // ROW_BLOCK).astype(jnp.int32)
        blk_count = (padded // ROW_BLOCK).astype(jnp.int32)
        zero_blk = jnp.maximum(pad_ends // ROW_BLOCK - 1, 0).astype(jnp.int32)
        n_tiles = t // COMBINE_TILE
        tile_cnt = jnp.sum(chosen.reshape(TOP_K, n_tiles, COMBINE_TILE, N_EXPERTS), axis=(0, 2), dtype=jnp.int32)
        tile_carry = jnp.cumsum(tile_cnt, axis=0) - tile_cnt
        run_start = pad_starts[None, :] + tile_carry
        run_skew = run_start % WINDOW_ALIGN
        run_base = (run_start - run_skew).reshape(-1).astype(jnp.int32)
        shift_tok = jnp.repeat(tile_carry - run_skew, COMBINE_TILE, axis=0)
        local_rank = rank - jnp.sum(jnp.where(chosen, shift_tok[None], 0), axis=-1)
        n_pass = jnp.maximum((jnp.max(tile_cnt + run_skew, axis=1) + COMBINE_WINDOW - 1) // COMBINE_WINDOW,
                             1).astype(jnp.int32)
        xs = _dispatch(dest, zero_blk, n_used[None], h, p_rows)
        rows = _experts(first_blk, blk_count, n_used[None], n_blocks + 1, xs, w_gate[l], b_gate[l][:, None, :],
                        w_up[l], b_up[l][:, None, :], w_down[l], b_down[l][:, None, :])
        g_next = norm_final_g[None, :] if l == depth - 1 else jnp.ones((1, d), F32)
        x2 = _combine(run_base, n_pass, rows, idx.T, local_rank.T.astype(jnp.int32), gates.T, x1, g_next,
                      COMBINE_TILE)
        if l != depth - 1:
            raise NotImplementedError("only the final layer's norm is fused into the combine kernel")
    return x2.reshape(b, s, d)
```

```python
import functools
import math

import jax
import jax.numpy as jnp
from jax import lax
from jax.experimental import pallas as pl
from jax.experimental.pallas import tpu as pltpu

F32 = jnp.float32
BF16 = jnp.bfloat16

DA_HEADS = 4
HEAD_DIM = 64
N_EXPERTS = 32
TOP_K = 4
ROPE_THETA = 10000.0
SWIGLU_LIMIT = 7.0
SWIGLU_ALPHA = 1.702
NORM_EPS = 1e-5
ROW_BLOCK = 256
COMBINE_TILE = 256
WINDOW_ALIGN = 16
COMBINE_WINDOW = 80
LANES = 128
NEG_BIG = -1e30
LOG2E = math.log2(math.e)
SB_DEAD_LOG2 = -150.0

VMEM_LIMIT = 56 * 1024 * 1024


def _params(sem, vmem=VMEM_LIMIT):
    return pltpu.CompilerParams(dimension_semantics=sem, vmem_limit_bytes=vmem)


def _in_proj_kernel(x_ref, g_ref, w_ref, cos_ref, sin_ref,
                    qa_ref, ka_ref, va_ref, qb_ref, kb_ref, vb_ref, ga_ref, gb_ref):
    x = x_ref[...]
    ms = jnp.mean(x * x, axis=-1, keepdims=True)
    h = (x * lax.rsqrt(ms + NORM_EPS) * g_ref[...]).astype(BF16)
    cos = cos_ref[...]
    sin = sin_ref[...]
    lane = lax.broadcasted_iota(jnp.int32, cos.shape, 1)
    first_half = (lane & (HEAD_DIM - 1)) < HEAD_DIM // 2

    def proj(c0, width):
        return jnp.dot(h, w_ref[:, c0:c0 + width], preferred_element_type=F32)

    def rope(r):
        outs = []
        for g in range(r.shape[1] // LANES):
            xg = r[:, g * LANES:(g + 1) * LANES]
            rot = jnp.where(first_half, pltpu.roll(xg, LANES - HEAD_DIM // 2, 1),
                            pltpu.roll(xg, HEAD_DIM // 2, 1))
            outs.append(xg * cos + rot * sin)
        return jnp.concatenate(outs, axis=1)

    scale = HEAD_DIM ** -0.5 * LOG2E
    w = qa_ref.shape[1]
    d = ga_ref.shape[1]
    qa_ref[...] = (rope(proj(0, w)) * scale).astype(BF16)
    ka_ref[...] = rope(proj(w, w)).astype(BF16)
    va_ref[...] = proj(2 * w, w).astype(BF16)
    qb_ref[...] = (proj(3 * w, w) * scale).astype(BF16)
    kb_ref[...] = proj(4 * w, w).astype(BF16)
    vb_ref[...] = proj(5 * w, w).astype(BF16)
    ga_ref[...] = jax.nn.sigmoid(proj(6 * w, d)).astype(BF16)
    gb_ref[...] = jax.nn.sigmoid(proj(6 * w + d, d)).astype(BF16)


def _in_proj(x2, g, w_in_bf, cos_t, sin_t, seq, tm=512):
    t, d = x2.shape
    w = 512
    nseq = seq // tm
    outs = [jax.ShapeDtypeStruct((t, w), BF16)] * 6 + [jax.ShapeDtypeStruct((t, d), BF16)] * 2
    row = lambda i: (i, 0)
    return pl.pallas_call(
        _in_proj_kernel,
        out_shape=outs,
        grid=(t // tm,),
        in_specs=[
            pl.BlockSpec((tm, d), row),
            pl.BlockSpec((1, d), lambda i: (0, 0)),
            pl.BlockSpec(w_in_bf.shape, lambda i: (0, 0)),
            pl.BlockSpec((tm, LANES), lambda i: (i % nseq, 0)),
            pl.BlockSpec((tm, LANES), lambda i: (i % nseq, 0)),
        ],
        out_specs=[pl.BlockSpec((tm, w), row)] * 6 + [pl.BlockSpec((tm, d), row)] * 2,
        compiler_params=_params(("arbitrary",)),
        name="in_proj",
    )(x2, g, w_in_bf, cos_t, sin_t)


def _lane_groups(x):
    return [x[:, g * LANES:(g + 1) * LANES] for g in range(x.shape[1] // LANES)]


def _da_kernel(q_ref, k_ref, v_ref, lam_ref, subg_ref, o_ref, s_buf, mx_ref, m_ref, acc_ref,
               *, tq, tk, lambda_init):
    i = pl.program_id(2)
    q = q_ref[0]
    lane = lax.broadcasted_iota(jnp.int32, q.shape, 1)
    zero = jnp.zeros_like(q)
    qs = (jnp.where(lane < HEAD_DIM, q, zero), jnp.where(lane >= HEAD_DIM, q, zero))
    mx_ref[...] = jnp.full(mx_ref.shape, NEG_BIG, F32)
    acc_ref[...] = jnp.zeros(acc_ref.shape, F32)

    def score_blocks(js, diag_last):
        maxes = [mx_ref[0], mx_ref[1]]
        for n, j in enumerate(js):
            k = k_ref[0, pl.ds(pl.multiple_of(j * tk, tk), tk), :]
            for h in range(2):
                s = lax.dot_general(qs[h], k, (((1,), (1,)), ((), ())), preferred_element_type=F32)
                if diag_last and n == len(js) - 1:
                    rows = lax.broadcasted_iota(jnp.int32, s.shape, 0)
                    cols = lax.broadcasted_iota(jnp.int32, s.shape, 1)
                    s = jnp.where(rows >= cols, s, NEG_BIG)
                s_buf[h, j] = s
                for sg in _lane_groups(s):
                    maxes[h] = jnp.maximum(maxes[h], sg)
        if diag_last:
            for h in range(2):
                m_ref[h] = jnp.broadcast_to(jnp.max(maxes[h], axis=1, keepdims=True), (tq, LANES))
        else:
            mx_ref[0], mx_ref[1] = maxes

    def score_body(jj, carry):
        score_blocks([2 * jj, 2 * jj + 1], False)
        return carry

    lax.fori_loop(0, i // 2, score_body, 0)

    @pl.when(i % 2 == 0)
    def _():
        score_blocks([i], True)

    @pl.when(i % 2 == 1)
    def _():
        score_blocks([i - 1, i], True)

    def pv_blocks(js):
        vs = [v_ref[0, pl.ds(pl.multiple_of(j * tk, tk), tk), :] for j in js]
        v_all = jnp.concatenate(vs, axis=0)
        v_ext = jnp.concatenate([v_all, jnp.ones_like(v_all)], axis=1)
        accs = [acc_ref[0], acc_ref[1]]
        for h in range(2):
            m = m_ref[h]
            p = jnp.concatenate([jnp.exp2(sg - m).astype(BF16)
                                 for j in js for sg in _lane_groups(s_buf[h, j])], axis=1)
            accs[h] = accs[h] + jnp.dot(p, v_ext, preferred_element_type=F32)
        acc_ref[0], acc_ref[1] = accs

    def pv_body(jj, carry):
        pv_blocks([2 * jj, 2 * jj + 1])
        return carry

    lax.fori_loop(0, (i + 1) // 2, pv_body, 0)

    @pl.when(i % 2 == 0)
    def _():
        pv_blocks([i])

    lam_p = lam_ref[...]
    lam = (jnp.exp(jnp.sum(lam_p[0:1] * lam_p[1:2], axis=1, keepdims=True))
           - jnp.exp(jnp.sum(lam_p[2:3] * lam_p[3:4], axis=1, keepdims=True)) + lambda_init)
    a0, a1 = acc_ref[0], acc_ref[1]
    o = a0[:, :LANES] / a0[:, LANES:] - lam * (a1[:, :LANES] / a1[:, LANES:])
    ms = jnp.mean(o * o, axis=-1, keepdims=True)
    y = o * lax.rsqrt(ms + NORM_EPS) * subg_ref[...] * (1.0 - lambda_init)
    o_ref[0] = y.astype(o_ref.dtype)


def _diff_attn(qa, ka, va, lam_p, subg, lambda_init, tq=512):
    b, s, w = qa.shape
    nh = w // LANES
    kern = functools.partial(_da_kernel, tq=tq, tk=tq, lambda_init=lambda_init)
    qspec = pl.BlockSpec((1, tq, LANES), lambda bb, h, i: (bb, i, h))
    kvspec = pl.BlockSpec((1, s, LANES), lambda bb, h, i: (bb, 0, h))
    return pl.pallas_call(
        kern,
        out_shape=jax.ShapeDtypeStruct((b, s, w), BF16),
        grid=(b, nh, s // tq),
        in_specs=[qspec, kvspec, kvspec,
                  pl.BlockSpec(lam_p.shape, lambda bb, h, i: (0, 0)),
                  pl.BlockSpec(subg.shape, lambda bb, h, i: (0, 0))],
        out_specs=qspec,
        scratch_shapes=[pltpu.VMEM((2, s // tq, tq, tq), F32),
                        pltpu.VMEM((2, tq, LANES), F32),
                        pltpu.VMEM((2, tq, LANES), F32),
                        pltpu.VMEM((2, tq, 2 * LANES), F32)],
        compiler_params=_params(("arbitrary",) * 3),
        name="diff_attn",
    )(qa, ka, va, lam_p, subg)


def _sb_kernel(q_ref, k_ref, v_ref, tri_ref, o_ref, c_ref, acc_ref, *, tq, tk):
    i = pl.program_id(2)
    n_tiles = tq // tk
    tri = tri_ref[...]
    lane = lax.broadcasted_iota(jnp.int32, (tk, LANES), 1)
    qs = []
    for r in range(n_tiles):
        q = q_ref[0, r * tk:(r + 1) * tk, :]
        zero = jnp.zeros_like(q)
        qs.append((jnp.where(lane < HEAD_DIM, q, zero), jnp.where(lane >= HEAD_DIM, q, zero)))

    def load_kv(j):
        rows = pl.ds(pl.multiple_of(j * tk, tk), tk)
        return k_ref[0, rows, :], v_ref[0, rows, :]

    def window(jobs):
        keys = [(r, h) for r, _, _ in jobs for h in range(2)]
        fresh = jnp.zeros((tk, LANES), F32)
        state = {(r, h): (fresh, fresh) if diag_first else (c_ref[r, h], acc_ref[r, h])
                 for r, _, diag_first in jobs for h in range(2)}
        new_state = {}
        for r, kvs, diag_first in jobs:
            v_all = jnp.concatenate([v for _, v in kvs], axis=0)
            for h in range(2):
                c_run, acc_prev = state[(r, h)]
                weights = []
                for n, (k, _) in enumerate(kvs):
                    masked = diag_first and n == 0
                    z = lax.dot_general(qs[r][h], k, (((1,), (1,)), ((), ())), preferred_element_type=F32)
                    soft = jnp.maximum(z, 0.0) + jnp.log2(1.0 + jnp.exp2(-jnp.abs(z)))
                    log_b = z - soft
                    if masked:
                        mask = (lax.broadcasted_iota(jnp.int32, z.shape, 1)
                                < lax.broadcasted_iota(jnp.int32, z.shape, 0))
                        soft = jnp.where(mask, soft, 0.0)
                    res = jnp.dot(soft.astype(BF16), tri, preferred_element_type=F32)
                    suffix, total = res[:, :tk], res[:, tk:]
                    a = jnp.concatenate([jnp.exp2(lb + sf + c_run)
                                         for lb, sf in zip(_lane_groups(log_b), _lane_groups(suffix))], axis=1)
                    if masked:
                        a = jnp.where(mask, a, 0.0)
                    weights.append(a.astype(BF16))
                    c_run = c_run + total
                a_all = jnp.concatenate(weights, axis=1)
                new_state[(r, h)] = (c_run, acc_prev + jnp.dot(a_all, v_all, preferred_element_type=F32))
        for key in keys:
            c_ref[key], acc_ref[key] = new_state[key]

    first = i * n_tiles

    @pl.when(i == 0)
    def _():
        blocks = [load_kv(r) for r in range(n_tiles)]
        window([(r, blocks[max(r - 1, 0):r + 1][::-1], True) for r in range(n_tiles)])

    @pl.when(i > 0)
    def _():
        blocks = [load_kv(first - 1 + n) for n in range(n_tiles + 1)]
        window([(r, [blocks[r + 1], blocks[r]], True) for r in range(n_tiles)])

    for r in range(n_tiles):
        def live(r=r):
            return jnp.max(jnp.maximum(c_ref[r, 0], c_ref[r, 1])) > SB_DEAD_LOG2

        def cond(carry):
            j, alive = carry
            return jnp.logical_and(j >= 0, alive)

        def body(carry, r=r, live=live):
            j, _ = carry
            window([(r, [load_kv(j)], False)])
            return j - 1, live()

        lax.while_loop(cond, body, (first + r - 2, live()))
        o_ref[0, r * tk:(r + 1) * tk, :] = jnp.where(lane < HEAD_DIM, acc_ref[r, 0],
                                                     acc_ref[r, 1]).astype(o_ref.dtype)


def _sb_attn(qb, kb, vb, tq=512, tk=256):
    b, s, w = qb.shape
    nh = w // LANES
    r = lax.broadcasted_iota(jnp.int32, (tk, tk), 0)
    c = lax.broadcasted_iota(jnp.int32, (tk, tk), 1)
    tri = -jnp.concatenate([(r > c).astype(BF16), jnp.ones((tk, LANES), BF16)], axis=1)
    kern = functools.partial(_sb_kernel, tq=tq, tk=tk)
    qspec = pl.BlockSpec((1, tq, LANES), lambda bb, h, i: (bb, i, h))
    kvspec = pl.BlockSpec((1, s, LANES), lambda bb, h, i: (bb, 0, h))
    return pl.pallas_call(
        kern,
        out_shape=jax.ShapeDtypeStruct((b, s, w), BF16),
        grid=(b, nh, s // tq),
        in_specs=[qspec, kvspec, kvspec, pl.BlockSpec(tri.shape, lambda bb, h, i: (0, 0))],
        out_specs=qspec,
        scratch_shapes=[pltpu.VMEM((tq // tk, 2, tk, LANES), F32), pltpu.VMEM((tq // tk, 2, tk, LANES), F32)],
        compiler_params=_params(("arbitrary",) * 3),
        name="sb_attn",
    )(qb, kb, vb, tri)


def _post_attn_kernel(oa_ref, ob_ref, ga_ref, gb_ref, x_ref, wda_ref, wsb_ref, wo_ref, g_ref,
                      wr_hi_ref, wr_lo_ref, br_ref, tri_ref,
                      x1_ref, h_ref, idx_ref, gate_ref, rank_ref, cnt_ref, carry_ref):
    i = pl.program_id(0)

    @pl.when(i == 0)
    def _():
        carry_ref[...] = jnp.zeros(carry_ref.shape, F32)

    ya = jnp.dot(oa_ref[...], wda_ref[...], preferred_element_type=F32)
    yb = jnp.dot(ob_ref[...], wsb_ref[...], preferred_element_type=F32)
    mix = ga_ref[...].astype(F32) * ya + gb_ref[...].astype(F32) * yb
    x1 = x_ref[...] + jnp.dot(mix.astype(BF16), wo_ref[...], preferred_element_type=F32)
    x1_ref[...] = x1
    ms = jnp.mean(x1 * x1, axis=-1, keepdims=True)
    h = x1 * lax.rsqrt(ms + NORM_EPS) * g_ref[...]
    h_ref[...] = h

    h_hi = h.astype(BF16)
    h_lo = (h - h_hi.astype(F32)).astype(BF16)
    ne = br_ref.shape[0]
    p_hi = jnp.dot(h_hi, wr_hi_ref[...], preferred_element_type=F32)
    p_lo = jnp.dot(h_lo, wr_lo_ref[...], preferred_element_type=F32)
    by_token = p_hi + pltpu.roll(p_hi, LANES - ne, 1) + p_lo
    logits = by_token.T[:ne] + br_ref[...]
    tm = logits.shape[1]
    eid = lax.broadcasted_iota(jnp.int32, (ne, tm), 0).astype(F32)
    vals, ids = [], []
    work = logits
    for _ in range(TOP_K):
        mx = jnp.max(work, axis=0, keepdims=True)
        sel = jnp.min(jnp.where(work == mx, eid, float(ne)), axis=0, keepdims=True)
        vals.append(mx)
        ids.append(sel)
        work = jnp.where(eid == sel, -jnp.inf, work)
    exps = [jnp.exp(v - vals[0]) for v in vals]
    denom = exps[0] + exps[1] + exps[2] + exps[3]
    onehots = [(eid == sel).astype(F32) for sel in ids]
    assigned = onehots[0] + onehots[1] + onehots[2] + onehots[3]
    before = jnp.dot(assigned.astype(BF16), tri_ref[...], preferred_element_type=F32) + carry_ref[...]
    for r in range(TOP_K):
        idx_ref[r:r + 1, :] = ids[r].astype(jnp.int32)
        gate_ref[r:r + 1, :] = exps[r] / denom
        rank_ref[r:r + 1, :] = jnp.sum(onehots[r] * before, axis=0, keepdims=True).astype(jnp.int32)
    carry = carry_ref[...] + jnp.sum(assigned, axis=1, keepdims=True)
    carry_ref[...] = carry
    cnt_ref[...] = jnp.broadcast_to(carry, cnt_ref.shape)


def _post_attn(oa, ob, ga, gb, x2, wda, wsb, wo, g, wr_hi, wr_lo, br, tm=512):
    t, d = x2.shape
    w = oa.shape[1]
    r = lax.broadcasted_iota(jnp.int32, (tm, tm), 0)
    c = lax.broadcasted_iota(jnp.int32, (tm, tm), 1)
    tri = (r < c).astype(BF16)
    row = lambda i: (i, 0)
    col = lambda i: (0, i)
    const = lambda i: (0, 0)
    full = lambda a: pl.BlockSpec(a.shape, const)
    return pl.pallas_call(
        _post_attn_kernel,
        out_shape=[jax.ShapeDtypeStruct((t, d), F32), jax.ShapeDtypeStruct((t, d), F32),
                   jax.ShapeDtypeStruct((TOP_K, t), jnp.int32), jax.ShapeDtypeStruct((TOP_K, t), F32),
                   jax.ShapeDtypeStruct((TOP_K, t), jnp.int32),
                   jax.ShapeDtypeStruct((N_EXPERTS, LANES), F32)],
        grid=(t // tm,),
        in_specs=[pl.BlockSpec((tm, w), row), pl.BlockSpec((tm, w), row),
                  pl.BlockSpec((tm, d), row), pl.BlockSpec((tm, d), row), pl.BlockSpec((tm, d), row),
                  full(wda), full(wsb), full(wo), full(g), full(wr_hi), full(wr_lo), full(br), full(tri)],
        out_specs=[pl.BlockSpec((tm, d), row), pl.BlockSpec((tm, d), row),
                   pl.BlockSpec((TOP_K, tm), col), pl.BlockSpec((TOP_K, tm), col),
                   pl.BlockSpec((TOP_K, tm), col), pl.BlockSpec((N_EXPERTS, LANES), const)],
        scratch_shapes=[pltpu.VMEM((N_EXPERTS, 1), F32)],
        compiler_params=_params(("arbitrary",)),
        name="post_attn",
    )(oa, ob, ga, gb, x2, wda, wsb, wo, g, wr_hi, wr_lo, br, tri)


def _dispatch_kernel(dest_ref, zero_blk_ref, n_used_ref, h_ref, xs_ref, zeros, sem, zsem, *, tm, t_total):
    i = pl.program_id(0)
    n_blocks = xs_ref.shape[0] // ROW_BLOCK

    def zero_copy(blk):
        row = pl.multiple_of(blk * ROW_BLOCK, ROW_BLOCK)
        return pltpu.make_async_copy(zeros, xs_ref.at[pl.ds(row, ROW_BLOCK), :], zsem)

    @pl.when(i == 0)
    def _():
        zeros[...] = jnp.zeros(zeros.shape, zeros.dtype)
        n_tail = n_blocks - n_used_ref[0]

        def start(n, carry):
            zero_copy(jnp.where(n < N_EXPERTS, zero_blk_ref[jnp.minimum(n, N_EXPERTS - 1)],
                                n_used_ref[0] + n - N_EXPERTS)).start()
            return carry

        def wait(n, carry):
            zero_copy(0).wait()
            return carry

        lax.fori_loop(0, N_EXPERTS + n_tail, start, 0)
        lax.fori_loop(0, N_EXPERTS + n_tail, wait, 0)

    def row_copy(t, k):
        dst = dest_ref[k * t_total + i * tm + t]
        return pltpu.make_async_copy(h_ref.at[pl.ds(t, 1), :], xs_ref.at[pl.ds(dst, 1), :], sem)

    def issue(t, carry):
        for k in range(TOP_K):
            row_copy(t, k).start()
        return carry

    lax.fori_loop(0, tm, issue, 0, unroll=8)
    for _ in range(TOP_K):
        pltpu.make_async_copy(h_ref, xs_ref.at[pl.ds(0, tm), :], sem).wait()


def _dispatch(dest_flat, zero_blk, n_used, h, p_rows, tm=1024):
    t, d = h.shape
    kern = functools.partial(_dispatch_kernel, tm=tm, t_total=t)
    return pl.pallas_call(
        kern,
        out_shape=jax.ShapeDtypeStruct((p_rows, d), h.dtype),
        grid_spec=pltpu.PrefetchScalarGridSpec(
            num_scalar_prefetch=3,
            grid=(t // tm,),
            in_specs=[pl.BlockSpec((tm, d), lambda i, *_: (i, 0))],
            out_specs=pl.BlockSpec(memory_space=pl.ANY),
            scratch_shapes=[pltpu.VMEM((ROW_BLOCK, d), h.dtype), pltpu.SemaphoreType.DMA,
                            pltpu.SemaphoreType.DMA],
        ),
        compiler_params=_params(("arbitrary",)),
        name="dispatch",
    )(dest_flat, zero_blk, n_used, h)


def _expert_kernel(first_ref, count_ref, n_used_ref, xs_ref, wg_ref, bg_ref, wu_ref, bu_ref, wd_ref, bd_ref,
                   o_ref, wg_bf, wu_bf, wd_bf, xbuf, obuf, in_sem, out_sem, pending_ref):
    e = pl.program_id(0)
    first = first_ref[e]
    count = count_ref[e]

    n_pairs = count // 2
    odd = count % 2

    def rows(blk, nb):
        return pl.ds(pl.multiple_of(blk * ROW_BLOCK, ROW_BLOCK), nb * ROW_BLOCK)

    def fetch(blk, nb, slot):
        return pltpu.make_async_copy(xs_ref.at[rows(blk, nb), :], xbuf.at[slot, pl.ds(0, nb * ROW_BLOCK), :],
                                     in_sem.at[slot])

    def flush(blk, nb, slot):
        return pltpu.make_async_copy(obuf.at[slot, pl.ds(0, nb * ROW_BLOCK), :], o_ref.at[rows(blk, nb), :],
                                     out_sem.at[slot])

    def mlp(x):
        x = x.astype(BF16)
        g = jnp.dot(x, wg_bf[...], preferred_element_type=F32) + bg_ref[0]
        u = jnp.dot(x, wu_bf[...], preferred_element_type=F32) + bu_ref[0]
        g = jnp.minimum(g, SWIGLU_LIMIT)
        u = jnp.clip(u, -SWIGLU_LIMIT, SWIGLU_LIMIT)
        glu = g * jax.nn.sigmoid(SWIGLU_ALPHA * g)
        act = ((u + 1.0) * glu).astype(BF16)
        return (jnp.dot(act, wd_bf[...], preferred_element_type=F32) + bd_ref[0]).astype(obuf.dtype)

    def start_item(first_x, count_x, k):
        @pl.when(2 * k + 1 < count_x)
        def _():
            fetch(first_x + 2 * k, 2, k % 2).start()

        @pl.when(2 * k + 1 == count_x)
        def _():
            fetch(first_x + 2 * k, 1, k % 2).start()

    @pl.when(e == 0)
    def _():
        start_item(first, count, 0)
        start_item(first, count, 1)

    @pl.when(count > 0)
    def _():
        wg_bf[...] = wg_ref[0].astype(BF16)
        wu_bf[...] = wu_ref[0].astype(BF16)
        wd_bf[...] = wd_ref[0].astype(BF16)

    @pl.when(e == 0)
    def _():
        pending_ref[0] = 0
        pending_ref[1] = 0

    def drain(slot):
        for nb in (1, 2):
            @pl.when(pending_ref[slot] == nb)
            def _():
                flush(0, nb, slot).wait()
        pending_ref[slot] = 0

    def write_back(blk, nb, slot):
        flush(blk, nb, slot).start()
        pending_ref[slot] = nb

    def pair(jj, carry):
        slot = jj % 2
        blk = first + 2 * jj
        fetch(blk, 2, slot).wait()
        y = mlp(xbuf[slot])
        start_item(first, count, jj + 2)
        drain(slot)
        obuf[slot] = y
        write_back(blk, 2, slot)
        return carry

    lax.fori_loop(0, n_pairs, pair, 0)
    last = first + 2 * n_pairs
    last_slot = n_pairs % 2

    @pl.when(odd == 1)
    def _():
        fetch(last, 1, last_slot).wait()
        y = mlp(xbuf[last_slot, :ROW_BLOCK, :])
        drain(last_slot)
        obuf[last_slot, :ROW_BLOCK, :] = y
        write_back(last, 1, last_slot)

    @pl.when(e + 1 < pl.num_programs(0))
    def _():
        start_item(first_ref[e + 1], count_ref[e + 1], 0)
        start_item(first_ref[e + 1], count_ref[e + 1], 1)

    @pl.when(e == pl.num_programs(0) - 1)
    def _():
        drain(0)
        drain(1)
        obuf[0, :ROW_BLOCK, :] = jnp.zeros((ROW_BLOCK, obuf.shape[2]), obuf.dtype)
        n_used = n_used_ref[0]
        n_total = o_ref.shape[0] // ROW_BLOCK

        def start(blk, carry):
            flush(blk, 1, 0).start()
            return carry

        def wait(blk, carry):
            flush(blk, 1, 0).wait()
            return carry

        lax.fori_loop(n_used, n_total, start, 0)
        lax.fori_loop(n_used, n_total, wait, 0)


def _experts(first_blk, blk_count, n_used, n_out_blocks, xs, wg, bg, wu, bu, wd, bd):
    d = xs.shape[1]
    n_exp, _, f = wg.shape
    wmap = lambda e, *_: (e, 0, 0)
    return pl.pallas_call(
        _expert_kernel,
        out_shape=jax.ShapeDtypeStruct((n_out_blocks * ROW_BLOCK, d), BF16),
        grid_spec=pltpu.PrefetchScalarGridSpec(
            num_scalar_prefetch=3,
            grid=(n_exp,),
            in_specs=[pl.BlockSpec(memory_space=pl.ANY),
                      pl.BlockSpec((1, d, f), wmap), pl.BlockSpec((1, 1, f), wmap),
                      pl.BlockSpec((1, d, f), wmap), pl.BlockSpec((1, 1, f), wmap),
                      pl.BlockSpec((1, f, d), wmap), pl.BlockSpec((1, 1, d), wmap)],
            out_specs=pl.BlockSpec(memory_space=pl.ANY),
            scratch_shapes=[pltpu.VMEM((d, f), BF16), pltpu.VMEM((d, f), BF16), pltpu.VMEM((f, d), BF16),
                            pltpu.VMEM((2, 2 * ROW_BLOCK, d), F32), pltpu.VMEM((2, 2 * ROW_BLOCK, d), BF16),
                            pltpu.SemaphoreType.DMA((2,)), pltpu.SemaphoreType.DMA((2,)),
                            pltpu.SMEM((2,), jnp.int32)],
        ),
        compiler_params=_params(("arbitrary",)),
        name="experts",
    )(first_blk, blk_count, n_used, xs, wg, bg, wu, bu, wd, bd)


def _combine_kernel(base_ref, npass_ref, rows_ref, idx_ref, lr_ref, gate_ref, x1_ref, g_ref, o_ref,
                    wbuf, sem, *, tm):
    i = pl.program_id(0)
    slot = i % 2
    win = COMBINE_WINDOW
    width = N_EXPERTS * win

    def start_fetch(tile, p, s):
        last_start = rows_ref.shape[0] - win
        for e in range(N_EXPERTS):
            start = jnp.minimum(base_ref[tile * N_EXPERTS + e] + p * win, last_start)
            start = pl.multiple_of(start, WINDOW_ALIGN)
            pltpu.make_async_copy(rows_ref.at[pl.ds(start, win), :],
                                  wbuf.at[s, pl.ds(e * win, win), :], sem.at[s]).start()

    def wait_fetch(s):
        pltpu.make_async_copy(rows_ref.at[pl.ds(0, width), :], wbuf.at[s], sem.at[s]).wait()

    @pl.when(i == 0)
    def _():
        start_fetch(0, 0, 0)

    @pl.when(i + 1 < pl.num_programs(0))
    def _():
        start_fetch(i + 1, 0, 1 - slot)

    idx = idx_ref[...]
    lr = lr_ref[...]
    gates = gate_ref[...]
    col = lax.broadcasted_iota(jnp.int32, (tm, width), 1)

    def weights(p):
        w = jnp.zeros((tm, width), F32)
        for k in range(TOP_K):
            r = lr[:, k:k + 1] - p * win
            tgt = jnp.where((r >= 0) & (r < win), idx[:, k:k + 1] * win + r, -1)
            w = jnp.where(col == tgt, gates[:, k:k + 1], w)
        return w.astype(BF16)

    def gathered(p):
        return jnp.dot(weights(p), wbuf[slot], preferred_element_type=F32)

    wait_fetch(slot)
    y = x1_ref[...] + gathered(0)

    def extra_pass(p, acc):
        start_fetch(i, p, slot)
        wait_fetch(slot)
        return acc + gathered(p)

    y = lax.fori_loop(1, npass_ref[i], extra_pass, y)
    ms = jnp.mean(y * y, axis=-1, keepdims=True)
    o_ref[...] = y * lax.rsqrt(ms + NORM_EPS) * g_ref[...]


def _combine(base_tbl, npass, rows, idx_t, lr_t, gates_t, x1, g, tm):
    t, d = x1.shape
    kern = functools.partial(_combine_kernel, tm=tm)
    tok = lambda i, *_: (i, 0)
    return pl.pallas_call(
        kern,
        out_shape=jax.ShapeDtypeStruct((t, d), F32),
        grid_spec=pltpu.PrefetchScalarGridSpec(
            num_scalar_prefetch=2,
            grid=(t // tm,),
            in_specs=[pl.BlockSpec(memory_space=pl.ANY),
                      pl.BlockSpec((tm, TOP_K), tok), pl.BlockSpec((tm, TOP_K), tok),
                      pl.BlockSpec((tm, TOP_K), tok), pl.BlockSpec((tm, d), tok),
                      pl.BlockSpec((1, d), lambda i, *_: (0, 0))],
            out_specs=pl.BlockSpec((tm, d), tok),
            scratch_shapes=[pltpu.VMEM((2, N_EXPERTS * COMBINE_WINDOW, d), rows.dtype),
                            pltpu.SemaphoreType.DMA((2,))],
        ),
        compiler_params=_params(("arbitrary",)),
        name="combine",
    )(base_tbl, npass, rows, idx_t, lr_t, gates_t, x1, g)


def _rope_tables(seq):
    inv = 1.0 / (ROPE_THETA ** (jnp.arange(0, HEAD_DIM, 2, dtype=F32) / HEAD_DIM))
    ang = jnp.arange(seq, dtype=F32)[:, None] * inv[None, :]
    cos, sin = jnp.cos(ang), jnp.sin(ang)
    return jnp.concatenate([cos] * 4, axis=1), jnp.concatenate([-sin, sin, -sin, sin], axis=1)


def kernel(x, norm_mix_g, w_in, lambda_q1, lambda_k1, lambda_q2, lambda_k2, da_subln_g, w_da_out, w_sb_out, w_o, norm_ffn_g, w_router, b_router, w_gate, b_gate, w_up, b_up, w_down, b_down, norm_final_g):
    b, s, d = x.shape
    depth = w_in.shape[0]
    t = b * s
    cos_t, sin_t = _rope_tables(s)
    n_blocks = (t * TOP_K + ROW_BLOCK - 1) // ROW_BLOCK + N_EXPERTS
    p_rows = n_blocks * ROW_BLOCK
    x2 = x.reshape(t, d)
    for l in range(depth):
        lambda_init = 0.8 - 0.6 * math.exp(-0.3 * l)
        qa, ka, va, qb, kb, vb, ga, gb = _in_proj(
            x2, norm_mix_g[l][None, :], w_in[l].astype(BF16), cos_t, sin_t, s)
        lam_p = jnp.stack([lambda_q1[l], lambda_k1[l], lambda_q2[l], lambda_k2[l]]).astype(F32)
        seq3 = lambda a: a.reshape(b, s, a.shape[1])
        oa = _diff_attn(seq3(qa), seq3(ka), seq3(va), lam_p, da_subln_g[l][None, :].astype(F32), lambda_init)
        ob = _sb_attn(seq3(qb), seq3(kb), seq3(vb))
        wr = w_router[l].astype(F32)
        wr_top = wr.astype(BF16)
        wr_rest = (wr - wr_top.astype(F32)).astype(BF16)
        lane_pad = lambda a: jnp.pad(a, ((0, 0), (0, LANES - a.shape[1])))
        wr_hi = lane_pad(jnp.concatenate([wr_top, wr_rest], axis=1))
        wr_lo = lane_pad(wr_top)
        x1, h, idx, gates, rank, cnt = _post_attn(
            oa.reshape(t, -1), ob.reshape(t, -1), ga, gb, x2,
            w_da_out[l].astype(BF16), w_sb_out[l].astype(BF16), w_o[l].astype(BF16),
            norm_ffn_g[l][None, :], wr_hi, wr_lo, b_router[l][:, None].astype(F32))
        counts = cnt[:, 0].astype(jnp.int32)
        padded = (counts + ROW_BLOCK - 1) // ROW_BLOCK * ROW_BLOCK
        pad_ends = jnp.cumsum(padded)
        pad_starts = pad_ends - padded
        experts = jnp.arange(N_EXPERTS, dtype=jnp.int32)
        chosen = idx[:, :, None] == experts
        base = jnp.sum(jnp.where(chosen, pad_starts, 0), axis=-1)
        dest = (base + rank).reshape(-1)
        n_used = (pad_ends[-1] // ROW_BLOCK).astype(jnp.int32)
        first_blk = (pad_starts // ROW_BLOCK).astype(jnp.int32)
        blk_count = (padded // ROW_BLOCK).astype(jnp.int32)
        zero_blk = jnp.maximum(pad_ends // ROW_BLOCK - 1, 0).astype(jnp.int32)
        n_tiles = t // COMBINE_TILE
        tile_cnt = jnp.sum(chosen.reshape(TOP_K, n_tiles, COMBINE_TILE, N_EXPERTS), axis=(0, 2), dtype=jnp.int32)
        tile_carry = jnp.cumsum(tile_cnt, axis=0) - tile_cnt
        run_start = pad_starts[None, :] + tile_carry
        run_skew = run_start % WINDOW_ALIGN
        run_base = (run_start - run_skew).reshape(-1).astype(jnp.int32)
        shift_tok = jnp.repeat(tile_carry - run_skew, COMBINE_TILE, axis=0)
        local_rank = rank - jnp.sum(jnp.where(chosen, shift_tok[None], 0), axis=-1)
        n_pass = jnp.maximum((jnp.max(tile_cnt + run_skew, axis=1) + COMBINE_WINDOW - 1) // COMBINE_WINDOW,
                             1).astype(jnp.int32)
        xs = _dispatch(dest, zero_blk, n_used[None], h, p_rows)
        rows = _experts(first_blk, blk_count, n_used[None], n_blocks + 1, xs, w_gate[l], b_gate[l][:, None, :],
                        w_up[l], b_up[l][:, None, :], w_down[l], b_down[l][:, None, :])
        g_next = norm_final_g[None, :] if l == depth - 1 else jnp.ones((1, d), F32)
        x2 = _combine(run_base, n_pass, rows, idx.T, local_rank.T.astype(jnp.int32), gates.T, x1, g_next,
                      COMBINE_TILE)
        if l != depth - 1:
            raise NotImplementedError("only the final layer's norm is fused into the combine kernel")
    return x2.reshape(b, s, d)
```

```python
import functools
import math

import jax
import jax.numpy as jnp
from jax import lax
from jax.experimental import pallas as pl
from jax.experimental.pallas import tpu as pltpu

F32 = jnp.float32
BF16 = jnp.bfloat16

DA_HEADS = 4
HEAD_DIM = 64
N_EXPERTS = 32
TOP_K = 4
ROPE_THETA = 10000.0
SWIGLU_LIMIT = 7.0
SWIGLU_ALPHA = 1.702
NORM_EPS = 1e-5
ROW_BLOCK = 256
COMBINE_TILE = 256
WINDOW_ALIGN = 16
COMBINE_WINDOW = 80
LANES = 128
NEG_BIG = -1e30
LOG2E = math.log2(math.e)
SB_DEAD_LOG2 = -150.0

VMEM_LIMIT = 56 * 1024 * 1024


def _params(sem, vmem=VMEM_LIMIT):
    return pltpu.CompilerParams(dimension_semantics=sem, vmem_limit_bytes=vmem)


def _in_proj_kernel(x_ref, g_ref, w_ref, cos_ref, sin_ref,
                    qa_ref, ka_ref, va_ref, qb_ref, kb_ref, vb_ref, ga_ref, gb_ref):
    x = x_ref[...]
    ms = jnp.mean(x * x, axis=-1, keepdims=True)
    h = (x * lax.rsqrt(ms + NORM_EPS) * g_ref[...]).astype(BF16)
    cos = cos_ref[...]
    sin = sin_ref[...]
    lane = lax.broadcasted_iota(jnp.int32, cos.shape, 1)
    first_half = (lane & (HEAD_DIM - 1)) < HEAD_DIM // 2

    def proj(c0, width):
        return jnp.dot(h, w_ref[:, c0:c0 + width], preferred_element_type=F32)

    def rope(r):
        outs = []
        for g in range(r.shape[1] // LANES):
            xg = r[:, g * LANES:(g + 1) * LANES]
            rot = jnp.where(first_half, pltpu.roll(xg, LANES - HEAD_DIM // 2, 1),
                            pltpu.roll(xg, HEAD_DIM // 2, 1))
            outs.append(xg * cos + rot * sin)
        return jnp.concatenate(outs, axis=1)

    scale = HEAD_DIM ** -0.5 * LOG2E
    w = qa_ref.shape[1]
    d = ga_ref.shape[1]
    qa_ref[...] = (rope(proj(0, w)) * scale).astype(BF16)
    ka_ref[...] = rope(proj(w, w)).astype(BF16)
    va_ref[...] = proj(2 * w, w).astype(BF16)
    qb_ref[...] = (proj(3 * w, w) * scale).astype(BF16)
    kb_ref[...] = proj(4 * w, w).astype(BF16)
    vb_ref[...] = proj(5 * w, w).astype(BF16)
    ga_ref[...] = jax.nn.sigmoid(proj(6 * w, d)).astype(BF16)
    gb_ref[...] = jax.nn.sigmoid(proj(6 * w + d, d)).astype(BF16)


def _in_proj(x2, g, w_in_bf, cos_t, sin_t, seq, tm=512):
    t, d = x2.shape
    w = 512
    nseq = seq // tm
    outs = [jax.ShapeDtypeStruct((t, w), BF16)] * 6 + [jax.ShapeDtypeStruct((t, d), BF16)] * 2
    row = lambda i: (i, 0)
    return pl.pallas_call(
        _in_proj_kernel,
        out_shape=outs,
        grid=(t // tm,),
        in_specs=[
            pl.BlockSpec((tm, d), row),
            pl.BlockSpec((1, d), lambda i: (0, 0)),
            pl.BlockSpec(w_in_bf.shape, lambda i: (0, 0)),
            pl.BlockSpec((tm, LANES), lambda i: (i % nseq, 0)),
            pl.BlockSpec((tm, LANES), lambda i: (i % nseq, 0)),
        ],
        out_specs=[pl.BlockSpec((tm, w), row)] * 6 + [pl.BlockSpec((tm, d), row)] * 2,
        compiler_params=_params(("arbitrary",)),
        name="in_proj",
    )(x2, g, w_in_bf, cos_t, sin_t)


def _lane_groups(x):
    return [x[:, g * LANES:(g + 1) * LANES] for g in range(x.shape[1] // LANES)]


def _da_kernel(q_ref, k_ref, v_ref, lam_ref, subg_ref, o_ref, s_buf, mx_ref, m_ref, acc_ref,
               *, tq, tk, lambda_init):
    i = pl.program_id(2)
    q = q_ref[0]
    lane = lax.broadcasted_iota(jnp.int32, q.shape, 1)
    zero = jnp.zeros_like(q)
    qs = (jnp.where(lane < HEAD_DIM, q, zero), jnp.where(lane >= HEAD_DIM, q, zero))
    mx_ref[...] = jnp.full(mx_ref.shape, NEG_BIG, F32)
    acc_ref[...] = jnp.zeros(acc_ref.shape, F32)

    def score_blocks(js, diag_last):
        maxes = [mx_ref[0], mx_ref[1]]
        for n, j in enumerate(js):
            k = k_ref[0, pl.ds(pl.multiple_of(j * tk, tk), tk), :]
            for h in range(2):
                s = lax.dot_general(qs[h], k, (((1,), (1,)), ((), ())), preferred_element_type=F32)
                if diag_last and n == len(js) - 1:
                    rows = lax.broadcasted_iota(jnp.int32, s.shape, 0)
                    cols = lax.broadcasted_iota(jnp.int32, s.shape, 1)
                    s = jnp.where(rows >= cols, s, NEG_BIG)
                s_buf[h, j] = s
                for sg in _lane_groups(s):
                    maxes[h] = jnp.maximum(maxes[h], sg)
        if diag_last:
            for h in range(2):
                m_ref[h] = jnp.broadcast_to(jnp.max(maxes[h], axis=1, keepdims=True), (tq, LANES))
        else:
            mx_ref[0], mx_ref[1] = maxes

    def score_body(jj, carry):
        score_blocks([2 * jj, 2 * jj + 1], False)
        return carry

    lax.fori_loop(0, i // 2, score_body, 0)

    @pl.when(i % 2 == 0)
    def _():
        score_blocks([i], True)

    @pl.when(i % 2 == 1)
    def _():
        score_blocks([i - 1, i], True)

    def pv_blocks(js):
        vs = [v_ref[0, pl.ds(pl.multiple_of(j * tk, tk), tk), :] for j in js]
        v_all = jnp.concatenate(vs, axis=0)
        v_ext = jnp.concatenate([v_all, jnp.ones_like(v_all)], axis=1)
        accs = [acc_ref[0], acc_ref[1]]
        for h in range(2):
            m = m_ref[h]
            p = jnp.concatenate([jnp.exp2(sg - m).astype(BF16)
                                 for j in js for sg in _lane_groups(s_buf[h, j])], axis=1)
            accs[h] = accs[h] + jnp.dot(p, v_ext, preferred_element_type=F32)
        acc_ref[0], acc_ref[1] = accs

    def pv_body(jj, carry):
        pv_blocks([2 * jj, 2 * jj + 1])
        return carry

    lax.fori_loop(0, (i + 1) // 2, pv_body, 0)

    @pl.when(i % 2 == 0)
    def _():
        pv_blocks([i])

    lam_p = lam_ref[...]
    lam = (jnp.exp(jnp.sum(lam_p[0:1] * lam_p[1:2], axis=1, keepdims=True))
           - jnp.exp(jnp.sum(lam_p[2:3] * lam_p[3:4], axis=1, keepdims=True)) + lambda_init)
    a0, a1 = acc_ref[0], acc_ref[1]
    o = a0[:, :LANES] / a0[:, LANES:] - lam * (a1[:, :LANES] / a1[:, LANES:])
    ms = jnp.mean(o * o, axis=-1, keepdims=True)
    y = o * lax.rsqrt(ms + NORM_EPS) * subg_ref[...] * (1.0 - lambda_init)
    o_ref[0] = y.astype(o_ref.dtype)


def _diff_attn(qa, ka, va, lam_p, subg, lambda_init, tq=512):
    b, s, w = qa.shape
    nh = w // LANES
    kern = functools.partial(_da_kernel, tq=tq, tk=tq, lambda_init=lambda_init)
    qspec = pl.BlockSpec((1, tq, LANES), lambda bb, h, i: (bb, i, h))
    kvspec = pl.BlockSpec((1, s, LANES), lambda bb, h, i: (bb, 0, h))
    return pl.pallas_call(
        kern,
        out_shape=jax.ShapeDtypeStruct((b, s, w), BF16),
        grid=(b, nh, s // tq),
        in_specs=[qspec, kvspec, kvspec,
                  pl.BlockSpec(lam_p.shape, lambda bb, h, i: (0, 0)),
                  pl.BlockSpec(subg.shape, lambda bb, h, i: (0, 0))],
        out_specs=qspec,
        scratch_shapes=[pltpu.VMEM((2, s // tq, tq, tq), F32),
                        pltpu.VMEM((2, tq, LANES), F32),
                        pltpu.VMEM((2, tq, LANES), F32),
                        pltpu.VMEM((2, tq, 2 * LANES), F32)],
        compiler_params=_params(("arbitrary",) * 3),
        name="diff_attn",
    )(qa, ka, va, lam_p, subg)


def _sb_kernel(q_ref, k_ref, v_ref, tri_ref, o_ref, c_ref, acc_ref, *, tq, tk):
    i = pl.program_id(2)
    n_tiles = tq // tk
    tri = tri_ref[...]
    lane = lax.broadcasted_iota(jnp.int32, (tk, LANES), 1)
    qs = []
    for r in range(n_tiles):
        q = q_ref[0, r * tk:(r + 1) * tk, :]
        zero = jnp.zeros_like(q)
        qs.append((jnp.where(lane < HEAD_DIM, q, zero), jnp.where(lane >= HEAD_DIM, q, zero)))

    def load_kv(j):
        rows = pl.ds(pl.multiple_of(j * tk, tk), tk)
        return k_ref[0, rows, :], v_ref[0, rows, :]

    def window(jobs):
        keys = [(r, h) for r, _, _ in jobs for h in range(2)]
        fresh = jnp.zeros((tk, LANES), F32)
        state = {(r, h): (fresh, fresh) if diag_first else (c_ref[r, h], acc_ref[r, h])
                 for r, _, diag_first in jobs for h in range(2)}
        new_state = {}
        for r, kvs, diag_first in jobs:
            v_all = jnp.concatenate([v for _, v in kvs], axis=0)
            for h in range(2):
                c_run, acc_prev = state[(r, h)]
                weights = []
                for n, (k, _) in enumerate(kvs):
                    masked = diag_first and n == 0
                    z = lax.dot_general(qs[r][h], k, (((1,), (1,)), ((), ())), preferred_element_type=F32)
                    soft = jnp.maximum(z, 0.0) + jnp.log2(1.0 + jnp.exp2(-jnp.abs(z)))
                    log_b = z - soft
                    if masked:
                        mask = (lax.broadcasted_iota(jnp.int32, z.shape, 1)
                                < lax.broadcasted_iota(jnp.int32, z.shape, 0))
                        soft = jnp.where(mask, soft, 0.0)
                    res = jnp.dot(soft.astype(BF16), tri, preferred_element_type=F32)
                    suffix, total = res[:, :tk], res[:, tk:]
                    a = jnp.concatenate([jnp.exp2(lb + sf + c_run)
                                         for lb, sf in zip(_lane_groups(log_b), _lane_groups(suffix))], axis=1)
                    if masked:
                        a = jnp.where(mask, a, 0.0)
                    weights.append(a.astype(BF16))
                    c_run = c_run + total
                a_all = jnp.concatenate(weights, axis=1)
                new_state[(r, h)] = (c_run, acc_prev + jnp.dot(a_all, v_all, preferred_element_type=F32))
        for key in keys:
            c_ref[key], acc_ref[key] = new_state[key]

    first = i * n_tiles

    @pl.when(i == 0)
    def _():
        blocks = [load_kv(r) for r in range(n_tiles)]
        window([(r, blocks[max(r - 1, 0):r + 1][::-1], True) for r in range(n_tiles)])

    @pl.when(i > 0)
    def _():
        blocks = [load_kv(first - 1 + n) for n in range(n_tiles + 1)]
        window([(r, [blocks[r + 1], blocks[r]], True) for r in range(n_tiles)])

    for r in range(n_tiles):
        def live(r=r):
            return jnp.max(jnp.maximum(c_ref[r, 0], c_ref[r, 1])) > SB_DEAD_LOG2

        def cond(carry):
            j, alive = carry
            return jnp.logical_and(j >= 0, alive)

        def body(carry, r=r, live=live):
            j, _ = carry
            window([(r, [load_kv(j)], False)])
            return j - 1, live()

        lax.while_loop(cond, body, (first + r - 2, live()))
        o_ref[0, r * tk:(r + 1) * tk, :] = jnp.where(lane < HEAD_DIM, acc_ref[r, 0],
                                                     acc_ref[r, 1]).astype(o_ref.dtype)


def _sb_attn(qb, kb, vb, tq=512, tk=256):
    b, s, w = qb.shape
    nh = w // LANES
    r = lax.broadcasted_iota(jnp.int32, (tk, tk), 0)
    c = lax.broadcasted_iota(jnp.int32, (tk, tk), 1)
    tri = -jnp.concatenate([(r > c).astype(BF16), jnp.ones((tk, LANES), BF16)], axis=1)
    kern = functools.partial(_sb_kernel, tq=tq, tk=tk)
    qspec = pl.BlockSpec((1, tq, LANES), lambda bb, h, i: (bb, i, h))
    kvspec = pl.BlockSpec((1, s, LANES), lambda bb, h, i: (bb, 0, h))
    return pl.pallas_call(
        kern,
        out_shape=jax.ShapeDtypeStruct((b, s, w), BF16),
        grid=(b, nh, s // tq),
        in_specs=[qspec, kvspec, kvspec, pl.BlockSpec(tri.shape, lambda bb, h, i: (0, 0))],
        out_specs=qspec,
        scratch_shapes=[pltpu.VMEM((tq // tk, 2, tk, LANES), F32), pltpu.VMEM((tq // tk, 2, tk, LANES), F32)],
        compiler_params=_params(("arbitrary",) * 3),
        name="sb_attn",
    )(qb, kb, vb, tri)


def _post_attn_kernel(oa_ref, ob_ref, ga_ref, gb_ref, x_ref, wda_ref, wsb_ref, wo_ref, g_ref,
                      wr_hi_ref, wr_lo_ref, br_ref, tri_ref,
                      x1_ref, h_ref, idx_ref, gate_ref, rank_ref, cnt_ref, carry_ref):
    i = pl.program_id(0)

    @pl.when(i == 0)
    def _():
        carry_ref[...] = jnp.zeros(carry_ref.shape, F32)

    ya = jnp.dot(oa_ref[...], wda_ref[...], preferred_element_type=F32)
    yb = jnp.dot(ob_ref[...], wsb_ref[...], preferred_element_type=F32)
    mix = ga_ref[...].astype(F32) * ya + gb_ref[...].astype(F32) * yb
    x1 = x_ref[...] + jnp.dot(mix.astype(BF16), wo_ref[...], preferred_element_type=F32)
    x1_ref[...] = x1
    ms = jnp.mean(x1 * x1, axis=-1, keepdims=True)
    h = x1 * lax.rsqrt(ms + NORM_EPS) * g_ref[...]
    h_ref[...] = h

    h_hi = h.astype(BF16)
    h_lo = (h - h_hi.astype(F32)).astype(BF16)
    ne = br_ref.shape[0]
    p_hi = jnp.dot(h_hi, wr_hi_ref[...], preferred_element_type=F32)
    p_lo = jnp.dot(h_lo, wr_lo_ref[...], preferred_element_type=F32)
    by_token = p_hi + pltpu.roll(p_hi, LANES - ne, 1) + p_lo
    logits = by_token.T[:ne] + br_ref[...]
    tm = logits.shape[1]
    eid = lax.broadcasted_iota(jnp.int32, (ne, tm), 0).astype(F32)
    vals, ids = [], []
    work = logits
    for _ in range(TOP_K):
        mx = jnp.max(work, axis=0, keepdims=True)
        sel = jnp.min(jnp.where(work == mx, eid, float(ne)), axis=0, keepdims=True)
        vals.append(mx)
        ids.append(sel)
        work = jnp.where(eid == sel, -jnp.inf, work)
    exps = [jnp.exp(v - vals[0]) for v in vals]
    denom = exps[0] + exps[1] + exps[2] + exps[3]
    onehots = [(eid == sel).astype(F32) for sel in ids]
    assigned = onehots[0] + onehots[1] + onehots[2] + onehots[3]
    before = jnp.dot(assigned.astype(BF16), tri_ref[...], preferred_element_type=F32) + carry_ref[...]
    for r in range(TOP_K):
        idx_ref[r:r + 1, :] = ids[r].astype(jnp.int32)
        gate_ref[r:r + 1, :] = exps[r] / denom
        rank_ref[r:r + 1, :] = jnp.sum(onehots[r] * before, axis=0, keepdims=True).astype(jnp.int32)
    carry = carry_ref[...] + jnp.sum(assigned, axis=1, keepdims=True)
    carry_ref[...] = carry
    cnt_ref[...] = jnp.broadcast_to(carry, cnt_ref.shape)


def _post_attn(oa, ob, ga, gb, x2, wda, wsb, wo, g, wr_hi, wr_lo, br, tm=512):
    t, d = x2.shape
    w = oa.shape[1]
    r = lax.broadcasted_iota(jnp.int32, (tm, tm), 0)
    c = lax.broadcasted_iota(jnp.int32, (tm, tm), 1)
    tri = (r < c).astype(BF16)
    row = lambda i: (i, 0)
    col = lambda i: (0, i)
    const = lambda i: (0, 0)
    full = lambda a: pl.BlockSpec(a.shape, const)
    return pl.pallas_call(
        _post_attn_kernel,
        out_shape=[jax.ShapeDtypeStruct((t, d), F32), jax.ShapeDtypeStruct((t, d), F32),
                   jax.ShapeDtypeStruct((TOP_K, t), jnp.int32), jax.ShapeDtypeStruct((TOP_K, t), F32),
                   jax.ShapeDtypeStruct((TOP_K, t), jnp.int32),
                   jax.ShapeDtypeStruct((N_EXPERTS, LANES), F32)],
        grid=(t // tm,),
        in_specs=[pl.BlockSpec((tm, w), row), pl.BlockSpec((tm, w), row),
                  pl.BlockSpec((tm, d), row), pl.BlockSpec((tm, d), row), pl.BlockSpec((tm, d), row),
                  full(wda), full(wsb), full(wo), full(g), full(wr_hi), full(wr_lo), full(br), full(tri)],
        out_specs=[pl.BlockSpec((tm, d), row), pl.BlockSpec((tm, d), row),
                   pl.BlockSpec((TOP_K, tm), col), pl.BlockSpec((TOP_K, tm), col),
                   pl.BlockSpec((TOP_K, tm), col), pl.BlockSpec((N_EXPERTS, LANES), const)],
        scratch_shapes=[pltpu.VMEM((N_EXPERTS, 1), F32)],
        compiler_params=_params(("arbitrary",)),
        name="post_attn",
    )(oa, ob, ga, gb, x2, wda, wsb, wo, g, wr_hi, wr_lo, br, tri)


def _dispatch_kernel(dest_ref, zero_blk_ref, n_used_ref, h_ref, xs_ref, zeros, sem, zsem, *, tm, t_total):
    i = pl.program_id(0)
    n_blocks = xs_ref.shape[0] // ROW_BLOCK

    def zero_copy(blk):
        row = pl.multiple_of(blk * ROW_BLOCK, ROW_BLOCK)
        return pltpu.make_async_copy(zeros, xs_ref.at[pl.ds(row, ROW_BLOCK), :], zsem)

    @pl.when(i == 0)
    def _():
        zeros[...] = jnp.zeros(zeros.shape, zeros.dtype)
        n_tail = n_blocks - n_used_ref[0]

        def start(n, carry):
            zero_copy(jnp.where(n < N_EXPERTS, zero_blk_ref[jnp.minimum(n, N_EXPERTS - 1)],
                                n_used_ref[0] + n - N_EXPERTS)).start()
            return carry

        def wait(n, carry):
            zero_copy(0).wait()
            return carry

        lax.fori_loop(0, N_EXPERTS + n_tail, start, 0)
        lax.fori_loop(0, N_EXPERTS + n_tail, wait, 0)

    def row_copy(t, k):
        dst = dest_ref[k * t_total + i * tm + t]
        return pltpu.make_async_copy(h_ref.at[pl.ds(t, 1), :], xs_ref.at[pl.ds(dst, 1), :], sem)

    def issue(t, carry):
        for k in range(TOP_K):
            row_copy(t, k).start(priority=k % 2)
        return carry

    lax.fori_loop(0, tm, issue, 0, unroll=8)
    for _ in range(TOP_K):
        pltpu.make_async_copy(h_ref, xs_ref.at[pl.ds(0, tm), :], sem).wait()


def _dispatch(dest_flat, zero_blk, n_used, h, p_rows, tm=1024):
    t, d = h.shape
    kern = functools.partial(_dispatch_kernel, tm=tm, t_total=t)
    return pl.pallas_call(
        kern,
        out_shape=jax.ShapeDtypeStruct((p_rows, d), h.dtype),
        grid_spec=pltpu.PrefetchScalarGridSpec(
            num_scalar_prefetch=3,
            grid=(t // tm,),
            in_specs=[pl.BlockSpec((tm, d), lambda i, *_: (i, 0))],
            out_specs=pl.BlockSpec(memory_space=pl.ANY),
            scratch_shapes=[pltpu.VMEM((ROW_BLOCK, d), h.dtype), pltpu.SemaphoreType.DMA,
                            pltpu.SemaphoreType.DMA],
        ),
        compiler_params=_params(("arbitrary",)),
        name="dispatch",
    )(dest_flat, zero_blk, n_used, h)


def _expert_kernel(first_ref, count_ref, n_used_ref, xs_ref, wg_ref, bg_ref, wu_ref, bu_ref, wd_ref, bd_ref,
                   o_ref, wg_bf, wu_bf, wd_bf, xbuf, obuf, in_sem, out_sem, pending_ref):
    e = pl.program_id(0)
    first = first_ref[e]
    count = count_ref[e]

    n_pairs = count // 2
    odd = count % 2

    def rows(blk, nb):
        return pl.ds(pl.multiple_of(blk * ROW_BLOCK, ROW_BLOCK), nb * ROW_BLOCK)

    def fetch(blk, nb, slot):
        return pltpu.make_async_copy(xs_ref.at[rows(blk, nb), :], xbuf.at[slot, pl.ds(0, nb * ROW_BLOCK), :],
                                     in_sem.at[slot])

    def flush(blk, nb, slot):
        return pltpu.make_async_copy(obuf.at[slot, pl.ds(0, nb * ROW_BLOCK), :], o_ref.at[rows(blk, nb), :],
                                     out_sem.at[slot])

    def mlp(x):
        x = x.astype(BF16)
        g = jnp.dot(x, wg_bf[...], preferred_element_type=F32) + bg_ref[0]
        u = jnp.dot(x, wu_bf[...], preferred_element_type=F32) + bu_ref[0]
        g = jnp.minimum(g, SWIGLU_LIMIT)
        u = jnp.clip(u, -SWIGLU_LIMIT, SWIGLU_LIMIT)
        glu = g * jax.nn.sigmoid(SWIGLU_ALPHA * g)
        act = ((u + 1.0) * glu).astype(BF16)
        return (jnp.dot(act, wd_bf[...], preferred_element_type=F32) + bd_ref[0]).astype(obuf.dtype)

    def start_item(first_x, count_x, k):
        @pl.when(2 * k + 1 < count_x)
        def _():
            fetch(first_x + 2 * k, 2, k % 2).start()

        @pl.when(2 * k + 1 == count_x)
        def _():
            fetch(first_x + 2 * k, 1, k % 2).start()

    @pl.when(e == 0)
    def _():
        start_item(first, count, 0)
        start_item(first, count, 1)

    @pl.when(count > 0)
    def _():
        wg_bf[...] = wg_ref[0].astype(BF16)
        wu_bf[...] = wu_ref[0].astype(BF16)
        wd_bf[...] = wd_ref[0].astype(BF16)

    @pl.when(e == 0)
    def _():
        pending_ref[0] = 0
        pending_ref[1] = 0

    def drain(slot):
        for nb in (1, 2):
            @pl.when(pending_ref[slot] == nb)
            def _():
                flush(0, nb, slot).wait()
        pending_ref[slot] = 0

    def write_back(blk, nb, slot):
        flush(blk, nb, slot).start()
        pending_ref[slot] = nb

    def pair(jj, carry):
        slot = jj % 2
        blk = first + 2 * jj
        fetch(blk, 2, slot).wait()
        y = mlp(xbuf[slot])
        start_item(first, count, jj + 2)
        drain(slot)
        obuf[slot] = y
        write_back(blk, 2, slot)
        return carry

    lax.fori_loop(0, n_pairs, pair, 0)
    last = first + 2 * n_pairs
    last_slot = n_pairs % 2

    @pl.when(odd == 1)
    def _():
        fetch(last, 1, last_slot).wait()
        y = mlp(xbuf[last_slot, :ROW_BLOCK, :])
        drain(last_slot)
        obuf[last_slot, :ROW_BLOCK, :] = y
        write_back(last, 1, last_slot)

    @pl.when(e + 1 < pl.num_programs(0))
    def _():
        start_item(first_ref[e + 1], count_ref[e + 1], 0)
        start_item(first_ref[e + 1], count_ref[e + 1], 1)

    @pl.when(e == pl.num_programs(0) - 1)
    def _():
        drain(0)
        drain(1)
        obuf[0, :ROW_BLOCK, :] = jnp.zeros((ROW_BLOCK, obuf.shape[2]), obuf.dtype)
        n_used = n_used_ref[0]
        n_total = o_ref.shape[0] // ROW_BLOCK

        def start(blk, carry):
            flush(blk, 1, 0).start()
            return carry

        def wait(blk, carry):
            flush(blk, 1, 0).wait()
            return carry

        lax.fori_loop(n_used, n_total, start, 0)
        lax.fori_loop(n_used, n_total, wait, 0)


def _experts(first_blk, blk_count, n_used, n_out_blocks, xs, wg, bg, wu, bu, wd, bd):
    d = xs.shape[1]
    n_exp, _, f = wg.shape
    wmap = lambda e, *_: (e, 0, 0)
    return pl.pallas_call(
        _expert_kernel,
        out_shape=jax.ShapeDtypeStruct((n_out_blocks * ROW_BLOCK, d), BF16),
        grid_spec=pltpu.PrefetchScalarGridSpec(
            num_scalar_prefetch=3,
            grid=(n_exp,),
            in_specs=[pl.BlockSpec(memory_space=pl.ANY),
                      pl.BlockSpec((1, d, f), wmap), pl.BlockSpec((1, 1, f), wmap),
                      pl.BlockSpec((1, d, f), wmap), pl.BlockSpec((1, 1, f), wmap),
                      pl.BlockSpec((1, f, d), wmap), pl.BlockSpec((1, 1, d), wmap)],
            out_specs=pl.BlockSpec(memory_space=pl.ANY),
            scratch_shapes=[pltpu.VMEM((d, f), BF16), pltpu.VMEM((d, f), BF16), pltpu.VMEM((f, d), BF16),
                            pltpu.VMEM((2, 2 * ROW_BLOCK, d), F32), pltpu.VMEM((2, 2 * ROW_BLOCK, d), BF16),
                            pltpu.SemaphoreType.DMA((2,)), pltpu.SemaphoreType.DMA((2,)),
                            pltpu.SMEM((2,), jnp.int32)],
        ),
        compiler_params=_params(("arbitrary",)),
        name="experts",
    )(first_blk, blk_count, n_used, xs, wg, bg, wu, bu, wd, bd)


def _combine_kernel(base_ref, npass_ref, rows_ref, idx_ref, lr_ref, gate_ref, x1_ref, g_ref, o_ref,
                    wbuf, sem, *, tm):
    i = pl.program_id(0)
    slot = i % 2
    win = COMBINE_WINDOW
    width = N_EXPERTS * win

    def start_fetch(tile, p, s):
        last_start = rows_ref.shape[0] - win
        for e in range(N_EXPERTS):
            start = jnp.minimum(base_ref[tile * N_EXPERTS + e] + p * win, last_start)
            start = pl.multiple_of(start, WINDOW_ALIGN)
            pltpu.make_async_copy(rows_ref.at[pl.ds(start, win), :],
                                  wbuf.at[s, pl.ds(e * win, win), :], sem.at[s]).start()

    def wait_fetch(s):
        pltpu.make_async_copy(rows_ref.at[pl.ds(0, width), :], wbuf.at[s], sem.at[s]).wait()

    @pl.when(i == 0)
    def _():
        start_fetch(0, 0, 0)

    @pl.when(i + 1 < pl.num_programs(0))
    def _():
        start_fetch(i + 1, 0, 1 - slot)

    idx = idx_ref[...]
    lr = lr_ref[...]
    gates = gate_ref[...]
    col = lax.broadcasted_iota(jnp.int32, (tm, width), 1)

    def weights(p):
        w = jnp.zeros((tm, width), F32)
        for k in range(TOP_K):
            r = lr[:, k:k + 1] - p * win
            tgt = jnp.where((r >= 0) & (r < win), idx[:, k:k + 1] * win + r, -1)
            w = jnp.where(col == tgt, gates[:, k:k + 1], w)
        return w.astype(BF16)

    def gathered(p):
        return jnp.dot(weights(p), wbuf[slot], preferred_element_type=F32)

    wait_fetch(slot)
    y = x1_ref[...] + gathered(0)

    def extra_pass(p, acc):
        start_fetch(i, p, slot)
        wait_fetch(slot)
        return acc + gathered(p)

    y = lax.fori_loop(1, npass_ref[i], extra_pass, y)
    ms = jnp.mean(y * y, axis=-1, keepdims=True)
    o_ref[...] = y * lax.rsqrt(ms + NORM_EPS) * g_ref[...]


def _combine(base_tbl, npass, rows, idx_t, lr_t, gates_t, x1, g, tm):
    t, d = x1.shape
    kern = functools.partial(_combine_kernel, tm=tm)
    tok = lambda i, *_: (i, 0)
    return pl.pallas_call(
        kern,
        out_shape=jax.ShapeDtypeStruct((t, d), F32),
        grid_spec=pltpu.PrefetchScalarGridSpec(
            num_scalar_prefetch=2,
            grid=(t // tm,),
            in_specs=[pl.BlockSpec(memory_space=pl.ANY),
                      pl.BlockSpec((tm, TOP_K), tok), pl.BlockSpec((tm, TOP_K), tok),
                      pl.BlockSpec((tm, TOP_K), tok), pl.BlockSpec((tm, d), tok),
                      pl.BlockSpec((1, d), lambda i, *_: (0, 0))],
            out_specs=pl.BlockSpec((tm, d), tok),
            scratch_shapes=[pltpu.VMEM((2, N_EXPERTS * COMBINE_WINDOW, d), rows.dtype),
                            pltpu.SemaphoreType.DMA((2,))],
        ),
        compiler_params=_params(("arbitrary",)),
        name="combine",
    )(base_tbl, npass, rows, idx_t, lr_t, gates_t, x1, g)


def _rope_tables(seq):
    inv = 1.0 / (ROPE_THETA ** (jnp.arange(0, HEAD_DIM, 2, dtype=F32) / HEAD_DIM))
    ang = jnp.arange(seq, dtype=F32)[:, None] * inv[None, :]
    cos, sin = jnp.cos(ang), jnp.sin(ang)
    return jnp.concatenate([cos] * 4, axis=1), jnp.concatenate([-sin, sin, -sin, sin], axis=1)


def kernel(x, norm_mix_g, w_in, lambda_q1, lambda_k1, lambda_q2, lambda_k2, da_subln_g, w_da_out, w_sb_out, w_o, norm_ffn_g, w_router, b_router, w_gate, b_gate, w_up, b_up, w_down, b_down, norm_final_g):
    b, s, d = x.shape
    depth = w_in.shape[0]
    t = b * s
    cos_t, sin_t = _rope_tables(s)
    n_blocks = (t * TOP_K + ROW_BLOCK - 1) // ROW_BLOCK + N_EXPERTS
    p_rows = n_blocks * ROW_BLOCK
    x2 = x.reshape(t, d)
    for l in range(depth):
        lambda_init = 0.8 - 0.6 * math.exp(-0.3 * l)
        qa, ka, va, qb, kb, vb, ga, gb = _in_proj(
            x2, norm_mix_g[l][None, :], w_in[l].astype(BF16), cos_t, sin_t, s)
        lam_p = jnp.stack([lambda_q1[l], lambda_k1[l], lambda_q2[l], lambda_k2[l]]).astype(F32)
        seq3 = lambda a: a.reshape(b, s, a.shape[1])
        oa = _diff_attn(seq3(qa), seq3(ka), seq3(va), lam_p, da_subln_g[l][None, :].astype(F32), lambda_init)
        ob = _sb_attn(seq3(qb), seq3(kb), seq3(vb))
        wr = w_router[l].astype(F32)
        wr_top = wr.astype(BF16)
        wr_rest = (wr - wr_top.astype(F32)).astype(BF16)
        lane_pad = lambda a: jnp.pad(a, ((0, 0), (0, LANES - a.shape[1])))
        wr_hi = lane_pad(jnp.concatenate([wr_top, wr_rest], axis=1))
        wr_lo = lane_pad(wr_top)
        x1, h, idx, gates, rank, cnt = _post_attn(
            oa.reshape(t, -1), ob.reshape(t, -1), ga, gb, x2,
            w_da_out[l].astype(BF16), w_sb_out[l].astype(BF16), w_o[l].astype(BF16),
            norm_ffn_g[l][None, :], wr_hi, wr_lo, b_router[l][:, None].astype(F32))
        counts = cnt[:, 0].astype(jnp.int32)
        padded = (counts + ROW_BLOCK - 1) // ROW_BLOCK * ROW_BLOCK
        pad_ends = jnp.cumsum(padded)
        pad_starts = pad_ends - padded
        experts = jnp.arange(N_EXPERTS, dtype=jnp.int32)
        chosen = idx[:, :, None] == experts
        base = jnp.sum(jnp.where(chosen, pad_starts, 0), axis=-1)
        dest = (base + rank).reshape(-1)
        n_used = (pad_ends[-1] // ROW_BLOCK).astype(jnp.int32)
        first_blk = (pad_starts // ROW_BLOCK).astype(jnp.int32)
        blk_count = (padded // ROW_BLOCK).astype(jnp.int32)
        zero_blk = jnp.maximum(pad_ends // ROW_BLOCK - 1, 0).astype(jnp.int32)
        n_tiles = t // COMBINE_TILE
        tile_cnt = jnp.sum(chosen.reshape(TOP_K, n_tiles, COMBINE_TILE, N_EXPERTS), axis=(0, 2), dtype=jnp.int32)
        tile_carry = jnp.cumsum(tile_cnt, axis=0) - tile_cnt
        run_start = pad_starts[None, :] + tile_carry
        run_skew = run_start % WINDOW_ALIGN
        run_base = (run_start - run_skew).reshape(-1).astype(jnp.int32)
        shift_tok = jnp.repeat(tile_carry - run_skew, COMBINE_TILE, axis=0)
        local_rank = rank - jnp.sum(jnp.where(chosen, shift_tok[None], 0), axis=-1)
        n_pass = jnp.maximum((jnp.max(tile_cnt + run_skew, axis=1) + COMBINE_WINDOW - 1) // COMBINE_WINDOW,
                             1).astype(jnp.int32)
        xs = _dispatch(dest, zero_blk, n_used[None], h, p_rows)
        rows = _experts(first_blk, blk_count, n_used[None], n_blocks + 1, xs, w_gate[l], b_gate[l][:, None, :],
                        w_up[l], b_up[l][:, None, :], w_down[l], b_down[l][:, None, :])
        g_next = norm_final_g[None, :] if l == depth - 1 else jnp.ones((1, d), F32)
        x2 = _combine(run_base, n_pass, rows, idx.T, local_rank.T.astype(jnp.int32), gates.T, x1, g_next,
                      COMBINE_TILE)
        if l != depth - 1:
            raise NotImplementedError("only the final layer's norm is fused into the combine kernel")
    return x2.reshape(b, s, d)
```
